```python
import jax, jax.numpy as jnp
from jax import lax
import numpy as np

D_MODEL = 1024
BATCH = 32
SEQ = 256
DEPTH = 2
DEC_BATCH = 2
DEC_SEQ = 4096
PAST_LEN = 256

GRID_W = 64
NA_HEADS = 8
NA_HEAD_DIM = 64
NA_WIDTH = NA_HEADS * NA_HEAD_DIM
NA_KH = 8
NA_KW = 16
CONV_CH = 512
CONV_K = 31
GQ_HEADS = 8
GQ_KV_HEADS = 2
GQ_HEAD_DIM = 64
GQ_Q_WIDTH = GQ_HEADS * GQ_HEAD_DIM
GQ_KV_WIDTH = GQ_KV_HEADS * GQ_HEAD_DIM
Q_BLOCK = 128
ROPE_THETA = 10000.0
N_BRANCH = 3
IN_SIZES = (NA_WIDTH, NA_WIDTH, NA_WIDTH, 2 * CONV_CH, GQ_Q_WIDTH, GQ_KV_WIDTH, GQ_KV_WIDTH, N_BRANCH * D_MODEL)
IN_COLS = sum(IN_SIZES)
N_EXPERTS = 16
N_EXPERT_GROUPS = 4
EXPERTS_PER_GROUP = N_EXPERTS // N_EXPERT_GROUPS
TOP_K = 2
D_EXPERT = 512
ALPHA = (2 * DEPTH) ** 0.25
BETA = (8 * DEPTH) ** -0.25
LN_EPS = 1e-6
RMS_EPS = 1e-6
NEG_INF = -1e30

kernel_name = 'hybrid_natten_conformer_gqa_moe_diffusion_step'


def layer_norm(x, g, b):
    xf = x.astype(jnp.float32)
    mu = jnp.mean(xf, axis=-1, keepdims=True)
    var = jnp.mean(jnp.square(xf - mu), axis=-1, keepdims=True)
    y = (xf - mu) * lax.rsqrt(var + LN_EPS) * g.astype(jnp.float32) + b.astype(jnp.float32)
    return y.astype(x.dtype)


def rms_norm(x, g):
    xf = x.astype(jnp.float32)
    y = xf * lax.rsqrt(jnp.mean(xf * xf, axis=-1, keepdims=True) + RMS_EPS) * g.astype(jnp.float32)
    return y.astype(x.dtype)


def axial_rope_tables(n_tokens, dtype):
    t = jnp.arange(n_tokens, dtype=jnp.int32)
    pos = jnp.stack([t // GRID_W, t % GRID_W], axis=0).astype(jnp.float32)
    axis_dim = GQ_HEAD_DIM // 2
    inv_freq = ROPE_THETA ** (-jnp.arange(0, axis_dim, 2, dtype=jnp.float32) / axis_dim)
    ang = pos[:, :, None, None] * inv_freq
    return jnp.cos(ang).astype(dtype), jnp.sin(ang).astype(dtype)


def _rotate(x, cos, sin):
    x1, x2 = jnp.split(x, 2, axis=-1)
    return jnp.concatenate([x1 * cos - x2 * sin, x2 * cos + x1 * sin], axis=-1)


def apply_axial_rope(x, cos, sin):
    xr, xc = jnp.split(x, 2, axis=-1)
    return jnp.concatenate([_rotate(xr, cos[0], sin[0]), _rotate(xc, cos[1], sin[1])], axis=-1)


def modulation(cond, w_mod, b_mod):
    m = jax.nn.silu(cond) @ w_mod + b_mod
    return jnp.split(m[..., None, :], 6, axis=-1)


def blocked_attention(q, k, v):
    b, t, h, d = q.shape
    hk = k.shape[2]
    nb = t // Q_BLOCK
    qb = q.reshape(b, nb, Q_BLOCK, hk, h // hk, d).transpose(1, 0, 2, 3, 4, 5)

    def one_block(qblk):
        s = jnp.einsum('bqkgd,bskd->bkgqs', qblk, k).astype(jnp.float32) * (d ** -0.5)
        p = jax.nn.softmax(s, axis=-1).astype(v.dtype)
        return jnp.einsum('bkgqs,bskd->bqkgd', p, v)

    o = lax.map(one_block, qb)
    return o.transpose(1, 0, 2, 3, 4, 5).reshape(b, t, h * d)


def neighbourhood_attention(q, k, v, k_ctx, v_ctx, rpb):
    b, t, h, d = q.shape
    rows = t // GRID_W
    kh = min(NA_KH, rows)
    r = jnp.arange(rows, dtype=jnp.int32)
    row_start = jnp.clip(r - kh // 2, 0, rows - kh)
    key_rows = row_start[:, None] + jnp.arange(kh, dtype=jnp.int32)[None, :]
    col = jnp.arange(GRID_W, dtype=jnp.int32)
    col_start = jnp.clip(col - NA_KW // 2, 0, GRID_W - NA_KW)
    col_mask = (col[None, :] >= col_start[:, None]) & (col[None, :] < col_start[:, None] + NA_KW)
    row_off = key_rows - r[:, None]
    col_off = jnp.clip(col[None, :] - col[:, None], -(NA_KW - 1), NA_KW - 1)
    bias = rpb[:, (row_off + NA_KH - 1)[:, None, :, None], (col_off + NA_KW - 1)[None, :, None, :]]

    qg = q.reshape(b, rows, GRID_W, h, d)
    kg = k.reshape(b, rows, GRID_W, h, d)[:, key_rows]
    vg = v.reshape(b, rows, GRID_W, h, d)[:, key_rows]
    scale = d ** -0.5
    s_lat = jnp.einsum('brqhd,brjkhd->bhrqjk', qg, kg).astype(jnp.float32) * scale + bias[None].astype(jnp.float32)
    s_lat = jnp.where(col_mask[:, None, :], s_lat, NEG_INF).reshape(b, h, rows, GRID_W, kh * GRID_W)
    s_ctx = jnp.einsum('brqhd,blhd->bhrql', qg, k_ctx).astype(jnp.float32) * scale
    p = jax.nn.softmax(jnp.concatenate([s_lat, s_ctx], axis=-1), axis=-1).astype(v.dtype)
    p_lat = p[..., :kh * GRID_W].reshape(b, h, rows, GRID_W, kh, GRID_W)
    p_ctx = p[..., kh * GRID_W:]
    o = jnp.einsum('bhrqjk,brjkhd->brqhd', p_lat, vg) + jnp.einsum('bhrql,blhd->brqhd', p_ctx, v_ctx)
    return o.reshape(b, t, h * d)


def conformer_conv(u, conv_w, conv_b, ln_g, ln_b):
    a, g = jnp.split(u, 2, axis=-1)
    hgl = a * jax.nn.sigmoid(g)
    hc = lax.conv_general_dilated(hgl, conv_w[:, None, :], window_strides=(1,),
                                  padding=[(CONV_K // 2, CONV_K // 2)],
                                  dimension_numbers=('NWC', 'WIO', 'NWC'),
                                  feature_group_count=CONV_CH) + conv_b
    return jax.nn.silu(layer_norm(hc, ln_g, ln_b))


def grouped_moe(h, w_router, b_router, w_gate_up, w_down):
    b, t, d = h.shape
    x = h.reshape(b * t, d)
    logits = (x @ w_router).astype(jnp.float32) + b_router.astype(jnp.float32)
    probs = jax.nn.softmax(logits, axis=-1)
    group_score = jnp.max(probs.reshape(-1, N_EXPERT_GROUPS, EXPERTS_PER_GROUP), axis=-1)
    sel_group = jnp.argmax(group_score, axis=-1).astype(jnp.int32)
    expert_group = jnp.arange(N_EXPERTS, dtype=jnp.int32) // EXPERTS_PER_GROUP
    in_group = expert_group[None, :] == sel_group[:, None]
    top_w, top_i = lax.top_k(jnp.where(in_group, probs, -1.0), TOP_K)
    top_w = top_w / jnp.sum(top_w, axis=-1, keepdims=True)
    gate = jnp.sum(jax.nn.one_hot(top_i, N_EXPERTS, dtype=jnp.float32) * top_w[..., None], axis=1)
    gu = jnp.einsum('nd,edf->nef', x, w_gate_up)
    a, u = jnp.split(gu, 2, axis=-1)
    hid = jax.nn.silu(a) * u * gate[..., None].astype(x.dtype)
    return jnp.einsum('nef,efd->nd', hid, w_down).reshape(b, t, d)


def trunk_layer(x, cond_mod, ctx_cache, rope, w_in, rpb, conv_w, conv_b, conv_ln_g, conv_ln_b,
                q_norm_g, k_norm_g, w_br_a, w_br_b, w_br_c, w_out, ln1_g, ln1_b, ln2_g, ln2_b,
                w_router, b_router, w_gate_up, w_down):
    sh1, sc1, g1, sh2, sc2, g2 = cond_mod
    h = x * (1 + sc1) + sh1
    b, t, _ = h.shape
    offsets = [int(o) for o in np.cumsum(IN_SIZES)[:-1]]
    qa, ka, va, u_conv, qc, kc, vc, gate_logits = jnp.split(h @ w_in, offsets, axis=-1)
    qa = qa.reshape(b, t, NA_HEADS, NA_HEAD_DIM)
    ka = ka.reshape(b, t, NA_HEADS, NA_HEAD_DIM)
    va = va.reshape(b, t, NA_HEADS, NA_HEAD_DIM)
    qc = rms_norm(qc.reshape(b, t, GQ_HEADS, GQ_HEAD_DIM), q_norm_g)
    kc = rms_norm(kc.reshape(b, t, GQ_KV_HEADS, GQ_HEAD_DIM), k_norm_g)
    vc = vc.reshape(b, t, GQ_KV_HEADS, GQ_HEAD_DIM)
    if ctx_cache is None:
        out_a = blocked_attention(qa, ka, va)
        out_c = blocked_attention(qc, kc, vc)
        ctx_tensors = (ka, va, kc, vc)
    else:
        ka_ctx, va_ctx, kc_ctx, vc_ctx = ctx_cache
        out_a = neighbourhood_attention(qa, ka, va, ka_ctx, va_ctx, rpb)
        cos, sin = rope
        qc = apply_axial_rope(qc, cos, sin)
        kc = apply_axial_rope(kc, cos, sin)
        out_c = blocked_attention(qc, jnp.concatenate([kc_ctx, kc], axis=1), jnp.concatenate([vc_ctx, vc], axis=1))
        ctx_tensors = None
    out_b = conformer_conv(u_conv, conv_w, conv_b, conv_ln_g, conv_ln_b)
    g_a, g_b, g_c = jnp.split(jax.nn.sigmoid(gate_logits), N_BRANCH, axis=-1)
    merged = g_a * (out_a @ w_br_a) + g_b * (out_b @ w_br_b) + g_c * (out_c @ w_br_c)
    x = layer_norm(ALPHA * x + g1 * (merged @ w_out), ln1_g, ln1_b)
    h = x * (1 + sc2) + sh2
    x = layer_norm(ALPHA * x + g2 * grouped_moe(h, w_router, b_router, w_gate_up, w_down), ln2_g, ln2_b)
    return x, ctx_tensors


def setup_inputs(seed: int = 0) -> dict:
    key = jax.random.key(seed)
    ks = jax.random.split(key, 32)
    f32 = jnp.float32
    D = D_MODEL

    def nrm(k, shape, scale):
        return jax.random.normal(k, shape, f32) * scale

    return {
        'x_prompt': nrm(ks[0], (BATCH, SEQ, D), 1.0),
        'x_sample': nrm(ks[1], (DEC_BATCH, DEC_SEQ, D), 1.0),
        'cache_nat_k': nrm(ks[2], (DEC_BATCH, DEPTH, PAST_LEN, NA_HEADS, NA_HEAD_DIM), 1.0),
        'cache_nat_v': nrm(ks[3], (DEC_BATCH, DEPTH, PAST_LEN, NA_HEADS, NA_HEAD_DIM), 1.0),
        'cache_gqa_k': nrm(ks[4], (DEC_BATCH, DEPTH, PAST_LEN, GQ_KV_HEADS, GQ_HEAD_DIM), 1.0),
        'cache_gqa_v': nrm(ks[5], (DEC_BATCH, DEPTH, PAST_LEN, GQ_KV_HEADS, GQ_HEAD_DIM), 1.0),
        'c': nrm(ks[6], (DEC_BATCH, D), 1.0),
        'c_ctx': nrm(ks[7], (D,), 1.0),
        'w_mod': nrm(ks[8], (DEPTH, D, 6 * D), 0.5 * D ** -0.5),
        'b_mod': nrm(ks[9], (DEPTH, 6 * D), 0.02),
        'w_in': nrm(ks[10], (DEPTH, D, IN_COLS), D ** -0.5),
        'nat_rpb': nrm(ks[11], (DEPTH, NA_HEADS, 2 * NA_KH - 1, 2 * NA_KW - 1), 0.1),
        'conv_w': nrm(ks[12], (DEPTH, CONV_K, CONV_CH), CONV_K ** -0.5),
        'conv_b': nrm(ks[13], (DEPTH, CONV_CH), 0.02),
        'conv_ln_g': 1.0 + nrm(ks[14], (DEPTH, CONV_CH), 0.05),
        'conv_ln_b': nrm(ks[15], (DEPTH, CONV_CH), 0.02),
        'q_norm_g': 1.0 + nrm(ks[16], (DEPTH, GQ_HEAD_DIM), 0.05),
        'k_norm_g': 1.0 + nrm(ks[17], (DEPTH, GQ_HEAD_DIM), 0.05),
        'w_br_a': nrm(ks[18], (DEPTH, NA_WIDTH, D), NA_WIDTH ** -0.5),
        'w_br_b': nrm(ks[19], (DEPTH, CONV_CH, D), CONV_CH ** -0.5),
        'w_br_c': nrm(ks[20], (DEPTH, GQ_Q_WIDTH, D), GQ_Q_WIDTH ** -0.5),
        'w_out': nrm(ks[21], (DEPTH, D, D), BETA * D ** -0.5),
        'ln1_g': 1.0 + nrm(ks[22], (DEPTH, D), 0.05),
        'ln1_b': nrm(ks[23], (DEPTH, D), 0.02),
        'ln2_g': 1.0 + nrm(ks[24], (DEPTH, D), 0.05),
        'ln2_b': nrm(ks[25], (DEPTH, D), 0.02),
        'w_router': nrm(ks[26], (D, N_EXPERTS), D ** -0.5),
        'b_router': nrm(ks[27], (N_EXPERTS,), 0.01),
        'w_gate_up': nrm(ks[28], (DEPTH, N_EXPERTS, D, 2 * D_EXPERT), D ** -0.5),
        'w_down': nrm(ks[29], (DEPTH, N_EXPERTS, D_EXPERT, D), BETA * D_EXPERT ** -0.5),
    }


def reference(x_prompt, x_sample, cache_nat_k, cache_nat_v, cache_gqa_k, cache_gqa_v, c, c_ctx,
              w_mod, b_mod, w_in, nat_rpb, conv_w, conv_b, conv_ln_g, conv_ln_b, q_norm_g, k_norm_g,
              w_br_a, w_br_b, w_br_c, w_out, ln1_g, ln1_b, ln2_g, ln2_b, w_router, b_router,
              w_gate_up, w_down):
    def layer_weights(l):
        return (w_in[l], nat_rpb[l], conv_w[l], conv_b[l], conv_ln_g[l], conv_ln_b[l], q_norm_g[l], k_norm_g[l],
                w_br_a[l], w_br_b[l], w_br_c[l], w_out[l], ln1_g[l], ln1_b[l], ln2_g[l], ln2_b[l],
                w_router, b_router, w_gate_up[l], w_down[l])

    xp = x_prompt
    nat_k, nat_v, gqa_k, gqa_v = [], [], [], []
    for l in range(DEPTH):
        mod = modulation(c_ctx, w_mod[l], b_mod[l])
        xp, (ka, va, kc, vc) = trunk_layer(xp, mod, None, None, *layer_weights(l))
        nat_k.append(ka)
        nat_v.append(va)
        gqa_k.append(kc)
        gqa_v.append(vc)
    new_nat_k = jnp.stack(nat_k, axis=1)
    new_nat_v = jnp.stack(nat_v, axis=1)
    new_gqa_k = jnp.stack(gqa_k, axis=1)
    new_gqa_v = jnp.stack(gqa_v, axis=1)

    rope = axial_rope_tables(x_sample.shape[1], x_sample.dtype)
    xs = x_sample
    for l in range(DEPTH):
        mod = modulation(c, w_mod[l], b_mod[l])
        ctx = (cache_nat_k[:, l], cache_nat_v[:, l], cache_gqa_k[:, l], cache_gqa_v[:, l])
        xs, _ = trunk_layer(xs, mod, ctx, rope, *layer_weights(l))

    return (xp, xs, new_nat_k, new_nat_v, new_gqa_k, new_gqa_v)
```

```python
import functools

import jax
import jax.numpy as jnp
from jax import lax
from jax.experimental import pallas as pl
from jax.experimental.pallas import tpu as pltpu

F32 = jnp.float32
BF16 = jnp.bfloat16

D_MODEL = 1024
DEPTH = 2
GRID_W = 64
HEAD_DIM = 64
NA_HEADS = 8
NA_WIDTH = NA_HEADS * HEAD_DIM
NA_KH = 8
NA_KW = 16
CONV_CH = 512
CONV_K = 31
GQ_HEADS = 8
GQ_KV_HEADS = 2
GQ_Q_WIDTH = GQ_HEADS * HEAD_DIM
GQ_KV_WIDTH = GQ_KV_HEADS * HEAD_DIM
ROPE_THETA = 10000.0
N_EXPERTS = 16
EXPERTS_PER_GROUP = 4
D_EXPERT = 512
ALPHA = (2 * DEPTH) ** 0.25
LN_EPS = 1e-6
RMS_EPS = 1e-6
NEG_INF = -1e30
QK_SCALE = HEAD_DIM ** -0.5

_OFF = (0, 512, 1024, 1536, 2560, 3072, 3200, 3328, 6400)
IN_COLS = _OFF[-1]

LANES = 128
V7X_VMEM_BYTES = 64 * 2 ** 20
HALO = 16

TM_PROJ = 256
TM_POST = 256
TM_MOE = 1024
TL_CONV = 256
TQ_GQA = 256


def _cparams(sem, vmem_mib):
    assert vmem_mib * 2 ** 20 < V7X_VMEM_BYTES
    return pltpu.CompilerParams(dimension_semantics=sem, vmem_limit_bytes=vmem_mib * 2 ** 20)


def _resident(shape):
    nd = len(shape)
    return pl.BlockSpec(shape, lambda *_: (0,) * nd, pipeline_mode=pl.Buffered(1))


def _dot(a, b):
    return jnp.dot(a, b, preferred_element_type=F32)


def _dot_nt(a, b):
    return lax.dot_general(a, b, (((1,), (1,)), ((), ())), preferred_element_type=F32)


def _split_bf16(a):
    hi = a.astype(BF16)
    lo = (a - hi.astype(F32)).astype(BF16)
    return hi, lo


def _dot3(a, b):
    a_hi, a_lo = _split_bf16(a)
    b_hi, b_lo = _split_bf16(b)
    return _dot(a_hi, b_hi) + _dot(a_lo, b_hi) + _dot(a_hi, b_lo)


def _sigmoid(x):
    return 1.0 / (1.0 + jnp.exp(-x))


def _layer_norm(x, g, b):
    mu = jnp.mean(x, axis=-1, keepdims=True)
    xc = x - mu
    var = jnp.mean(xc * xc, axis=-1, keepdims=True)
    return xc * lax.rsqrt(var + LN_EPS) * g + b


def _mod_kernel(cond_ref, w_ref, b_ref, o_ref):
    c = cond_ref[...]
    o_ref[...] = _dot3(c * _sigmoid(c), w_ref[...]) + b_ref[...]


def _modulation(cond8, w_mod, b_mod):
    tn = 1536
    n = w_mod.shape[-1]
    return pl.pallas_call(
        _mod_kernel,
        grid=(DEPTH, n // tn),
        in_specs=[
            pl.BlockSpec((8, D_MODEL), lambda l, j: (0, 0)),
            pl.BlockSpec((None, D_MODEL, tn), lambda l, j: (l, 0, j)),
            pl.BlockSpec((None, 1, tn), lambda l, j: (l, 0, j)),
        ],
        out_specs=pl.BlockSpec((None, 8, tn), lambda l, j: (l, 0, j)),
        out_shape=jax.ShapeDtypeStruct((DEPTH, 8, n), F32),
        compiler_params=_cparams(("arbitrary", "arbitrary"), 40),
        name="modulation",
    )(cond8, w_mod, b_mod.reshape(DEPTH, 1, n))


def _head_rms(x, gmat, gain):
    hi, lo = _split_bf16(x * x)
    ms = _dot(hi, gmat) + _dot(lo, gmat)
    return x * lax.rsqrt(ms + RMS_EPS) * gain


def _rope(x, c, s):
    w = x.shape[1]
    reps = w // LANES
    if reps > 1:
        c = jnp.concatenate([c] * reps, axis=1)
        s = jnp.concatenate([s] * reps, axis=1)
    lane = lax.broadcasted_iota(jnp.int32, x.shape, 1)
    partner = jnp.where((lane % 32) < 16, pltpu.roll(x, w - 16, 1), pltpu.roll(x, 16, 1))
    return x * c + partner * s


def _proj_kernel(latent, *refs):
    if latent:
        (x_ref, sc_ref, sh_ref, w_ref, gm_ref, qn_ref, kn_ref, c_ref, s_ref,
         qa_ref, ka_ref, va_ref, u_ref, qc_ref, kc_ref, vc_ref, gt_ref) = refs
    else:
        (x_ref, sc_ref, sh_ref, w_ref, gm_ref, qn_ref, kn_ref,
         qa_ref, ka_ref, va_ref, u_ref, qc_ref, kc_ref, vc_ref, gt_ref) = refs
    h = (x_ref[...] * sc_ref[...] + sh_ref[...]).astype(BF16)

    def mm(i):
        return _dot(h, w_ref[:, _OFF[i]:_OFF[i + 1]])

    qa_ref[...] = (mm(0) * QK_SCALE).astype(qa_ref.dtype)
    ka_ref[...] = mm(1).astype(ka_ref.dtype)
    va_ref[...] = mm(2).astype(va_ref.dtype)
    u_ref[...] = mm(3).astype(u_ref.dtype)
    qc = _head_rms(mm(4), gm_ref[...], qn_ref[...])
    kc = _head_rms(mm(5), gm_ref[:GQ_KV_WIDTH, :GQ_KV_WIDTH], kn_ref[...])
    if latent:
        qc = _rope(qc, c_ref[...], s_ref[...])
        kc = _rope(kc, c_ref[...], s_ref[...])
    qc_ref[...] = (qc * QK_SCALE).astype(qc_ref.dtype)
    kc_ref[...] = kc.astype(kc_ref.dtype)
    vc_ref[...] = mm(6).astype(vc_ref.dtype)
    gt_ref[...] = mm(7).astype(gt_ref.dtype)


def _mod_row_map(latent, tiles_per_seq):
    if latent:
        return lambda i: (1 + i // tiles_per_seq, 0, 0)
    return lambda i: (0, 0, 0)


def _proj(x, scale, shift, w_bf, gmat, qn, kn, rope, *, latent, seq_len):
    t = x.shape[0]
    tm = TM_PROJ
    kv_dt = BF16 if latent else F32
    row = lambda i: (i, 0)
    mod_map = _mod_row_map(latent, seq_len // tm)
    in_specs = [
        pl.BlockSpec((tm, D_MODEL), row),
        pl.BlockSpec((None, 1, D_MODEL), mod_map),
        pl.BlockSpec((None, 1, D_MODEL), mod_map),
        _resident((D_MODEL, IN_COLS)),
        _resident((GQ_Q_WIDTH, GQ_Q_WIDTH)),
        _resident((1, GQ_Q_WIDTH)),
        _resident((1, GQ_KV_WIDTH)),
    ]
    args = [x, scale, shift, w_bf, gmat, qn, kn]
    if latent:
        per = seq_len // tm
        in_specs += [pl.BlockSpec((tm, LANES), lambda i: (i % per, 0))] * 2
        args += list(rope)
    widths = (NA_WIDTH, NA_WIDTH, NA_WIDTH, 2 * CONV_CH, GQ_Q_WIDTH, GQ_KV_WIDTH, GQ_KV_WIDTH, 3 * D_MODEL)
    dtypes = (BF16, kv_dt, kv_dt, BF16, BF16, kv_dt, kv_dt, BF16)
    return pl.pallas_call(
        functools.partial(_proj_kernel, latent),
        grid=(t // tm,),
        in_specs=in_specs,
        out_specs=[pl.BlockSpec((tm, w), row) for w in widths],
        out_shape=[jax.ShapeDtypeStruct((t, w), dt) for w, dt in zip(widths, dtypes)],
        compiler_params=_cparams(("arbitrary",), 48),
        name="proj_lat" if latent else "proj_ctx",
    )(*args)


def _softmax_pv(pieces):
    m = None
    for s, _ in pieces:
        mi = jnp.max(s, axis=-1, keepdims=True)
        m = mi if m is None else jnp.maximum(m, mi)
    l = 0.0
    o = 0.0
    for s, v in pieces:
        p = jnp.exp(s - m)
        l = l + jnp.sum(p, axis=-1, keepdims=True)
        o = o + _dot(p.astype(BF16), v)
    return o / l


def _ctx_attn_kernel(qa_ref, ka_ref, va_ref, qc_ref, kc_ref, vc_ref, oa_ref, oc_ref):
    for h in range(NA_HEADS):
        sl = slice(h * HEAD_DIM, (h + 1) * HEAD_DIM)
        k = ka_ref[:, sl].astype(BF16)
        v = va_ref[:, sl].astype(BF16)
        oa_ref[:, sl] = _softmax_pv([(_dot_nt(qa_ref[:, sl], k), v)]).astype(oa_ref.dtype)
    for h in range(GQ_HEADS):
        sl = slice(h * HEAD_DIM, (h + 1) * HEAD_DIM)
        g = h // (GQ_HEADS // GQ_KV_HEADS)
        gsl = slice(g * HEAD_DIM, (g + 1) * HEAD_DIM)
        k = kc_ref[:, gsl].astype(BF16)
        v = vc_ref[:, gsl].astype(BF16)
        oc_ref[:, sl] = _softmax_pv([(_dot_nt(qc_ref[:, sl], k), v)]).astype(oc_ref.dtype)


def _ctx_attention(qa, ka, va, qc, kc, vc, *, seq_len):
    t = qa.shape[0]
    row = lambda i: (i, 0)
    spec = lambda w: pl.BlockSpec((seq_len, w), row)
    return pl.pallas_call(
        _ctx_attn_kernel,
        grid=(t // seq_len,),
        in_specs=[spec(NA_WIDTH), spec(NA_WIDTH), spec(NA_WIDTH), spec(GQ_Q_WIDTH), spec(GQ_KV_WIDTH), spec(GQ_KV_WIDTH)],
        out_specs=[spec(NA_WIDTH), spec(GQ_Q_WIDTH)],
        out_shape=[jax.ShapeDtypeStruct((t, NA_WIDTH), BF16), jax.ShapeDtypeStruct((t, GQ_Q_WIDTH), BF16)],
        compiler_params=_cparams(("arbitrary",), 32),
        name="attn_ctx",
    )(qa, ka, va, qc, kc, vc)


def _na_lat_kernel(rows, q_ref, k_ref, v_ref, kc_ref, vc_ref, bias_ref, o_ref):
    r = pl.program_id(1)
    row_start = jnp.clip(r - NA_KH // 2, 0, rows - NA_KH)
    shift = r - row_start
    base = pl.multiple_of(row_start * GRID_W, GRID_W)
    nk = NA_KH * GRID_W
    kblk = k_ref[pl.ds(base, nk), :]
    vblk = v_ref[pl.ds(base, nk), :]
    qcol = lax.broadcasted_iota(jnp.int32, (GRID_W, nk), 0)
    kcol = lax.broadcasted_iota(jnp.int32, (GRID_W, nk), 1) % GRID_W
    col_start = jnp.clip(qcol - NA_KW // 2, 0, GRID_W - NA_KW)
    valid = (kcol >= col_start) & (kcol < col_start + NA_KW)
    for h in range(NA_HEADS):
        sl = slice(h * HEAD_DIM, (h + 1) * HEAD_DIM)
        q = q_ref[:, sl]
        s_lat = jnp.where(valid, _dot_nt(q, kblk[:, sl]) + bias_ref[shift, h], NEG_INF)
        s_ctx = _dot_nt(q, kc_ref[:, sl])
        o_ref[:, sl] = _softmax_pv([(s_lat, vblk[:, sl]), (s_ctx, vc_ref[:, sl])]).astype(o_ref.dtype)


def _na_latent(q, k, v, k_ctx, v_ctx, bias, *, n_seq, seq_len):
    rows = seq_len // GRID_W
    nb = bias.shape
    return pl.pallas_call(
        functools.partial(_na_lat_kernel, rows),
        grid=(n_seq, rows),
        in_specs=[
            pl.BlockSpec((GRID_W, NA_WIDTH), lambda b, r: (b * rows + r, 0)),
            pl.BlockSpec((seq_len, NA_WIDTH), lambda b, r: (b, 0)),
            pl.BlockSpec((seq_len, NA_WIDTH), lambda b, r: (b, 0)),
            pl.BlockSpec((None,) + k_ctx.shape[1:], lambda b, r: (b, 0, 0)),
            pl.BlockSpec((None,) + v_ctx.shape[1:], lambda b, r: (b, 0, 0)),
            _resident(nb),
        ],
        out_specs=pl.BlockSpec((GRID_W, NA_WIDTH), lambda b, r: (b * rows + r, 0)),
        out_shape=jax.ShapeDtypeStruct(q.shape, BF16),
        compiler_params=_cparams(("arbitrary", "arbitrary"), 48),
        name="attn_na_lat",
    )(q, k, v, k_ctx, v_ctx, bias)


def _gqa_lat_kernel(q_ref, k_ref, v_ref, o_ref):
    for h in range(GQ_HEADS):
        sl = slice(h * HEAD_DIM, (h + 1) * HEAD_DIM)
        g = h // (GQ_HEADS // GQ_KV_HEADS)
        gsl = slice(g * HEAD_DIM, (g + 1) * HEAD_DIM)
        s = _dot_nt(q_ref[:, sl], k_ref[:, gsl])
        o_ref[:, sl] = _softmax_pv([(s, v_ref[:, gsl])]).astype(o_ref.dtype)


def _gqa_latent(q, k_all, v_all, *, n_seq, seq_len):
    per = seq_len // TQ_GQA
    s_all = k_all.shape[1]
    return pl.pallas_call(
        _gqa_lat_kernel,
        grid=(n_seq, per),
        in_specs=[
            pl.BlockSpec((TQ_GQA, GQ_Q_WIDTH), lambda b, t: (b * per + t, 0)),
            pl.BlockSpec((None, s_all, GQ_KV_WIDTH), lambda b, t: (b, 0, 0)),
            pl.BlockSpec((None, s_all, GQ_KV_WIDTH), lambda b, t: (b, 0, 0)),
        ],
        out_specs=pl.BlockSpec((TQ_GQA, GQ_Q_WIDTH), lambda b, t: (b * per + t, 0)),
        out_shape=jax.ShapeDtypeStruct(q.shape, BF16),
        compiler_params=_cparams(("arbitrary", "arbitrary"), 48),
        name="attn_gqa_lat",
    )(q, k_all, v_all)


def _conv_kernel(u_ref, up_ref, un_ref, w_ref, b_ref, g_ref, beta_ref, o_ref, slab_ref):
    t = pl.program_id(1)
    nt = pl.num_programs(1)
    tl = u_ref.shape[0]

    def glu(v):
        return v[:, :CONV_CH].astype(F32) * _sigmoid(v[:, CONV_CH:].astype(F32))

    slab_ref[0:HALO, :] = jnp.where(t > 0, glu(up_ref[...]), 0.0)
    slab_ref[HALO:HALO + tl, :] = glu(u_ref[...])
    slab_ref[HALO + tl:, :] = jnp.where(t < nt - 1, glu(un_ref[...]), 0.0)
    acc = jnp.zeros((tl, CONV_CH), F32)
    for k in range(CONV_K):
        start = HALO - CONV_K // 2 + k
        acc = acc + slab_ref[start:start + tl, :] * w_ref[k:k + 1, :]
    y = _layer_norm(acc + b_ref[...], g_ref[...], beta_ref[...])
    o_ref[...] = (y * _sigmoid(y)).astype(o_ref.dtype)


def _conformer_conv(u, conv_w, conv_b, ln_g, ln_b, *, n_seq, seq_len):
    tl = TL_CONV
    nt = seq_len // tl
    hb = tl // HALO
    last_hb = seq_len // HALO - 1
    u3 = u.reshape(n_seq, seq_len, 2 * CONV_CH)
    vec = lambda a: a.reshape(1, CONV_CH)
    out = pl.pallas_call(
        _conv_kernel,
        grid=(n_seq, nt),
        in_specs=[
            pl.BlockSpec((None, tl, 2 * CONV_CH), lambda s, t: (s, t, 0)),
            pl.BlockSpec((None, HALO, 2 * CONV_CH), lambda s, t: (s, jnp.maximum(t * hb - 1, 0), 0)),
            pl.BlockSpec((None, HALO, 2 * CONV_CH), lambda s, t: (s, jnp.minimum((t + 1) * hb, last_hb), 0)),
            _resident((CONV_K, CONV_CH)),
            _resident((1, CONV_CH)),
            _resident((1, CONV_CH)),
            _resident((1, CONV_CH)),
        ],
        out_specs=pl.BlockSpec((None, tl, CONV_CH), lambda s, t: (s, t, 0)),
        out_shape=jax.ShapeDtypeStruct((n_seq, seq_len, CONV_CH), BF16),
        scratch_shapes=[pltpu.VMEM((tl + 2 * HALO, CONV_CH), F32)],
        compiler_params=_cparams(("arbitrary", "arbitrary"), 32),
        name="conformer_conv",
    )(u3, u3, u3, conv_w, vec(conv_b), vec(ln_g), vec(ln_b))
    return out.reshape(n_seq * seq_len, CONV_CH)


def _router_gates(logits):
    lane = lax.broadcasted_iota(jnp.int32, logits.shape, 1)
    m = jnp.max(logits, axis=-1, keepdims=True)
    e = jnp.exp(logits - m)
    probs = e / jnp.sum(e, axis=-1, keepdims=True)
    p1 = jnp.max(probs, axis=-1, keepdims=True)
    i1 = jnp.min(jnp.where(probs == p1, lane, LANES), axis=-1, keepdims=True)
    in_group = (lane // EXPERTS_PER_GROUP) == (i1 // EXPERTS_PER_GROUP)
    cand = jnp.where(in_group, jnp.where(lane == i1, -1.0, probs), -1.0)
    p2 = jnp.max(cand, axis=-1, keepdims=True)
    i2 = jnp.min(jnp.where(cand == p2, lane, LANES), axis=-1, keepdims=True)
    den = p1 + p2
    return jnp.where(lane == i1, p1 / den, 0.0) + jnp.where(lane == i2, p2 / den, 0.0)


def _post_kernel(oa_ref, ob_ref, oc_ref, gt_ref, x_ref, g1_ref, lg_ref, lb_ref, sc2_ref, sh2_ref,
                 wa_ref, wb_ref, wc_ref, wo_ref, wr_ref, br_ref, x1_ref, h2_ref, gate_ref):
    sg = _sigmoid(gt_ref[...].astype(F32))
    merged = (sg[:, :D_MODEL] * _dot(oa_ref[...], wa_ref[...])
              + sg[:, D_MODEL:2 * D_MODEL] * _dot(ob_ref[...], wb_ref[...])
              + sg[:, 2 * D_MODEL:] * _dot(oc_ref[...], wc_ref[...]))
    y = _dot(merged.astype(BF16), wo_ref[...])
    x1 = _layer_norm(ALPHA * x_ref[...] + g1_ref[...] * y, lg_ref[...], lb_ref[...])
    x1_ref[...] = x1
    h2 = x1 * sc2_ref[...] + sh2_ref[...]
    h2_ref[...] = h2.astype(h2_ref.dtype)
    gate_ref[...] = _router_gates(_dot3(h2, wr_ref[...]) + br_ref[...])


def _post(oa, ob, oc, gates, x, g1, ln_g, ln_b, scale2, shift2, wa, wb, wc, wo, wr, br, *, latent, seq_len):
    t = x.shape[0]
    tm = TM_POST
    row = lambda i: (i, 0)
    mod_map = _mod_row_map(latent, seq_len // tm)
    mod = pl.BlockSpec((None, 1, D_MODEL), mod_map)
    return pl.pallas_call(
        _post_kernel,
        grid=(t // tm,),
        in_specs=[
            pl.BlockSpec((tm, NA_WIDTH), row), pl.BlockSpec((tm, CONV_CH), row), pl.BlockSpec((tm, GQ_Q_WIDTH), row),
            pl.BlockSpec((tm, 3 * D_MODEL), row), pl.BlockSpec((tm, D_MODEL), row),
            mod, _resident((1, D_MODEL)), _resident((1, D_MODEL)), mod, mod,
            _resident(wa.shape), _resident(wb.shape), _resident(wc.shape), _resident(wo.shape),
            _resident(wr.shape), _resident(br.shape),
        ],
        out_specs=[pl.BlockSpec((tm, D_MODEL), row), pl.BlockSpec((tm, D_MODEL), row), pl.BlockSpec((tm, LANES), row)],
        out_shape=[jax.ShapeDtypeStruct((t, D_MODEL), F32), jax.ShapeDtypeStruct((t, D_MODEL), BF16),
                   jax.ShapeDtypeStruct((t, LANES), F32)],
        compiler_params=_cparams(("arbitrary",), 40),
        name="post_lat" if latent else "post_ctx",
    )(oa, ob, oc, gates, x, g1, ln_g, ln_b, scale2, shift2, wa, wb, wc, wo, wr, br)


def _moe_kernel(h_ref, gate_ref, x1_ref, g2_ref, lg_ref, lb_ref, wgu_ref, wd_ref, o_ref, acc_ref):
    e = pl.program_id(1)

    @pl.when(e == 0)
    def _():
        acc_ref[...] = jnp.zeros_like(acc_ref)

    gu = _dot(h_ref[...], wgu_ref[...])
    a = gu[:, :D_EXPERT]
    lane = lax.broadcasted_iota(jnp.int32, gate_ref.shape, 1)
    ge = jnp.sum(jnp.where(lane == e, gate_ref[...], 0.0), axis=-1, keepdims=True)
    hid = a * _sigmoid(a) * gu[:, D_EXPERT:] * ge
    acc_ref[...] += _dot(hid.astype(BF16), wd_ref[...])

    @pl.when(e == pl.num_programs(1) - 1)
    def _():
        o_ref[...] = _layer_norm(ALPHA * x1_ref[...] + g2_ref[...] * acc_ref[...], lg_ref[...], lb_ref[...])


def _moe(h2, gate, x1, g2, ln_g, ln_b, wgu, wd, *, latent, seq_len):
    t = h2.shape[0]
    tm = TM_MOE
    row = lambda i, e: (i, 0)
    per = seq_len // tm
    mod_map = (lambda i, e: (1 + i // per, 0, 0)) if latent else (lambda i, e: (0, 0, 0))
    return pl.pallas_call(
        _moe_kernel,
        grid=(t // tm, N_EXPERTS),
        in_specs=[
            pl.BlockSpec((tm, D_MODEL), row), pl.BlockSpec((tm, LANES), row), pl.BlockSpec((tm, D_MODEL), row),
            pl.BlockSpec((None, 1, D_MODEL), mod_map), _resident((1, D_MODEL)), _resident((1, D_MODEL)),
            pl.BlockSpec((None, D_MODEL, 2 * D_EXPERT), lambda i, e: (e, 0, 0)),
            pl.BlockSpec((None, D_EXPERT, D_MODEL), lambda i, e: (e, 0, 0)),
        ],
        out_specs=pl.BlockSpec((tm, D_MODEL), row),
        out_shape=jax.ShapeDtypeStruct((t, D_MODEL), F32),
        scratch_shapes=[pltpu.VMEM((tm, D_MODEL), F32)],
        compiler_params=_cparams(("arbitrary", "arbitrary"), 52),
        name="moe_lat" if latent else "moe_ctx",
    )(h2, gate, x1, g2, ln_g, ln_b, wgu, wd)


def _head_mean_matrix():
    i = jnp.arange(GQ_Q_WIDTH) // HEAD_DIM
    return jnp.where(i[:, None] == i[None, :], 1.0 / HEAD_DIM, 0.0).astype(BF16)


def _rope_tables(n_tokens):
    t = jnp.arange(n_tokens, dtype=jnp.int32)
    axis_dim = HEAD_DIM // 2
    inv_freq = ROPE_THETA ** (-jnp.arange(0, axis_dim, 2, dtype=F32) / axis_dim)
    ar = (t // GRID_W).astype(F32)[:, None] * inv_freq
    ac = (t % GRID_W).astype(F32)[:, None] * inv_freq
    c = jnp.concatenate([jnp.cos(ar), jnp.cos(ar), jnp.cos(ac), jnp.cos(ac)], axis=1)
    s = jnp.concatenate([-jnp.sin(ar), jnp.sin(ar), -jnp.sin(ac), jnp.sin(ac)], axis=1)
    return jnp.tile(c, (1, LANES // HEAD_DIM)), jnp.tile(s, (1, LANES // HEAD_DIM))


def _na_bias_table(rpb):
    col = jnp.arange(GRID_W)
    col_idx = jnp.clip(col[None, :] - col[:, None], -(NA_KW - 1), NA_KW - 1) + NA_KW - 1
    j = jnp.arange(NA_KH)
    row_idx = j[None, :] - j[:, None] + NA_KH - 1
    t = rpb[:, row_idx[:, :, None, None], col_idx[None, None, :, :]]
    return t.transpose(1, 0, 3, 2, 4).reshape(NA_KH, NA_HEADS, GRID_W, NA_KH * GRID_W)


def _stream_layer(x, mods, l, wts, *, latent, n_seq, seq_len, cache=None, rope=None):
    sh1, sc1, g1, sh2, sc2, g2 = mods
    qa, ka, va, u, qc, kc, vc, gates = _proj(
        x, 1.0 + sc1, sh1, wts["w_in"][l], wts["gmat"], wts["qn"][l], wts["kn"][l], rope,
        latent=latent, seq_len=seq_len)
    if latent:
        nat_k, nat_v, gqa_k, gqa_v = cache
        out_a = _na_latent(qa, ka, va, nat_k, nat_v, wts["na_bias"][l], n_seq=n_seq, seq_len=seq_len)
        k_all = jnp.concatenate([gqa_k, kc.reshape(n_seq, seq_len, GQ_KV_WIDTH)], axis=1)
        v_all = jnp.concatenate([gqa_v, vc.reshape(n_seq, seq_len, GQ_KV_WIDTH)], axis=1)
        out_c = _gqa_latent(qc, k_all, v_all, n_seq=n_seq, seq_len=seq_len)
    else:
        out_a, out_c = _ctx_attention(qa, ka, va, qc, kc, vc, seq_len=seq_len)
    out_b = _conformer_conv(u, wts["conv_w"][l], wts["conv_b"][l], wts["conv_ln_g"][l], wts["conv_ln_b"][l],
                            n_seq=n_seq, seq_len=seq_len)
    x1, h2, gate = _post(out_a, out_b, out_c, gates, x, g1, wts["ln1_g"][l], wts["ln1_b"][l], 1.0 + sc2, sh2,
                         wts["w_br_a"][l], wts["w_br_b"][l], wts["w_br_c"][l], wts["w_out"][l],
                         wts["w_router"], wts["b_router"], latent=latent, seq_len=seq_len)
    x2 = _moe(h2, gate, x1, g2, wts["ln2_g"][l], wts["ln2_b"][l], wts["w_gate_up"][l], wts["w_down"][l],
              latent=latent, seq_len=seq_len)
    return x2, (ka, va, kc, vc)


def kernel(x_prompt, x_sample, cache_nat_k, cache_nat_v, cache_gqa_k, cache_gqa_v, c, c_ctx, w_mod, b_mod, w_in, nat_rpb, conv_w, conv_b, conv_ln_g, conv_ln_b, q_norm_g, k_norm_g, w_br_a, w_br_b, w_br_c, w_out, ln1_g, ln1_b, ln2_g, ln2_b, w_router, b_router, w_gate_up, w_down):
    batch, seq, _ = x_prompt.shape
    dec_batch, dec_seq, _ = x_sample.shape
    past = cache_nat_k.shape[2]
    assert dec_batch == 2 and seq % TM_PROJ == 0 and dec_seq % TM_MOE == 0

    cond8 = jnp.zeros((8, D_MODEL), F32).at[0].set(c_ctx).at[1:1 + dec_batch].set(c)
    mods_all = _modulation(cond8, w_mod, b_mod)

    pad = LANES - N_EXPERTS
    vecd = lambda a: a.reshape(DEPTH, 1, -1)
    wts = dict(
        w_in=w_in.astype(BF16),
        gmat=_head_mean_matrix(),
        qn=jnp.tile(q_norm_g, (1, GQ_HEADS)).reshape(DEPTH, 1, GQ_Q_WIDTH),
        kn=jnp.tile(k_norm_g, (1, GQ_KV_HEADS)).reshape(DEPTH, 1, GQ_KV_WIDTH),
        na_bias=jnp.stack([_na_bias_table(nat_rpb[l]) for l in range(DEPTH)]),
        conv_w=conv_w, conv_b=conv_b, conv_ln_g=conv_ln_g, conv_ln_b=conv_ln_b,
        ln1_g=vecd(ln1_g), ln1_b=vecd(ln1_b), ln2_g=vecd(ln2_g), ln2_b=vecd(ln2_b),
        w_br_a=w_br_a.astype(BF16), w_br_b=w_br_b.astype(BF16), w_br_c=w_br_c.astype(BF16),
        w_out=w_out.astype(BF16),
        w_router=jnp.pad(w_router, ((0, 0), (0, pad))),
        b_router=jnp.pad(b_router, (0, pad), constant_values=NEG_INF).reshape(1, LANES),
        w_gate_up=w_gate_up.astype(BF16), w_down=w_down.astype(BF16),
    )
    rope = _rope_tables(dec_seq)

    def layer_mods(l):
        m = mods_all[l, :1 + dec_batch].reshape(1 + dec_batch, 6, 1, D_MODEL)
        return [m[:, i] for i in range(6)]

    xp = x_prompt.reshape(batch * seq, D_MODEL)
    xs = x_sample.reshape(dec_batch * dec_seq, D_MODEL)
    new_cache = []
    for l in range(DEPTH):
        mods = layer_mods(l)
        xp, ctx_kv = _stream_layer(xp, mods, l, wts, latent=False, n_seq=batch, seq_len=seq)
        new_cache.append(ctx_kv)
        cache = (cache_nat_k[:, l].reshape(dec_batch, past, NA_WIDTH).astype(BF16),
                 cache_nat_v[:, l].reshape(dec_batch, past, NA_WIDTH).astype(BF16),
                 cache_gqa_k[:, l].reshape(dec_batch, past, GQ_KV_WIDTH).astype(BF16),
                 cache_gqa_v[:, l].reshape(dec_batch, past, GQ_KV_WIDTH).astype(BF16))
        xs, _ = _stream_layer(xs, mods, l, wts, latent=True, n_seq=dec_batch, seq_len=dec_seq, cache=cache, rope=rope)

    def stack(i, heads):
        return jnp.stack([kv[i].reshape(batch, seq, heads, HEAD_DIM) for kv in new_cache], axis=1)

    return (xp.reshape(batch, seq, D_MODEL), xs.reshape(dec_batch, dec_seq, D_MODEL),
            stack(0, NA_HEADS), stack(1, NA_HEADS), stack(2, GQ_KV_HEADS), stack(3, GQ_KV_HEADS))
```

```python
import functools

import numpy as np
import jax
import jax.numpy as jnp
from jax import lax
from jax.experimental import pallas as pl
from jax.experimental.pallas import tpu as pltpu

F32 = jnp.float32
BF16 = jnp.bfloat16

D_MODEL = 1024
DEPTH = 2
GRID_W = 64
HEAD_DIM = 64
NA_HEADS = 8
NA_WIDTH = NA_HEADS * HEAD_DIM
NA_KH = 8
NA_KW = 16
CONV_CH = 512
CONV_K = 31
GQ_HEADS = 8
GQ_KV_HEADS = 2
GQ_Q_WIDTH = GQ_HEADS * HEAD_DIM
GQ_KV_WIDTH = GQ_KV_HEADS * HEAD_DIM
ROPE_THETA = 10000.0
N_EXPERTS = 16
EXPERTS_PER_GROUP = 4
D_EXPERT = 512
ALPHA = (2 * DEPTH) ** 0.25
LN_EPS = 1e-6
RMS_EPS = 1e-6
NEG_INF = -1e30
QK_SCALE = HEAD_DIM ** -0.5

_OFF = (0, 512, 1024, 1536, 2560, 3072, 3200, 3328, 6400)
IN_COLS = _OFF[-1]

LANES = 128
V7X_VMEM_BYTES = 64 * 2 ** 20
HALO = 16

TM_PROJ = 256
TM_POST = 256
TM_MOE = 1024
TL_CONV = 256
TQ_GQA = 256


def _cparams(sem, vmem_mib):
    assert vmem_mib * 2 ** 20 < V7X_VMEM_BYTES
    return pltpu.CompilerParams(dimension_semantics=sem, vmem_limit_bytes=vmem_mib * 2 ** 20)


def _resident(shape):
    nd = len(shape)
    return pl.BlockSpec(shape, lambda *_: (0,) * nd, pipeline_mode=pl.Buffered(1))


def _dot(a, b):
    return jnp.dot(a, b, preferred_element_type=F32)


def _dot_nt(a, b):
    return lax.dot_general(a, b, (((1,), (1,)), ((), ())), preferred_element_type=F32)


def _split_bf16(a):
    hi = a.astype(BF16)
    lo = (a - hi.astype(F32)).astype(BF16)
    return hi, lo


def _dot3(a, b):
    a_hi, a_lo = _split_bf16(a)
    b_hi, b_lo = _split_bf16(b)
    return _dot(a_hi, b_hi) + _dot(a_lo, b_hi) + _dot(a_hi, b_lo)


def _sigmoid(x):
    return 1.0 / (1.0 + jnp.exp(-x))


def _layer_norm(x, g, b):
    mu = jnp.mean(x, axis=-1, keepdims=True)
    xc = x - mu
    var = jnp.mean(xc * xc, axis=-1, keepdims=True)
    return xc * lax.rsqrt(var + LN_EPS) * g + b


def _mod_kernel(cond_ref, w_ref, b_ref, o_ref):
    c = cond_ref[...]
    o_ref[...] = _dot3(c * _sigmoid(c), w_ref[...]) + b_ref[...]


def _modulation(cond8, w_mod, b_mod):
    tn = 1536
    n = w_mod.shape[-1]
    return pl.pallas_call(
        _mod_kernel,
        grid=(DEPTH, n // tn),
        in_specs=[
            pl.BlockSpec((8, D_MODEL), lambda l, j: (0, 0)),
            pl.BlockSpec((None, D_MODEL, tn), lambda l, j: (l, 0, j)),
            pl.BlockSpec((None, 1, tn), lambda l, j: (l, 0, j)),
        ],
        out_specs=pl.BlockSpec((None, 8, tn), lambda l, j: (l, 0, j)),
        out_shape=jax.ShapeDtypeStruct((DEPTH, 8, n), F32),
        compiler_params=_cparams(("arbitrary", "arbitrary"), 40),
        name="modulation",
    )(cond8, w_mod, b_mod.reshape(DEPTH, 1, n))


def _head_rms(x, gmat, gain):
    hi, lo = _split_bf16(x * x)
    ms = _dot(hi, gmat) + _dot(lo, gmat)
    return x * lax.rsqrt(ms + RMS_EPS) * gain


def _rope(x, c, s):
    w = x.shape[1]
    reps = w // LANES
    if reps > 1:
        c = jnp.concatenate([c] * reps, axis=1)
        s = jnp.concatenate([s] * reps, axis=1)
    lane = lax.broadcasted_iota(jnp.int32, x.shape, 1)
    partner = jnp.where((lane % 32) < 16, pltpu.roll(x, w - 16, 1), pltpu.roll(x, 16, 1))
    return x * c + partner * s


def _proj_kernel(latent, *refs):
    if latent:
        (x_ref, sc_ref, sh_ref, w_ref, gm_ref, qn_ref, kn_ref, c_ref, s_ref,
         qa_ref, ka_ref, va_ref, u_ref, qc_ref, kc_ref, vc_ref, gt_ref) = refs
    else:
        (x_ref, sc_ref, sh_ref, w_ref, gm_ref, qn_ref, kn_ref,
         qa_ref, ka_ref, va_ref, u_ref, qc_ref, kc_ref, vc_ref, gt_ref) = refs
    h = (x_ref[...] * sc_ref[...] + sh_ref[...]).astype(BF16)

    def mm(i):
        return _dot(h, w_ref[:, _OFF[i]:_OFF[i + 1]])

    qa_ref[...] = (mm(0) * QK_SCALE).astype(qa_ref.dtype)
    ka_ref[...] = mm(1).astype(ka_ref.dtype)
    va_ref[...] = mm(2).astype(va_ref.dtype)
    u_ref[...] = mm(3).astype(u_ref.dtype)
    qc = _head_rms(mm(4), gm_ref[...], qn_ref[...])
    kc = _head_rms(mm(5), gm_ref[:GQ_KV_WIDTH, :GQ_KV_WIDTH], kn_ref[...])
    if latent:
        qc = _rope(qc, c_ref[...], s_ref[...])
        kc = _rope(kc, c_ref[...], s_ref[...])
    qc_ref[...] = (qc * QK_SCALE).astype(qc_ref.dtype)
    kc_ref[...] = kc.astype(kc_ref.dtype)
    vc_ref[...] = mm(6).astype(vc_ref.dtype)
    gt_ref[...] = mm(7).astype(gt_ref.dtype)


def _mod_row_map(latent, tiles_per_seq):
    if latent:
        return lambda i: (1 + i // tiles_per_seq, 0, 0)
    return lambda i: (0, 0, 0)


def _proj(x, scale, shift, w_bf, gmat, qn, kn, rope, *, latent, seq_len):
    t = x.shape[0]
    tm = TM_PROJ
    kv_dt = BF16 if latent else F32
    row = lambda i: (i, 0)
    mod_map = _mod_row_map(latent, seq_len // tm)
    in_specs = [
        pl.BlockSpec((tm, D_MODEL), row),
        pl.BlockSpec((None, 1, D_MODEL), mod_map),
        pl.BlockSpec((None, 1, D_MODEL), mod_map),
        _resident((D_MODEL, IN_COLS)),
        _resident((GQ_Q_WIDTH, GQ_Q_WIDTH)),
        _resident((1, GQ_Q_WIDTH)),
        _resident((1, GQ_KV_WIDTH)),
    ]
    args = [x, scale, shift, w_bf, gmat, qn, kn]
    if latent:
        per = seq_len // tm
        in_specs += [pl.BlockSpec((tm, LANES), lambda i: (i % per, 0))] * 2
        args += list(rope)
    widths = (NA_WIDTH, NA_WIDTH, NA_WIDTH, 2 * CONV_CH, GQ_Q_WIDTH, GQ_KV_WIDTH, GQ_KV_WIDTH, 3 * D_MODEL)
    dtypes = (BF16, kv_dt, kv_dt, BF16, BF16, kv_dt, kv_dt, BF16)
    return pl.pallas_call(
        functools.partial(_proj_kernel, latent),
        grid=(t // tm,),
        in_specs=in_specs,
        out_specs=[pl.BlockSpec((tm, w), row) for w in widths],
        out_shape=[jax.ShapeDtypeStruct((t, w), dt) for w, dt in zip(widths, dtypes)],
        compiler_params=_cparams(("arbitrary",), 48),
        name="proj_lat" if latent else "proj_ctx",
    )(*args)


def _softmax_pv(pieces):
    m = None
    for s, _ in pieces:
        mi = jnp.max(s, axis=-1, keepdims=True)
        m = mi if m is None else jnp.maximum(m, mi)
    l = 0.0
    o = 0.0
    for s, v in pieces:
        p = jnp.exp(s - m)
        l = l + jnp.sum(p, axis=-1, keepdims=True)
        o = o + _dot(p.astype(BF16), v)
    return o / l


def _ctx_attn_kernel(qa_ref, ka_ref, va_ref, qc_ref, kc_ref, vc_ref, oa_ref, oc_ref):
    for h in range(NA_HEADS):
        sl = slice(h * HEAD_DIM, (h + 1) * HEAD_DIM)
        k = ka_ref[:, sl].astype(BF16)
        v = va_ref[:, sl].astype(BF16)
        oa_ref[:, sl] = _softmax_pv([(_dot_nt(qa_ref[:, sl], k), v)]).astype(oa_ref.dtype)
    for h in range(GQ_HEADS):
        sl = slice(h * HEAD_DIM, (h + 1) * HEAD_DIM)
        g = h // (GQ_HEADS // GQ_KV_HEADS)
        gsl = slice(g * HEAD_DIM, (g + 1) * HEAD_DIM)
        k = kc_ref[:, gsl].astype(BF16)
        v = vc_ref[:, gsl].astype(BF16)
        oc_ref[:, sl] = _softmax_pv([(_dot_nt(qc_ref[:, sl], k), v)]).astype(oc_ref.dtype)


def _ctx_attention(qa, ka, va, qc, kc, vc, *, seq_len):
    t = qa.shape[0]
    row = lambda i: (i, 0)
    spec = lambda w: pl.BlockSpec((seq_len, w), row)
    return pl.pallas_call(
        _ctx_attn_kernel,
        grid=(t // seq_len,),
        in_specs=[spec(NA_WIDTH), spec(NA_WIDTH), spec(NA_WIDTH), spec(GQ_Q_WIDTH), spec(GQ_KV_WIDTH), spec(GQ_KV_WIDTH)],
        out_specs=[spec(NA_WIDTH), spec(GQ_Q_WIDTH)],
        out_shape=[jax.ShapeDtypeStruct((t, NA_WIDTH), BF16), jax.ShapeDtypeStruct((t, GQ_Q_WIDTH), BF16)],
        compiler_params=_cparams(("arbitrary",), 32),
        name="attn_ctx",
    )(qa, ka, va, qc, kc, vc)


def _na_lat_kernel(rows, q_ref, k_ref, v_ref, kc_ref, vc_ref, bias_ref, o_ref):
    r = pl.program_id(1)
    row_start = jnp.clip(r - NA_KH // 2, 0, rows - NA_KH)
    shift = r - row_start
    base = pl.multiple_of(row_start * GRID_W, GRID_W)
    nk = NA_KH * GRID_W
    kblk = k_ref[pl.ds(base, nk), :]
    vblk = v_ref[pl.ds(base, nk), :]
    qcol = lax.broadcasted_iota(jnp.int32, (GRID_W, nk), 0)
    kcol = lax.broadcasted_iota(jnp.int32, (GRID_W, nk), 1) % GRID_W
    col_start = jnp.clip(qcol - NA_KW // 2, 0, GRID_W - NA_KW)
    valid = (kcol >= col_start) & (kcol < col_start + NA_KW)
    for h in range(NA_HEADS):
        sl = slice(h * HEAD_DIM, (h + 1) * HEAD_DIM)
        q = q_ref[:, sl]
        s_lat = jnp.where(valid, _dot_nt(q, kblk[:, sl]) + bias_ref[shift, h], NEG_INF)
        s_ctx = _dot_nt(q, kc_ref[:, sl])
        o_ref[:, sl] = _softmax_pv([(s_lat, vblk[:, sl]), (s_ctx, vc_ref[:, sl])]).astype(o_ref.dtype)


def _na_latent(q, k, v, k_ctx, v_ctx, bias, *, n_seq, seq_len):
    rows = seq_len // GRID_W
    nb = bias.shape
    return pl.pallas_call(
        functools.partial(_na_lat_kernel, rows),
        grid=(n_seq, rows),
        in_specs=[
            pl.BlockSpec((GRID_W, NA_WIDTH), lambda b, r: (b * rows + r, 0)),
            pl.BlockSpec((seq_len, NA_WIDTH), lambda b, r: (b, 0)),
            pl.BlockSpec((seq_len, NA_WIDTH), lambda b, r: (b, 0)),
            pl.BlockSpec((None,) + k_ctx.shape[1:], lambda b, r: (b, 0, 0)),
            pl.BlockSpec((None,) + v_ctx.shape[1:], lambda b, r: (b, 0, 0)),
            _resident(nb),
        ],
        out_specs=pl.BlockSpec((GRID_W, NA_WIDTH), lambda b, r: (b * rows + r, 0)),
        out_shape=jax.ShapeDtypeStruct(q.shape, BF16),
        compiler_params=_cparams(("arbitrary", "arbitrary"), 48),
        name="attn_na_lat",
    )(q, k, v, k_ctx, v_ctx, bias)


def _gqa_lat_kernel(q_ref, k_ref, v_ref, o_ref):
    for h in range(GQ_HEADS):
        sl = slice(h * HEAD_DIM, (h + 1) * HEAD_DIM)
        g = h // (GQ_HEADS // GQ_KV_HEADS)
        gsl = slice(g * HEAD_DIM, (g + 1) * HEAD_DIM)
        s = _dot_nt(q_ref[:, sl], k_ref[:, gsl])
        o_ref[:, sl] = _softmax_pv([(s, v_ref[:, gsl])]).astype(o_ref.dtype)


def _gqa_latent(q, k_all, v_all, *, n_seq, seq_len):
    per = seq_len // TQ_GQA
    s_all = k_all.shape[1]
    return pl.pallas_call(
        _gqa_lat_kernel,
        grid=(n_seq, per),
        in_specs=[
            pl.BlockSpec((TQ_GQA, GQ_Q_WIDTH), lambda b, t: (b * per + t, 0)),
            pl.BlockSpec((None, s_all, GQ_KV_WIDTH), lambda b, t: (b, 0, 0)),
            pl.BlockSpec((None, s_all, GQ_KV_WIDTH), lambda b, t: (b, 0, 0)),
        ],
        out_specs=pl.BlockSpec((TQ_GQA, GQ_Q_WIDTH), lambda b, t: (b * per + t, 0)),
        out_shape=jax.ShapeDtypeStruct(q.shape, BF16),
        compiler_params=_cparams(("arbitrary", "arbitrary"), 48),
        name="attn_gqa_lat",
    )(q, k_all, v_all)


def _conv_kernel(u_ref, up_ref, un_ref, w_ref, b_ref, g_ref, beta_ref, o_ref, slab_ref):
    t = pl.program_id(1)
    nt = pl.num_programs(1)
    tl = u_ref.shape[0]

    def glu(v):
        return v[:, :CONV_CH].astype(F32) * _sigmoid(v[:, CONV_CH:].astype(F32))

    slab_ref[0:HALO, :] = jnp.where(t > 0, glu(up_ref[...]), 0.0)
    slab_ref[HALO:HALO + tl, :] = glu(u_ref[...])
    slab_ref[HALO + tl:, :] = jnp.where(t < nt - 1, glu(un_ref[...]), 0.0)
    acc = jnp.zeros((tl, CONV_CH), F32)
    for k in range(CONV_K):
        start = HALO - CONV_K // 2 + k
        acc = acc + slab_ref[start:start + tl, :] * w_ref[k:k + 1, :]
    y = _layer_norm(acc + b_ref[...], g_ref[...], beta_ref[...])
    o_ref[...] = (y * _sigmoid(y)).astype(o_ref.dtype)


def _conformer_conv(u, conv_w, conv_b, ln_g, ln_b, *, n_seq, seq_len):
    tl = TL_CONV
    nt = seq_len // tl
    hb = tl // HALO
    last_hb = seq_len // HALO - 1
    u3 = u.reshape(n_seq, seq_len, 2 * CONV_CH)
    vec = lambda a: a.reshape(1, CONV_CH)
    out = pl.pallas_call(
        _conv_kernel,
        grid=(n_seq, nt),
        in_specs=[
            pl.BlockSpec((None, tl, 2 * CONV_CH), lambda s, t: (s, t, 0)),
            pl.BlockSpec((None, HALO, 2 * CONV_CH), lambda s, t: (s, jnp.maximum(t * hb - 1, 0), 0)),
            pl.BlockSpec((None, HALO, 2 * CONV_CH), lambda s, t: (s, jnp.minimum((t + 1) * hb, last_hb), 0)),
            _resident((CONV_K, CONV_CH)),
            _resident((1, CONV_CH)),
            _resident((1, CONV_CH)),
            _resident((1, CONV_CH)),
        ],
        out_specs=pl.BlockSpec((None, tl, CONV_CH), lambda s, t: (s, t, 0)),
        out_shape=jax.ShapeDtypeStruct((n_seq, seq_len, CONV_CH), BF16),
        scratch_shapes=[pltpu.VMEM((tl + 2 * HALO, CONV_CH), F32)],
        compiler_params=_cparams(("arbitrary", "arbitrary"), 32),
        name="conformer_conv",
    )(u3, u3, u3, conv_w, vec(conv_b), vec(ln_g), vec(ln_b))
    return out.reshape(n_seq * seq_len, CONV_CH)


def _router_gates(logits):
    lane = lax.broadcasted_iota(jnp.int32, logits.shape, 1)
    m = jnp.max(logits, axis=-1, keepdims=True)
    e = jnp.exp(logits - m)
    probs = e / jnp.sum(e, axis=-1, keepdims=True)
    p1 = jnp.max(probs, axis=-1, keepdims=True)
    i1 = jnp.min(jnp.where(probs == p1, lane, LANES), axis=-1, keepdims=True)
    in_group = (lane // EXPERTS_PER_GROUP) == (i1 // EXPERTS_PER_GROUP)
    cand = jnp.where(in_group, jnp.where(lane == i1, -1.0, probs), -1.0)
    p2 = jnp.max(cand, axis=-1, keepdims=True)
    i2 = jnp.min(jnp.where(cand == p2, lane, LANES), axis=-1, keepdims=True)
    den = p1 + p2
    return jnp.where(lane == i1, p1 / den, 0.0) + jnp.where(lane == i2, p2 / den, 0.0)


def _post_kernel(oa_ref, ob_ref, oc_ref, gt_ref, x_ref, g1_ref, lg_ref, lb_ref, sc2_ref, sh2_ref,
                 wa_ref, wb_ref, wc_ref, wo_ref, wr_ref, br_ref, x1_ref, h2_ref, gate_ref):
    sg = _sigmoid(gt_ref[...].astype(F32))
    merged = (sg[:, :D_MODEL] * _dot(oa_ref[...], wa_ref[...])
              + sg[:, D_MODEL:2 * D_MODEL] * _dot(ob_ref[...], wb_ref[...])
              + sg[:, 2 * D_MODEL:] * _dot(oc_ref[...], wc_ref[...]))
    y = _dot(merged.astype(BF16), wo_ref[...])
    x1 = _layer_norm(ALPHA * x_ref[...] + g1_ref[...] * y, lg_ref[...], lb_ref[...])
    x1_ref[...] = x1
    h2 = x1 * sc2_ref[...] + sh2_ref[...]
    h2_ref[...] = h2.astype(h2_ref.dtype)
    gate_ref[...] = _router_gates(_dot3(h2, wr_ref[...]) + br_ref[...])


def _post(oa, ob, oc, gates, x, g1, ln_g, ln_b, scale2, shift2, wa, wb, wc, wo, wr, br, *, latent, seq_len):
    t = x.shape[0]
    tm = TM_POST
    row = lambda i: (i, 0)
    mod_map = _mod_row_map(latent, seq_len // tm)
    mod = pl.BlockSpec((None, 1, D_MODEL), mod_map)
    return pl.pallas_call(
        _post_kernel,
        grid=(t // tm,),
        in_specs=[
            pl.BlockSpec((tm, NA_WIDTH), row), pl.BlockSpec((tm, CONV_CH), row), pl.BlockSpec((tm, GQ_Q_WIDTH), row),
            pl.BlockSpec((tm, 3 * D_MODEL), row), pl.BlockSpec((tm, D_MODEL), row),
            mod, _resident((1, D_MODEL)), _resident((1, D_MODEL)), mod, mod,
            _resident(wa.shape), _resident(wb.shape), _resident(wc.shape), _resident(wo.shape),
            _resident(wr.shape), _resident(br.shape),
        ],
        out_specs=[pl.BlockSpec((tm, D_MODEL), row), pl.BlockSpec((tm, D_MODEL), row), pl.BlockSpec((tm, LANES), row)],
        out_shape=[jax.ShapeDtypeStruct((t, D_MODEL), F32), jax.ShapeDtypeStruct((t, D_MODEL), BF16),
                   jax.ShapeDtypeStruct((t, LANES), F32)],
        compiler_params=_cparams(("arbitrary",), 40),
        name="post_lat" if latent else "post_ctx",
    )(oa, ob, oc, gates, x, g1, ln_g, ln_b, scale2, shift2, wa, wb, wc, wo, wr, br)


def _moe_kernel(h_ref, gate_ref, x1_ref, g2_ref, lg_ref, lb_ref, wgu_ref, wd_ref, o_ref, acc_ref):
    e = pl.program_id(1)

    @pl.when(e == 0)
    def _():
        acc_ref[...] = jnp.zeros_like(acc_ref)

    gu = _dot(h_ref[...], wgu_ref[...])
    a = gu[:, :D_EXPERT]
    lane = lax.broadcasted_iota(jnp.int32, gate_ref.shape, 1)
    ge = jnp.sum(jnp.where(lane == e, gate_ref[...], 0.0), axis=-1, keepdims=True)
    hid = a * _sigmoid(a) * gu[:, D_EXPERT:] * ge
    acc_ref[...] += _dot(hid.astype(BF16), wd_ref[...])

    @pl.when(e == pl.num_programs(1) - 1)
    def _():
        o_ref[...] = _layer_norm(ALPHA * x1_ref[...] + g2_ref[...] * acc_ref[...], lg_ref[...], lb_ref[...])


def _moe(h2, gate, x1, g2, ln_g, ln_b, wgu, wd, *, latent, seq_len):
    t = h2.shape[0]
    tm = TM_MOE
    row = lambda i, e: (i, 0)
    per = seq_len // tm
    mod_map = (lambda i, e: (1 + i // per, 0, 0)) if latent else (lambda i, e: (0, 0, 0))
    return pl.pallas_call(
        _moe_kernel,
        grid=(t // tm, N_EXPERTS),
        in_specs=[
            pl.BlockSpec((tm, D_MODEL), row), pl.BlockSpec((tm, LANES), row), pl.BlockSpec((tm, D_MODEL), row),
            pl.BlockSpec((None, 1, D_MODEL), mod_map), _resident((1, D_MODEL)), _resident((1, D_MODEL)),
            pl.BlockSpec((None, D_MODEL, 2 * D_EXPERT), lambda i, e: (e, 0, 0)),
            pl.BlockSpec((None, D_EXPERT, D_MODEL), lambda i, e: (e, 0, 0)),
        ],
        out_specs=pl.BlockSpec((tm, D_MODEL), row),
        out_shape=jax.ShapeDtypeStruct((t, D_MODEL), F32),
        scratch_shapes=[pltpu.VMEM((tm, D_MODEL), F32)],
        compiler_params=_cparams(("arbitrary", "arbitrary"), 52),
        name="moe_lat" if latent else "moe_ctx",
    )(h2, gate, x1, g2, ln_g, ln_b, wgu, wd)


def _head_mean_matrix():
    i = jnp.arange(GQ_Q_WIDTH) // HEAD_DIM
    return jnp.where(i[:, None] == i[None, :], 1.0 / HEAD_DIM, 0.0).astype(BF16)


def _rope_tables(n_tokens):
    t = jnp.arange(n_tokens, dtype=jnp.int32)
    axis_dim = HEAD_DIM // 2
    inv_freq = ROPE_THETA ** (-jnp.arange(0, axis_dim, 2, dtype=F32) / axis_dim)
    ar = (t // GRID_W).astype(F32)[:, None] * inv_freq
    ac = (t % GRID_W).astype(F32)[:, None] * inv_freq
    c = jnp.concatenate([jnp.cos(ar), jnp.cos(ar), jnp.cos(ac), jnp.cos(ac)], axis=1)
    s = jnp.concatenate([-jnp.sin(ar), jnp.sin(ar), -jnp.sin(ac), jnp.sin(ac)], axis=1)
    return jnp.tile(c, (1, LANES // HEAD_DIM)), jnp.tile(s, (1, LANES // HEAD_DIM))


def _na_bias_table(rpb):
    col = np.arange(GRID_W)
    col_idx = np.clip(col[None, :] - col[:, None], -(NA_KW - 1), NA_KW - 1) + NA_KW - 1
    onehot = (col_idx[None] == np.arange(2 * NA_KW - 1)[:, None, None]).astype(np.float32)
    t = jnp.einsum("hic,cqk->hiqk", rpb, onehot, precision=lax.Precision.HIGHEST)
    pats = [t[:, NA_KH - 1 - s:2 * NA_KH - 1 - s].transpose(0, 2, 1, 3).reshape(NA_HEADS, GRID_W, NA_KH * GRID_W)
            for s in range(NA_KH)]
    return jnp.stack(pats)


def _stream_layer(x, mods, l, wts, *, latent, n_seq, seq_len, cache=None, rope=None):
    sh1, sc1, g1, sh2, sc2, g2 = mods
    qa, ka, va, u, qc, kc, vc, gates = _proj(
        x, 1.0 + sc1, sh1, wts["w_in"][l], wts["gmat"], wts["qn"][l], wts["kn"][l], rope,
        latent=latent, seq_len=seq_len)
    if latent:
        nat_k, nat_v, gqa_k, gqa_v = cache
        out_a = _na_latent(qa, ka, va, nat_k, nat_v, wts["na_bias"][l], n_seq=n_seq, seq_len=seq_len)
        k_all = jnp.concatenate([gqa_k, kc.reshape(n_seq, seq_len, GQ_KV_WIDTH)], axis=1)
        v_all = jnp.concatenate([gqa_v, vc.reshape(n_seq, seq_len, GQ_KV_WIDTH)], axis=1)
        out_c = _gqa_latent(qc, k_all, v_all, n_seq=n_seq, seq_len=seq_len)
    else:
        out_a, out_c = _ctx_attention(qa, ka, va, qc, kc, vc, seq_len=seq_len)
    out_b = _conformer_conv(u, wts["conv_w"][l], wts["conv_b"][l], wts["conv_ln_g"][l], wts["conv_ln_b"][l],
                            n_seq=n_seq, seq_len=seq_len)
    x1, h2, gate = _post(out_a, out_b, out_c, gates, x, g1, wts["ln1_g"][l], wts["ln1_b"][l], 1.0 + sc2, sh2,
                         wts["w_br_a"][l], wts["w_br_b"][l], wts["w_br_c"][l], wts["w_out"][l],
                         wts["w_router"], wts["b_router"], latent=latent, seq_len=seq_len)
    x2 = _moe(h2, gate, x1, g2, wts["ln2_g"][l], wts["ln2_b"][l], wts["w_gate_up"][l], wts["w_down"][l],
              latent=latent, seq_len=seq_len)
    return x2, (ka, va, kc, vc)


def kernel(x_prompt, x_sample, cache_nat_k, cache_nat_v, cache_gqa_k, cache_gqa_v, c, c_ctx, w_mod, b_mod, w_in, nat_rpb, conv_w, conv_b, conv_ln_g, conv_ln_b, q_norm_g, k_norm_g, w_br_a, w_br_b, w_br_c, w_out, ln1_g, ln1_b, ln2_g, ln2_b, w_router, b_router, w_gate_up, w_down):
    batch, seq, _ = x_prompt.shape
    dec_batch, dec_seq, _ = x_sample.shape
    past = cache_nat_k.shape[2]
    assert dec_batch == 2 and seq % TM_PROJ == 0 and dec_seq % TM_MOE == 0

    cond8 = jnp.zeros((8, D_MODEL), F32).at[0].set(c_ctx).at[1:1 + dec_batch].set(c)
    mods_all = _modulation(cond8, w_mod, b_mod)

    pad = LANES - N_EXPERTS
    vecd = lambda a: a.reshape(DEPTH, 1, -1)
    wts = dict(
        w_in=w_in.astype(BF16),
        gmat=_head_mean_matrix(),
        qn=jnp.tile(q_norm_g, (1, GQ_HEADS)).reshape(DEPTH, 1, GQ_Q_WIDTH),
        kn=jnp.tile(k_norm_g, (1, GQ_KV_HEADS)).reshape(DEPTH, 1, GQ_KV_WIDTH),
        na_bias=jnp.stack([_na_bias_table(nat_rpb[l]) for l in range(DEPTH)]),
        conv_w=conv_w, conv_b=conv_b, conv_ln_g=conv_ln_g, conv_ln_b=conv_ln_b,
        ln1_g=vecd(ln1_g), ln1_b=vecd(ln1_b), ln2_g=vecd(ln2_g), ln2_b=vecd(ln2_b),
        w_br_a=w_br_a.astype(BF16), w_br_b=w_br_b.astype(BF16), w_br_c=w_br_c.astype(BF16),
        w_out=w_out.astype(BF16),
        w_router=jnp.pad(w_router, ((0, 0), (0, pad))),
        b_router=jnp.pad(b_router, (0, pad), constant_values=NEG_INF).reshape(1, LANES),
        w_gate_up=w_gate_up.astype(BF16), w_down=w_down.astype(BF16),
    )
    rope = _rope_tables(dec_seq)

    def layer_mods(l):
        m = mods_all[l, :1 + dec_batch].reshape(1 + dec_batch, 6, 1, D_MODEL)
        return [m[:, i] for i in range(6)]

    xp = x_prompt.reshape(batch * seq, D_MODEL)
    xs = x_sample.reshape(dec_batch * dec_seq, D_MODEL)
    new_cache = []
    for l in range(DEPTH):
        mods = layer_mods(l)
        xp, ctx_kv = _stream_layer(xp, mods, l, wts, latent=False, n_seq=batch, seq_len=seq)
        new_cache.append(ctx_kv)
        cache = (cache_nat_k[:, l].reshape(dec_batch, past, NA_WIDTH).astype(BF16),
                 cache_nat_v[:, l].reshape(dec_batch, past, NA_WIDTH).astype(BF16),
                 cache_gqa_k[:, l].reshape(dec_batch, past, GQ_KV_WIDTH).astype(BF16),
                 cache_gqa_v[:, l].reshape(dec_batch, past, GQ_KV_WIDTH).astype(BF16))
        xs, _ = _stream_layer(xs, mods, l, wts, latent=True, n_seq=dec_batch, seq_len=dec_seq, cache=cache, rope=rope)

    def stack(i, heads):
        return jnp.stack([kv[i].reshape(batch, seq, heads, HEAD_DIM) for kv in new_cache], axis=1)

    return (xp.reshape(batch, seq, D_MODEL), xs.reshape(dec_batch, dec_seq, D_MODEL),
            stack(0, NA_HEADS), stack(1, NA_HEADS), stack(2, GQ_KV_HEADS), stack(3, GQ_KV_HEADS))
```

```python
import functools

import numpy as np
import jax
import jax.numpy as jnp
from jax import lax
from jax.experimental import pallas as pl
from jax.experimental.pallas import tpu as pltpu

F32 = jnp.float32
BF16 = jnp.bfloat16

D_MODEL = 1024
DEPTH = 2
GRID_W = 64
HEAD_DIM = 64
NA_HEADS = 8
NA_WIDTH = NA_HEADS * HEAD_DIM
NA_KH = 8
NA_KW = 16
CONV_CH = 512
CONV_K = 31
GQ_HEADS = 8
GQ_KV_HEADS = 2
GQ_Q_WIDTH = GQ_HEADS * HEAD_DIM
GQ_KV_WIDTH = GQ_KV_HEADS * HEAD_DIM
ROPE_THETA = 10000.0
N_EXPERTS = 16
EXPERTS_PER_GROUP = 4
N_GROUPS = N_EXPERTS // EXPERTS_PER_GROUP
GROUP_LANE = N_EXPERTS
MOE_ROW = D_MODEL + 128
D_EXPERT = 512
ALPHA = (2 * DEPTH) ** 0.25
LN_EPS = 1e-6
RMS_EPS = 1e-6
NEG_INF = -1e30
QK_SCALE = HEAD_DIM ** -0.5

_OFF = (0, 512, 1024, 1536, 2560, 3072, 3200, 3328, 6400)
IN_COLS = _OFF[-1]

LANES = 128
V7X_VMEM_BYTES = 64 * 2 ** 20
HALO = 16

TM_PROJ = 256
TM_POST = 256
TM_MOE = 512
TL_CONV = 256
TQ_GQA = 256


def _cparams(sem, vmem_mib):
    assert vmem_mib * 2 ** 20 < V7X_VMEM_BYTES
    return pltpu.CompilerParams(dimension_semantics=sem, vmem_limit_bytes=vmem_mib * 2 ** 20)


def _resident(shape):
    nd = len(shape)
    return pl.BlockSpec(shape, lambda *_: (0,) * nd, pipeline_mode=pl.Buffered(1))


def _dot(a, b):
    return jnp.dot(a, b, preferred_element_type=F32)


def _dot_nt(a, b):
    return lax.dot_general(a, b, (((1,), (1,)), ((), ())), preferred_element_type=F32)


def _split_bf16(a):
    hi = a.astype(BF16)
    lo = (a - hi.astype(F32)).astype(BF16)
    return hi, lo


def _dot3(a, b):
    a_hi, a_lo = _split_bf16(a)
    b_hi, b_lo = _split_bf16(b)
    return _dot(a_hi, b_hi) + _dot(a_lo, b_hi) + _dot(a_hi, b_lo)


def _sigmoid(x):
    return 1.0 / (1.0 + jnp.exp(-x))


def _layer_norm(x, g, b):
    mu = jnp.mean(x, axis=-1, keepdims=True)
    xc = x - mu
    var = jnp.mean(xc * xc, axis=-1, keepdims=True)
    return xc * lax.rsqrt(var + LN_EPS) * g + b


def _mod_kernel(cond_ref, w_ref, b_ref, o_ref):
    c = cond_ref[...]
    o_ref[...] = _dot3(c * _sigmoid(c), w_ref[...]) + b_ref[...]


def _modulation(cond8, w_mod, b_mod):
    tn = 1536
    n = w_mod.shape[-1]
    return pl.pallas_call(
        _mod_kernel,
        grid=(DEPTH, n // tn),
        in_specs=[
            pl.BlockSpec((8, D_MODEL), lambda l, j: (0, 0)),
            pl.BlockSpec((None, D_MODEL, tn), lambda l, j: (l, 0, j)),
            pl.BlockSpec((None, 1, tn), lambda l, j: (l, 0, j)),
        ],
        out_specs=pl.BlockSpec((None, 8, tn), lambda l, j: (l, 0, j)),
        out_shape=jax.ShapeDtypeStruct((DEPTH, 8, n), F32),
        compiler_params=_cparams(("arbitrary", "arbitrary"), 40),
        name="modulation",
    )(cond8, w_mod, b_mod.reshape(DEPTH, 1, n))


def _head_rms(x, gmat, gain):
    hi, lo = _split_bf16(x * x)
    ms = _dot(hi, gmat) + _dot(lo, gmat)
    return x * lax.rsqrt(ms + RMS_EPS) * gain


def _rope(x, c, s):
    w = x.shape[1]
    reps = w // LANES
    if reps > 1:
        c = jnp.concatenate([c] * reps, axis=1)
        s = jnp.concatenate([s] * reps, axis=1)
    lane = lax.broadcasted_iota(jnp.int32, x.shape, 1)
    partner = jnp.where((lane % 32) < 16, pltpu.roll(x, w - 16, 1), pltpu.roll(x, 16, 1))
    return x * c + partner * s


def _proj_kernel(latent, *refs):
    if latent:
        (x_ref, sc_ref, sh_ref, w_ref, gm_ref, qn_ref, kn_ref, c_ref, s_ref,
         qa_ref, ka_ref, va_ref, u_ref, qc_ref, kc_ref, vc_ref, gt_ref) = refs
    else:
        (x_ref, sc_ref, sh_ref, w_ref, gm_ref, qn_ref, kn_ref,
         qa_ref, ka_ref, va_ref, u_ref, qc_ref, kc_ref, vc_ref, gt_ref) = refs
    h = (x_ref[...] * sc_ref[...] + sh_ref[...]).astype(BF16)

    def mm(i):
        return _dot(h, w_ref[:, _OFF[i]:_OFF[i + 1]])

    qa_ref[...] = (mm(0) * QK_SCALE).astype(qa_ref.dtype)
    ka_ref[...] = mm(1).astype(ka_ref.dtype)
    va_ref[...] = mm(2).astype(va_ref.dtype)
    u_ref[...] = mm(3).astype(u_ref.dtype)
    qc = _head_rms(mm(4), gm_ref[...], qn_ref[...])
    kc = _head_rms(mm(5), gm_ref[:GQ_KV_WIDTH, :GQ_KV_WIDTH], kn_ref[...])
    if latent:
        qc = _rope(qc, c_ref[...], s_ref[...])
        kc = _rope(kc, c_ref[...], s_ref[...])
    qc_ref[...] = (qc * QK_SCALE).astype(qc_ref.dtype)
    kc_ref[...] = kc.astype(kc_ref.dtype)
    vc_ref[...] = mm(6).astype(vc_ref.dtype)
    gt_ref[...] = mm(7).astype(gt_ref.dtype)


def _mod_row_map(latent, tiles_per_seq):
    if latent:
        return lambda i: (1 + i // tiles_per_seq, 0, 0)
    return lambda i: (0, 0, 0)


def _proj(x, scale, shift, w_bf, gmat, qn, kn, rope, *, latent, seq_len):
    t = x.shape[0]
    tm = TM_PROJ
    kv_dt = BF16 if latent else F32
    row = lambda i: (i, 0)
    mod_map = _mod_row_map(latent, seq_len // tm)
    in_specs = [
        pl.BlockSpec((tm, D_MODEL), row),
        pl.BlockSpec((None, 1, D_MODEL), mod_map),
        pl.BlockSpec((None, 1, D_MODEL), mod_map),
        _resident((D_MODEL, IN_COLS)),
        _resident((GQ_Q_WIDTH, GQ_Q_WIDTH)),
        _resident((1, GQ_Q_WIDTH)),
        _resident((1, GQ_KV_WIDTH)),
    ]
    args = [x, scale, shift, w_bf, gmat, qn, kn]
    if latent:
        per = seq_len // tm
        in_specs += [pl.BlockSpec((tm, LANES), lambda i: (i % per, 0))] * 2
        args += list(rope)
    widths = (NA_WIDTH, NA_WIDTH, NA_WIDTH, 2 * CONV_CH, GQ_Q_WIDTH, GQ_KV_WIDTH, GQ_KV_WIDTH, 3 * D_MODEL)
    dtypes = (BF16, kv_dt, kv_dt, BF16, BF16, kv_dt, kv_dt, BF16)
    return pl.pallas_call(
        functools.partial(_proj_kernel, latent),
        grid=(t // tm,),
        in_specs=in_specs,
        out_specs=[pl.BlockSpec((tm, w), row) for w in widths],
        out_shape=[jax.ShapeDtypeStruct((t, w), dt) for w, dt in zip(widths, dtypes)],
        compiler_params=_cparams(("arbitrary",), 48),
        name="proj_lat" if latent else "proj_ctx",
    )(*args)


def _softmax_pv(pieces):
    m = None
    for s, _ in pieces:
        mi = jnp.max(s, axis=-1, keepdims=True)
        m = mi if m is None else jnp.maximum(m, mi)
    l = 0.0
    o = 0.0
    for s, v in pieces:
        p = jnp.exp(s - m)
        l = l + jnp.sum(p, axis=-1, keepdims=True)
        o = o + _dot(p.astype(BF16), v)
    return o / l


def _ctx_attn_kernel(qa_ref, ka_ref, va_ref, qc_ref, kc_ref, vc_ref, oa_ref, oc_ref):
    for h in range(NA_HEADS):
        sl = slice(h * HEAD_DIM, (h + 1) * HEAD_DIM)
        k = ka_ref[:, sl].astype(BF16)
        v = va_ref[:, sl].astype(BF16)
        oa_ref[:, sl] = _softmax_pv([(_dot_nt(qa_ref[:, sl], k), v)]).astype(oa_ref.dtype)
    for h in range(GQ_HEADS):
        sl = slice(h * HEAD_DIM, (h + 1) * HEAD_DIM)
        g = h // (GQ_HEADS // GQ_KV_HEADS)
        gsl = slice(g * HEAD_DIM, (g + 1) * HEAD_DIM)
        k = kc_ref[:, gsl].astype(BF16)
        v = vc_ref[:, gsl].astype(BF16)
        oc_ref[:, sl] = _softmax_pv([(_dot_nt(qc_ref[:, sl], k), v)]).astype(oc_ref.dtype)


def _ctx_attention(qa, ka, va, qc, kc, vc, *, seq_len):
    t = qa.shape[0]
    row = lambda i: (i, 0)
    spec = lambda w: pl.BlockSpec((seq_len, w), row)
    return pl.pallas_call(
        _ctx_attn_kernel,
        grid=(t // seq_len,),
        in_specs=[spec(NA_WIDTH), spec(NA_WIDTH), spec(NA_WIDTH), spec(GQ_Q_WIDTH), spec(GQ_KV_WIDTH), spec(GQ_KV_WIDTH)],
        out_specs=[spec(NA_WIDTH), spec(GQ_Q_WIDTH)],
        out_shape=[jax.ShapeDtypeStruct((t, NA_WIDTH), BF16), jax.ShapeDtypeStruct((t, GQ_Q_WIDTH), BF16)],
        compiler_params=_cparams(("arbitrary",), 32),
        name="attn_ctx",
    )(qa, ka, va, qc, kc, vc)


def _na_lat_kernel(rows, q_ref, k_ref, v_ref, kc_ref, vc_ref, bias_ref, o_ref):
    r = pl.program_id(1)
    row_start = jnp.clip(r - NA_KH // 2, 0, rows - NA_KH)
    shift = r - row_start
    base = pl.multiple_of(row_start * GRID_W, GRID_W)
    nk = NA_KH * GRID_W
    kblk = k_ref[pl.ds(base, nk), :]
    vblk = v_ref[pl.ds(base, nk), :]
    qcol = lax.broadcasted_iota(jnp.int32, (GRID_W, nk), 0)
    kcol = lax.broadcasted_iota(jnp.int32, (GRID_W, nk), 1) % GRID_W
    col_start = jnp.clip(qcol - NA_KW // 2, 0, GRID_W - NA_KW)
    valid = (kcol >= col_start) & (kcol < col_start + NA_KW)
    for h in range(NA_HEADS):
        sl = slice(h * HEAD_DIM, (h + 1) * HEAD_DIM)
        q = q_ref[:, sl]
        s_lat = jnp.where(valid, _dot_nt(q, kblk[:, sl]) + bias_ref[shift, h], NEG_INF)
        s_ctx = _dot_nt(q, kc_ref[:, sl])
        o_ref[:, sl] = _softmax_pv([(s_lat, vblk[:, sl]), (s_ctx, vc_ref[:, sl])]).astype(o_ref.dtype)


def _na_latent(q, k, v, k_ctx, v_ctx, bias, *, n_seq, seq_len):
    rows = seq_len // GRID_W
    nb = bias.shape
    return pl.pallas_call(
        functools.partial(_na_lat_kernel, rows),
        grid=(n_seq, rows),
        in_specs=[
            pl.BlockSpec((GRID_W, NA_WIDTH), lambda b, r: (b * rows + r, 0)),
            pl.BlockSpec((seq_len, NA_WIDTH), lambda b, r: (b, 0)),
            pl.BlockSpec((seq_len, NA_WIDTH), lambda b, r: (b, 0)),
            pl.BlockSpec((None,) + k_ctx.shape[1:], lambda b, r: (b, 0, 0)),
            pl.BlockSpec((None,) + v_ctx.shape[1:], lambda b, r: (b, 0, 0)),
            _resident(nb),
        ],
        out_specs=pl.BlockSpec((GRID_W, NA_WIDTH), lambda b, r: (b * rows + r, 0)),
        out_shape=jax.ShapeDtypeStruct(q.shape, BF16),
        compiler_params=_cparams(("arbitrary", "arbitrary"), 48),
        name="attn_na_lat",
    )(q, k, v, k_ctx, v_ctx, bias)


def _gqa_lat_kernel(q_ref, k_ref, v_ref, o_ref):
    for h in range(GQ_HEADS):
        sl = slice(h * HEAD_DIM, (h + 1) * HEAD_DIM)
        g = h // (GQ_HEADS // GQ_KV_HEADS)
        gsl = slice(g * HEAD_DIM, (g + 1) * HEAD_DIM)
        s = _dot_nt(q_ref[:, sl], k_ref[:, gsl])
        o_ref[:, sl] = _softmax_pv([(s, v_ref[:, gsl])]).astype(o_ref.dtype)


def _gqa_latent(q, k_all, v_all, *, n_seq, seq_len):
    per = seq_len // TQ_GQA
    s_all = k_all.shape[1]
    return pl.pallas_call(
        _gqa_lat_kernel,
        grid=(n_seq, per),
        in_specs=[
            pl.BlockSpec((TQ_GQA, GQ_Q_WIDTH), lambda b, t: (b * per + t, 0)),
            pl.BlockSpec((None, s_all, GQ_KV_WIDTH), lambda b, t: (b, 0, 0)),
            pl.BlockSpec((None, s_all, GQ_KV_WIDTH), lambda b, t: (b, 0, 0)),
        ],
        out_specs=pl.BlockSpec((TQ_GQA, GQ_Q_WIDTH), lambda b, t: (b * per + t, 0)),
        out_shape=jax.ShapeDtypeStruct(q.shape, BF16),
        compiler_params=_cparams(("arbitrary", "arbitrary"), 48),
        name="attn_gqa_lat",
    )(q, k_all, v_all)


def _conv_kernel(u_ref, up_ref, un_ref, w_ref, b_ref, g_ref, beta_ref, o_ref, slab_ref):
    t = pl.program_id(1)
    nt = pl.num_programs(1)
    tl = u_ref.shape[0]

    def glu(v):
        return v[:, :CONV_CH].astype(F32) * _sigmoid(v[:, CONV_CH:].astype(F32))

    slab_ref[0:HALO, :] = jnp.where(t > 0, glu(up_ref[...]), 0.0)
    slab_ref[HALO:HALO + tl, :] = glu(u_ref[...])
    slab_ref[HALO + tl:, :] = jnp.where(t < nt - 1, glu(un_ref[...]), 0.0)
    acc = jnp.zeros((tl, CONV_CH), F32)
    for k in range(CONV_K):
        start = HALO - CONV_K // 2 + k
        acc = acc + slab_ref[start:start + tl, :] * w_ref[k:k + 1, :]
    y = _layer_norm(acc + b_ref[...], g_ref[...], beta_ref[...])
    o_ref[...] = (y * _sigmoid(y)).astype(o_ref.dtype)


def _conformer_conv(u, conv_w, conv_b, ln_g, ln_b, *, n_seq, seq_len):
    tl = TL_CONV
    nt = seq_len // tl
    hb = tl // HALO
    last_hb = seq_len // HALO - 1
    u3 = u.reshape(n_seq, seq_len, 2 * CONV_CH)
    vec = lambda a: a.reshape(1, CONV_CH)
    out = pl.pallas_call(
        _conv_kernel,
        grid=(n_seq, nt),
        in_specs=[
            pl.BlockSpec((None, tl, 2 * CONV_CH), lambda s, t: (s, t, 0)),
            pl.BlockSpec((None, HALO, 2 * CONV_CH), lambda s, t: (s, jnp.maximum(t * hb - 1, 0), 0)),
            pl.BlockSpec((None, HALO, 2 * CONV_CH), lambda s, t: (s, jnp.minimum((t + 1) * hb, last_hb), 0)),
            _resident((CONV_K, CONV_CH)),
            _resident((1, CONV_CH)),
            _resident((1, CONV_CH)),
            _resident((1, CONV_CH)),
        ],
        out_specs=pl.BlockSpec((None, tl, CONV_CH), lambda s, t: (s, t, 0)),
        out_shape=jax.ShapeDtypeStruct((n_seq, seq_len, CONV_CH), BF16),
        scratch_shapes=[pltpu.VMEM((tl + 2 * HALO, CONV_CH), F32)],
        compiler_params=_cparams(("arbitrary", "arbitrary"), 32),
        name="conformer_conv",
    )(u3, u3, u3, conv_w, vec(conv_b), vec(ln_g), vec(ln_b))
    return out.reshape(n_seq * seq_len, CONV_CH)


def _router_gates(logits):
    lane = lax.broadcasted_iota(jnp.int32, logits.shape, 1)
    m = jnp.max(logits, axis=-1, keepdims=True)
    e = jnp.exp(logits - m)
    probs = e / jnp.sum(e, axis=-1, keepdims=True)
    p1 = jnp.max(probs, axis=-1, keepdims=True)
    i1 = jnp.min(jnp.where(probs == p1, lane, LANES), axis=-1, keepdims=True)
    in_group = (lane // EXPERTS_PER_GROUP) == (i1 // EXPERTS_PER_GROUP)
    cand = jnp.where(in_group, jnp.where(lane == i1, -1.0, probs), -1.0)
    p2 = jnp.max(cand, axis=-1, keepdims=True)
    i2 = jnp.min(jnp.where(cand == p2, lane, LANES), axis=-1, keepdims=True)
    den = p1 + p2
    gates = jnp.where(lane == i1, p1 / den, 0.0) + jnp.where(lane == i2, p2 / den, 0.0)
    group = (i1 // EXPERTS_PER_GROUP).astype(F32)
    return jnp.where(lane == GROUP_LANE, group, gates)


def _post_kernel(oa_ref, ob_ref, oc_ref, gt_ref, x_ref, g1_ref, lg_ref, lb_ref, sc2_ref, sh2_ref,
                 wa_ref, wb_ref, wc_ref, wo_ref, wr_ref, br_ref, x1_ref, hg_ref):
    sg = _sigmoid(gt_ref[...].astype(F32))
    merged = (sg[:, :D_MODEL] * _dot(oa_ref[...], wa_ref[...])
              + sg[:, D_MODEL:2 * D_MODEL] * _dot(ob_ref[...], wb_ref[...])
              + sg[:, 2 * D_MODEL:] * _dot(oc_ref[...], wc_ref[...]))
    y = _dot(merged.astype(BF16), wo_ref[...])
    x1 = _layer_norm(ALPHA * x_ref[...] + g1_ref[...] * y, lg_ref[...], lb_ref[...])
    x1_ref[...] = x1
    h2 = x1 * sc2_ref[...] + sh2_ref[...]
    hg_ref[:, :D_MODEL] = h2
    hg_ref[:, D_MODEL:] = _router_gates(_dot3(h2, wr_ref[...]) + br_ref[...])


def _post(oa, ob, oc, gates, x, g1, ln_g, ln_b, scale2, shift2, wa, wb, wc, wo, wr, br, *, latent, seq_len):
    t = x.shape[0]
    tm = TM_POST
    row = lambda i: (i, 0)
    mod_map = _mod_row_map(latent, seq_len // tm)
    mod = pl.BlockSpec((None, 1, D_MODEL), mod_map)
    return pl.pallas_call(
        _post_kernel,
        grid=(t // tm,),
        in_specs=[
            pl.BlockSpec((tm, NA_WIDTH), row), pl.BlockSpec((tm, CONV_CH), row), pl.BlockSpec((tm, GQ_Q_WIDTH), row),
            pl.BlockSpec((tm, 3 * D_MODEL), row), pl.BlockSpec((tm, D_MODEL), row),
            mod, _resident((1, D_MODEL)), _resident((1, D_MODEL)), mod, mod,
            _resident(wa.shape), _resident(wb.shape), _resident(wc.shape), _resident(wo.shape),
            _resident(wr.shape), _resident(br.shape),
        ],
        out_specs=[pl.BlockSpec((tm, D_MODEL), row), pl.BlockSpec((tm, MOE_ROW), row)],
        out_shape=[jax.ShapeDtypeStruct((t, D_MODEL), F32), jax.ShapeDtypeStruct((t, MOE_ROW), F32)],
        compiler_params=_cparams(("arbitrary",), 40),
        name="post_lat" if latent else "post_ctx",
    )(oa, ob, oc, gates, x, g1, ln_g, ln_b, scale2, shift2, wa, wb, wc, wo, wr, br)


def _route(hg, tm):
    n = hg.shape[0]
    n_tiles = n // tm + N_GROUPS
    grp = hg[:, D_MODEL + GROUP_LANE].astype(jnp.int32)
    onehot = (grp[:, None] == jnp.arange(N_GROUPS, dtype=jnp.int32)[None, :]).astype(jnp.int32)
    csum = jnp.cumsum(onehot, axis=0)
    rank = jnp.sum((csum - onehot) * onehot, axis=1)
    tiles = (csum[-1] + tm - 1) // tm
    tile_end = jnp.cumsum(tiles)
    slot = jnp.sum(onehot * (tile_end - tiles)[None, :], axis=1) * tm + rank
    tile_ids = jnp.arange(n_tiles, dtype=jnp.int32)
    tile_group = jnp.minimum(jnp.sum((tile_ids[:, None] >= tile_end[None, :]).astype(jnp.int32), axis=1), N_GROUPS - 1)
    r = jnp.arange(n_tiles * tm, dtype=jnp.int32)
    tok = jnp.arange(n, dtype=jnp.int32)
    src = jnp.zeros((n_tiles * tm,), jnp.int32).at[slot].set(tok)
    dst = (n + ((r // tm) % 2) * tm + r % tm).at[slot].set(tok)
    return tile_group.astype(jnp.int32), src, dst, tile_end[-1:].astype(jnp.int32)


def _moe_kernel(tg_ref, src_ref, dst_ref, nused_ref, h_hbm, wgu_ref, wd_ref, y_hbm, hbuf, ybuf, gsem, ssem):
    t = pl.program_id(0)
    n_used = nused_ref[0]
    slot = t % 2
    tm = hbuf.shape[1]

    def gather_row(tile, s, r):
        tok = src_ref[tile * tm + r]
        return pltpu.make_async_copy(h_hbm.at[pl.ds(tok, 1), :], hbuf.at[s, pl.ds(r, 1), :], gsem.at[s])

    def scatter_row(tile, s, r):
        row = dst_ref[tile * tm + r]
        return pltpu.make_async_copy(ybuf.at[s, pl.ds(r, 1), :], y_hbm.at[pl.ds(row, 1), :], ssem.at[s])

    def for_rows(fn):
        def body(r, carry):
            fn(r)
            return carry
        lax.fori_loop(0, tm, body, 0, unroll=8)

    @pl.when(t == 0)
    def _():
        for_rows(lambda r: gather_row(0, 0, r).start())

    @pl.when(t + 1 < n_used)
    def _():
        for_rows(lambda r: gather_row(t + 1, 1 - slot, r).start())

    @pl.when(t < n_used)
    def _():
        for_rows(lambda r: gather_row(t, slot, r).wait())

        @pl.when(t >= 2)
        def _():
            for_rows(lambda r: scatter_row(t - 2, slot, r).wait())

        first = tg_ref[t] * EXPERTS_PER_GROUP
        x = hbuf[slot]
        h = x[:, :D_MODEL].astype(BF16)
        gates = x[:, D_MODEL:]
        lane = lax.broadcasted_iota(jnp.int32, gates.shape, 1)
        acc = jnp.zeros((tm, D_MODEL), F32)
        for e in range(EXPERTS_PER_GROUP):
            gu = _dot(h, wgu_ref[e])
            a = gu[:, :D_EXPERT]
            ge = jnp.sum(jnp.where(lane == first + e, gates, 0.0), axis=-1, keepdims=True)
            hid = a * _sigmoid(a) * gu[:, D_EXPERT:] * ge
            acc = acc + _dot(hid.astype(BF16), wd_ref[e])
        ybuf[slot] = acc
        for_rows(lambda r: scatter_row(t, slot, r).start())

        @pl.when(t == n_used - 1)
        def _():
            for_rows(lambda r: scatter_row(t, slot, r).wait())

            @pl.when(t >= 1)
            def _():
                for_rows(lambda r: scatter_row(t - 1, 1 - slot, r).wait())

            n = y_hbm.shape[0] - 2 * tm
            ybuf[0] = jnp.zeros((tm, D_MODEL), F32)
            for half in range(2):
                fill = pltpu.make_async_copy(ybuf.at[0], y_hbm.at[pl.ds(n + half * tm, tm), :], ssem.at[half])
                fill.start()
                fill.wait()


def _moe_experts(hg, route, wgu, wd):
    n = hg.shape[0]
    tm = TM_MOE
    tile_group, src, dst, n_used = route
    n_tiles = tile_group.shape[0]
    grid_spec = pltpu.PrefetchScalarGridSpec(
        num_scalar_prefetch=4,
        grid=(n_tiles,),
        in_specs=[
            pl.BlockSpec(memory_space=pl.ANY),
            pl.BlockSpec((EXPERTS_PER_GROUP, D_MODEL, 2 * D_EXPERT), lambda t, tg, s, d, nu: (tg[t], 0, 0)),
            pl.BlockSpec((EXPERTS_PER_GROUP, D_EXPERT, D_MODEL), lambda t, tg, s, d, nu: (tg[t], 0, 0)),
        ],
        out_specs=pl.BlockSpec(memory_space=pl.ANY),
        scratch_shapes=[
            pltpu.VMEM((2, tm, MOE_ROW), F32),
            pltpu.VMEM((2, tm, D_MODEL), F32),
            pltpu.SemaphoreType.DMA((2,)),
            pltpu.SemaphoreType.DMA((2,)),
        ],
    )
    return pl.pallas_call(
        _moe_kernel,
        grid_spec=grid_spec,
        out_shape=jax.ShapeDtypeStruct((n + 2 * tm, D_MODEL), F32),
        compiler_params=_cparams(("arbitrary",), 56),
        name="moe_experts",
    )(tile_group, src, dst, n_used, hg, wgu, wd)


def _moe_out_kernel(x1_ref, y_ref, g2_ref, lg_ref, lb_ref, o_ref):
    o_ref[...] = _layer_norm(ALPHA * x1_ref[...] + g2_ref[...] * y_ref[...], lg_ref[...], lb_ref[...])


def _moe_out(x1, y, g2, ln_g, ln_b, *, latent, seq_len):
    t = x1.shape[0]
    tm = TM_POST
    row = lambda i: (i, 0)
    return pl.pallas_call(
        _moe_out_kernel,
        grid=(t // tm,),
        in_specs=[pl.BlockSpec((tm, D_MODEL), row), pl.BlockSpec((tm, D_MODEL), row),
                  pl.BlockSpec((None, 1, D_MODEL), _mod_row_map(latent, seq_len // tm)),
                  _resident((1, D_MODEL)), _resident((1, D_MODEL))],
        out_specs=pl.BlockSpec((tm, D_MODEL), row),
        out_shape=jax.ShapeDtypeStruct((t, D_MODEL), F32),
        compiler_params=_cparams(("arbitrary",), 32),
        name="moe_out",
    )(x1, y, g2, ln_g, ln_b)


def _moe(hg, x1, g2, ln_g, ln_b, wgu, wd, *, latent, seq_len):
    y = _moe_experts(hg, _route(hg, TM_MOE), wgu, wd)
    return _moe_out(x1, y, g2, ln_g, ln_b, latent=latent, seq_len=seq_len)


def _head_mean_matrix():
    i = jnp.arange(GQ_Q_WIDTH) // HEAD_DIM
    return jnp.where(i[:, None] == i[None, :], 1.0 / HEAD_DIM, 0.0).astype(BF16)


def _rope_tables(n_tokens):
    t = jnp.arange(n_tokens, dtype=jnp.int32)
    axis_dim = HEAD_DIM // 2
    inv_freq = ROPE_THETA ** (-jnp.arange(0, axis_dim, 2, dtype=F32) / axis_dim)
    ar = (t // GRID_W).astype(F32)[:, None] * inv_freq
    ac = (t % GRID_W).astype(F32)[:, None] * inv_freq
    c = jnp.concatenate([jnp.cos(ar), jnp.cos(ar), jnp.cos(ac), jnp.cos(ac)], axis=1)
    s = jnp.concatenate([-jnp.sin(ar), jnp.sin(ar), -jnp.sin(ac), jnp.sin(ac)], axis=1)
    return jnp.tile(c, (1, LANES // HEAD_DIM)), jnp.tile(s, (1, LANES // HEAD_DIM))


def _na_bias_table(rpb):
    col = np.arange(GRID_W)
    col_idx = np.clip(col[None, :] - col[:, None], -(NA_KW - 1), NA_KW - 1) + NA_KW - 1
    onehot = (col_idx[None] == np.arange(2 * NA_KW - 1)[:, None, None]).astype(np.float32)
    t = jnp.einsum("hic,cqk->hiqk", rpb, onehot, precision=lax.Precision.HIGHEST)
    pats = [t[:, NA_KH - 1 - s:2 * NA_KH - 1 - s].transpose(0, 2, 1, 3).reshape(NA_HEADS, GRID_W, NA_KH * GRID_W)
            for s in range(NA_KH)]
    return jnp.stack(pats)


def _stream_layer(x, mods, l, wts, *, latent, n_seq, seq_len, cache=None, rope=None):
    sh1, sc1, g1, sh2, sc2, g2 = mods
    qa, ka, va, u, qc, kc, vc, gates = _proj(
        x, 1.0 + sc1, sh1, wts["w_in"][l], wts["gmat"], wts["qn"][l], wts["kn"][l], rope,
        latent=latent, seq_len=seq_len)
    if latent:
        nat_k, nat_v, gqa_k, gqa_v = cache
        out_a = _na_latent(qa, ka, va, nat_k, nat_v, wts["na_bias"][l], n_seq=n_seq, seq_len=seq_len)
        k_all = jnp.concatenate([gqa_k, kc.reshape(n_seq, seq_len, GQ_KV_WIDTH)], axis=1)
        v_all = jnp.concatenate([gqa_v, vc.reshape(n_seq, seq_len, GQ_KV_WIDTH)], axis=1)
        out_c = _gqa_latent(qc, k_all, v_all, n_seq=n_seq, seq_len=seq_len)
    else:
        out_a, out_c = _ctx_attention(qa, ka, va, qc, kc, vc, seq_len=seq_len)
    out_b = _conformer_conv(u, wts["conv_w"][l], wts["conv_b"][l], wts["conv_ln_g"][l], wts["conv_ln_b"][l],
                            n_seq=n_seq, seq_len=seq_len)
    x1, hg = _post(out_a, out_b, out_c, gates, x, g1, wts["ln1_g"][l], wts["ln1_b"][l], 1.0 + sc2, sh2,
                         wts["w_br_a"][l], wts["w_br_b"][l], wts["w_br_c"][l], wts["w_out"][l],
                         wts["w_router"], wts["b_router"], latent=latent, seq_len=seq_len)
    x2 = _moe(hg, x1, g2, wts["ln2_g"][l], wts["ln2_b"][l], wts["w_gate_up"][l], wts["w_down"][l],
              latent=latent, seq_len=seq_len)
    return x2, (ka, va, kc, vc)


def kernel(x_prompt, x_sample, cache_nat_k, cache_nat_v, cache_gqa_k, cache_gqa_v, c, c_ctx, w_mod, b_mod, w_in, nat_rpb, conv_w, conv_b, conv_ln_g, conv_ln_b, q_norm_g, k_norm_g, w_br_a, w_br_b, w_br_c, w_out, ln1_g, ln1_b, ln2_g, ln2_b, w_router, b_router, w_gate_up, w_down):
    batch, seq, _ = x_prompt.shape
    dec_batch, dec_seq, _ = x_sample.shape
    past = cache_nat_k.shape[2]
    assert dec_batch == 2 and seq % TM_PROJ == 0 and dec_seq % TM_MOE == 0

    cond8 = jnp.zeros((8, D_MODEL), F32).at[0].set(c_ctx).at[1:1 + dec_batch].set(c)
    mods_all = _modulation(cond8, w_mod, b_mod)

    pad = LANES - N_EXPERTS
    vecd = lambda a: a.reshape(DEPTH, 1, -1)
    wts = dict(
        w_in=w_in.astype(BF16),
        gmat=_head_mean_matrix(),
        qn=jnp.tile(q_norm_g, (1, GQ_HEADS)).reshape(DEPTH, 1, GQ_Q_WIDTH),
        kn=jnp.tile(k_norm_g, (1, GQ_KV_HEADS)).reshape(DEPTH, 1, GQ_KV_WIDTH),
        na_bias=jnp.stack([_na_bias_table(nat_rpb[l]) for l in range(DEPTH)]),
        conv_w=conv_w, conv_b=conv_b, conv_ln_g=conv_ln_g, conv_ln_b=conv_ln_b,
        ln1_g=vecd(ln1_g), ln1_b=vecd(ln1_b), ln2_g=vecd(ln2_g), ln2_b=vecd(ln2_b),
        w_br_a=w_br_a.astype(BF16), w_br_b=w_br_b.astype(BF16), w_br_c=w_br_c.astype(BF16),
        w_out=w_out.astype(BF16),
        w_router=jnp.pad(w_router, ((0, 0), (0, pad))),
        b_router=jnp.pad(b_router, (0, pad), constant_values=NEG_INF).reshape(1, LANES),
        w_gate_up=w_gate_up.astype(BF16), w_down=w_down.astype(BF16),
    )
    rope = _rope_tables(dec_seq)

    def layer_mods(l):
        m = mods_all[l, :1 + dec_batch].reshape(1 + dec_batch, 6, 1, D_MODEL)
        return [m[:, i] for i in range(6)]

    xp = x_prompt.reshape(batch * seq, D_MODEL)
    xs = x_sample.reshape(dec_batch * dec_seq, D_MODEL)
    new_cache = []
    for l in range(DEPTH):
        mods = layer_mods(l)
        xp, ctx_kv = _stream_layer(xp, mods, l, wts, latent=False, n_seq=batch, seq_len=seq)
        new_cache.append(ctx_kv)
        cache = (cache_nat_k[:, l].reshape(dec_batch, past, NA_WIDTH).astype(BF16),
                 cache_nat_v[:, l].reshape(dec_batch, past, NA_WIDTH).astype(BF16),
                 cache_gqa_k[:, l].reshape(dec_batch, past, GQ_KV_WIDTH).astype(BF16),
                 cache_gqa_v[:, l].reshape(dec_batch, past, GQ_KV_WIDTH).astype(BF16))
        xs, _ = _stream_layer(xs, mods, l, wts, latent=True, n_seq=dec_batch, seq_len=dec_seq, cache=cache, rope=rope)

    def stack(i, heads):
        return jnp.stack([kv[i].reshape(batch, seq, heads, HEAD_DIM) for kv in new_cache], axis=1)

    return (xp.reshape(batch, seq, D_MODEL), xs.reshape(dec_batch, dec_seq, D_MODEL),
            stack(0, NA_HEADS), stack(1, NA_HEADS), stack(2, GQ_KV_HEADS), stack(3, GQ_KV_HEADS))
```

```python
import functools

import numpy as np
import jax
import jax.numpy as jnp
from jax import lax
from jax.experimental import pallas as pl
from jax.experimental.pallas import tpu as pltpu

F32 = jnp.float32
BF16 = jnp.bfloat16

D_MODEL = 1024
DEPTH = 2
GRID_W = 64
HEAD_DIM = 64
NA_HEADS = 8
NA_WIDTH = NA_HEADS * HEAD_DIM
NA_KH = 8
NA_KW = 16
NA_QROWS = 4
NA_WIN_ROWS = NA_QROWS + NA_KH
CONV_CH = 512
CONV_K = 31
GQ_HEADS = 8
GQ_KV_HEADS = 2
GQ_Q_WIDTH = GQ_HEADS * HEAD_DIM
GQ_KV_WIDTH = GQ_KV_HEADS * HEAD_DIM
ROPE_THETA = 10000.0
N_EXPERTS = 16
EXPERTS_PER_GROUP = 4
N_GROUPS = N_EXPERTS // EXPERTS_PER_GROUP
GROUP_LANE = N_EXPERTS
MOE_ROW = D_MODEL + 128
D_EXPERT = 512
ALPHA = (2 * DEPTH) ** 0.25
LN_EPS = 1e-6
RMS_EPS = 1e-6
NEG_INF = -1e30
QK_SCALE = HEAD_DIM ** -0.5

_OFF = (0, 512, 1024, 1536, 2560, 3072, 3200, 3328, 6400)
IN_COLS = _OFF[-1]

LANES = 128
V7X_VMEM_BYTES = 64 * 2 ** 20
HALO = 16

TM_PROJ = 256
TM_POST = 256
TM_MOE = 512
TL_CONV = 256
TQ_GQA = 256


def _cparams(sem, vmem_mib):
    assert vmem_mib * 2 ** 20 < V7X_VMEM_BYTES
    return pltpu.CompilerParams(dimension_semantics=sem, vmem_limit_bytes=vmem_mib * 2 ** 20)


def _resident(shape):
    nd = len(shape)
    return pl.BlockSpec(shape, lambda *_: (0,) * nd, pipeline_mode=pl.Buffered(1))


def _dot(a, b):
    return jnp.dot(a, b, preferred_element_type=F32)


def _dot_nt(a, b):
    return lax.dot_general(a, b, (((1,), (1,)), ((), ())), preferred_element_type=F32)


def _split_bf16(a):
    hi = a.astype(BF16)
    lo = (a - hi.astype(F32)).astype(BF16)
    return hi, lo


def _dot3(a, b):
    a_hi, a_lo = _split_bf16(a)
    b_hi, b_lo = _split_bf16(b)
    return _dot(a_hi, b_hi) + _dot(a_lo, b_hi) + _dot(a_hi, b_lo)


def _sigmoid(x):
    return 1.0 / (1.0 + jnp.exp(-x))


def _layer_norm(x, g, b):
    mu = jnp.mean(x, axis=-1, keepdims=True)
    xc = x - mu
    var = jnp.mean(xc * xc, axis=-1, keepdims=True)
    return xc * lax.rsqrt(var + LN_EPS) * g + b


def _mod_kernel(cond_ref, w_ref, b_ref, o_ref):
    c = cond_ref[...]
    o_ref[...] = _dot3(c * _sigmoid(c), w_ref[...]) + b_ref[...]


def _modulation(cond8, w_mod, b_mod):
    tn = 1536
    n = w_mod.shape[-1]
    return pl.pallas_call(
        _mod_kernel,
        grid=(DEPTH, n // tn),
        in_specs=[
            pl.BlockSpec((8, D_MODEL), lambda l, j: (0, 0)),
            pl.BlockSpec((None, D_MODEL, tn), lambda l, j: (l, 0, j)),
            pl.BlockSpec((None, 1, tn), lambda l, j: (l, 0, j)),
        ],
        out_specs=pl.BlockSpec((None, 8, tn), lambda l, j: (l, 0, j)),
        out_shape=jax.ShapeDtypeStruct((DEPTH, 8, n), F32),
        compiler_params=_cparams(("arbitrary", "arbitrary"), 40),
        name="modulation",
    )(cond8, w_mod, b_mod.reshape(DEPTH, 1, n))


def _head_rms(x, gmat, gain):
    hi, lo = _split_bf16(x * x)
    ms = _dot(hi, gmat) + _dot(lo, gmat)
    return x * lax.rsqrt(ms + RMS_EPS) * gain


def _rope(x, c, s):
    w = x.shape[1]
    reps = w // LANES
    if reps > 1:
        c = jnp.concatenate([c] * reps, axis=1)
        s = jnp.concatenate([s] * reps, axis=1)
    lane = lax.broadcasted_iota(jnp.int32, x.shape, 1)
    partner = jnp.where((lane % 32) < 16, pltpu.roll(x, w - 16, 1), pltpu.roll(x, 16, 1))
    return x * c + partner * s


def _proj_kernel(latent, *refs):
    if latent:
        (x_ref, sc_ref, sh_ref, w_ref, gm_ref, qn_ref, kn_ref, c_ref, s_ref,
         qa_ref, ka_ref, va_ref, u_ref, qc_ref, kc_ref, vc_ref, gt_ref) = refs
    else:
        (x_ref, sc_ref, sh_ref, w_ref, gm_ref, qn_ref, kn_ref,
         qa_ref, ka_ref, va_ref, u_ref, qc_ref, kc_ref, vc_ref, gt_ref) = refs
    h = (x_ref[...] * sc_ref[...] + sh_ref[...]).astype(BF16)

    def mm(i):
        return _dot(h, w_ref[:, _OFF[i]:_OFF[i + 1]])

    qa_ref[...] = (mm(0) * QK_SCALE).astype(qa_ref.dtype)
    ka_ref[...] = mm(1).astype(ka_ref.dtype)
    va_ref[...] = mm(2).astype(va_ref.dtype)
    u_ref[...] = mm(3).astype(u_ref.dtype)
    qc = _head_rms(mm(4), gm_ref[...], qn_ref[...])
    kc = _head_rms(mm(5), gm_ref[:GQ_KV_WIDTH, :GQ_KV_WIDTH], kn_ref[...])
    if latent:
        qc = _rope(qc, c_ref[...], s_ref[...])
        kc = _rope(kc, c_ref[...], s_ref[...])
    qc_ref[...] = (qc * QK_SCALE).astype(qc_ref.dtype)
    kc_ref[...] = kc.astype(kc_ref.dtype)
    vc_ref[...] = mm(6).astype(vc_ref.dtype)
    gt_ref[...] = mm(7).astype(gt_ref.dtype)


def _mod_row_map(latent, tiles_per_seq):
    if latent:
        return lambda i: (1 + i // tiles_per_seq, 0, 0)
    return lambda i: (0, 0, 0)


def _proj(x, scale, shift, w_bf, gmat, qn, kn, rope, *, latent, seq_len):
    t = x.shape[0]
    tm = TM_PROJ
    kv_dt = BF16 if latent else F32
    row = lambda i: (i, 0)
    mod_map = _mod_row_map(latent, seq_len // tm)
    in_specs = [
        pl.BlockSpec((tm, D_MODEL), row),
        pl.BlockSpec((None, 1, D_MODEL), mod_map),
        pl.BlockSpec((None, 1, D_MODEL), mod_map),
        _resident((D_MODEL, IN_COLS)),
        _resident((GQ_Q_WIDTH, GQ_Q_WIDTH)),
        _resident((1, GQ_Q_WIDTH)),
        _resident((1, GQ_KV_WIDTH)),
    ]
    args = [x, scale, shift, w_bf, gmat, qn, kn]
    if latent:
        per = seq_len // tm
        in_specs += [pl.BlockSpec((tm, LANES), lambda i: (i % per, 0))] * 2
        args += list(rope)
    widths = (NA_WIDTH, NA_WIDTH, NA_WIDTH, 2 * CONV_CH, GQ_Q_WIDTH, GQ_KV_WIDTH, GQ_KV_WIDTH, 3 * D_MODEL)
    dtypes = (BF16, kv_dt, kv_dt, BF16, BF16, kv_dt, kv_dt, BF16)
    return pl.pallas_call(
        functools.partial(_proj_kernel, latent),
        grid=(t // tm,),
        in_specs=in_specs,
        out_specs=[pl.BlockSpec((tm, w), row) for w in widths],
        out_shape=[jax.ShapeDtypeStruct((t, w), dt) for w, dt in zip(widths, dtypes)],
        compiler_params=_cparams(("arbitrary",), 48),
        name="proj_lat" if latent else "proj_ctx",
    )(*args)


def _pair(ref_or_val, j):
    return ref_or_val[:, j * LANES:(j + 1) * LANES]


def _keep_half(x, half, move_to=None):
    lane = lax.broadcasted_iota(jnp.int32, x.shape, 1)
    keep = (lane >= HEAD_DIM) if half else (lane < HEAD_DIM)
    if move_to is None or move_to == half:
        return jnp.where(keep, x, jnp.zeros_like(x))
    xf = jnp.where(keep, x.astype(F32), 0.0)
    return pltpu.roll(xf, HEAD_DIM, 1).astype(x.dtype)


def _attend_t(qp, pieces):
    scores = []
    for k, _, bias in pieces:
        s = _dot_nt(k, qp)
        scores.append(s if bias is None else s + bias)
    m = None
    for s in scores:
        mi = jnp.max(s, axis=0, keepdims=True)
        m = mi if m is None else jnp.maximum(m, mi)
    l = 0.0
    o = 0.0
    for s, (_, vt, _) in zip(scores, pieces):
        p = jnp.exp(s - m)
        l = l + jnp.sum(p, axis=0, keepdims=True)
        o = o + _dot(vt, p.astype(BF16))
    return o / l


def _select_rows(o_even, o_odd):
    return jnp.concatenate([o_even[:HEAD_DIM], o_odd[HEAD_DIM:]], axis=0)


def _ctx_attn_kernel(qa_ref, ka_ref, va_ref, qc_ref, kc_ref, vc_ref, oa_ref, oc_ref):
    ka = ka_ref[...].astype(BF16)
    vat = va_ref[...].T.astype(BF16)
    outs = []
    for j in range(NA_HEADS // 2):
        k = _pair(ka, j)
        vt = vat[j * LANES:(j + 1) * LANES]
        q = _pair(qa_ref, j)
        outs.append(_select_rows(_attend_t(_keep_half(q, 0), [(k, vt, None)]),
                                 _attend_t(_keep_half(q, 1), [(k, vt, None)])))
    oa_ref[...] = jnp.concatenate(outs, axis=0).T.astype(oa_ref.dtype)
    kc = kc_ref[...].astype(BF16)
    vct = vc_ref[...].T.astype(BF16)
    outs = []
    for h in range(GQ_HEADS):
        g = h // (GQ_HEADS // GQ_KV_HEADS)
        o = _attend_t(_keep_half(_pair(qc_ref, h // 2), h % 2, move_to=g), [(kc, vct, None)])
        outs.append(o[g * HEAD_DIM:(g + 1) * HEAD_DIM])
    oc_ref[...] = jnp.concatenate(outs, axis=0).T.astype(oc_ref.dtype)


def _ctx_attention(qa, ka, va, qc, kc, vc, *, seq_len):
    t = qa.shape[0]
    row = lambda i: (i, 0)
    spec = lambda w: pl.BlockSpec((seq_len, w), row)
    return pl.pallas_call(
        _ctx_attn_kernel,
        grid=(t // seq_len,),
        in_specs=[spec(NA_WIDTH), spec(NA_WIDTH), spec(NA_WIDTH), spec(GQ_Q_WIDTH), spec(GQ_KV_WIDTH), spec(GQ_KV_WIDTH)],
        out_specs=[spec(NA_WIDTH), spec(GQ_Q_WIDTH)],
        out_shape=[jax.ShapeDtypeStruct((t, NA_WIDTH), BF16), jax.ShapeDtypeStruct((t, GQ_Q_WIDTH), BF16)],
        compiler_params=_cparams(("arbitrary",), 32),
        name="attn_ctx",
    )(qa, ka, va, qc, kc, vc)


def _na_key_row_start(blk, rows):
    return jnp.clip(blk * NA_QROWS - NA_KH // 2, 0, rows - NA_WIN_ROWS)


def _na_lat_kernel(rows, q_ref, k_ref, vt_ref, kc_ref, vct_ref, bias_ref, o_ref):
    base = pl.multiple_of(_na_key_row_start(pl.program_id(1), rows) * GRID_W, 2 * LANES)
    nk = NA_WIN_ROWS * GRID_W
    kwin = k_ref[pl.ds(base, nk), :]
    outs = []
    for j in range(NA_HEADS // 2):
        k = _pair(kwin, j)
        vt = vt_ref[j * LANES:(j + 1) * LANES, pl.ds(base, nk)]
        kc = _pair(kc_ref, j)
        vct = vct_ref[j * LANES:(j + 1) * LANES, :]
        q = _pair(q_ref, j)
        halves = [_attend_t(_keep_half(q, half), [(k, vt, bias_ref[2 * j + half]), (kc, vct, None)])
                  for half in range(2)]
        outs.append(_select_rows(*halves))
    o_ref[...] = jnp.concatenate(outs, axis=0).T.astype(o_ref.dtype)


def _na_latent(q, k, vt, k_ctx, vt_ctx, bias, *, n_seq, seq_len):
    rows = seq_len // GRID_W
    nblk = rows // NA_QROWS
    tq = NA_QROWS * GRID_W
    pattern = lambda b, i: (jnp.where(i == 0, 0, jnp.where(i == nblk - 1, 2, 1)), 0, 0, 0)
    return pl.pallas_call(
        functools.partial(_na_lat_kernel, rows),
        grid=(n_seq, nblk),
        in_specs=[
            pl.BlockSpec((tq, NA_WIDTH), lambda b, i: (b * nblk + i, 0)),
            pl.BlockSpec((seq_len, NA_WIDTH), lambda b, i: (b, 0)),
            pl.BlockSpec((None, NA_WIDTH, seq_len), lambda b, i: (b, 0, 0)),
            pl.BlockSpec((None,) + k_ctx.shape[1:], lambda b, i: (b, 0, 0)),
            pl.BlockSpec((None,) + vt_ctx.shape[1:], lambda b, i: (b, 0, 0)),
            pl.BlockSpec((None,) + bias.shape[1:], pattern),
        ],
        out_specs=pl.BlockSpec((tq, NA_WIDTH), lambda b, i: (b * nblk + i, 0)),
        out_shape=jax.ShapeDtypeStruct(q.shape, BF16),
        compiler_params=_cparams(("arbitrary", "arbitrary"), 52),
        name="attn_na_lat",
    )(q, k, vt, k_ctx, vt_ctx, bias)


def _gqa_lat_kernel(q_ref, k_ref, v_ref, o_ref):
    k = k_ref[...]
    v = v_ref[...]
    for h in range(GQ_HEADS):
        g = h // (GQ_HEADS // GQ_KV_HEADS)
        s = _dot_nt(_keep_half(_pair(q_ref, h // 2), h % 2, move_to=g), k)
        p = jnp.exp(s - jnp.max(s, axis=-1, keepdims=True))
        o = _dot(p.astype(BF16), v) / jnp.sum(p, axis=-1, keepdims=True)
        o_ref[:, h * HEAD_DIM:(h + 1) * HEAD_DIM] = o[:, g * HEAD_DIM:(g + 1) * HEAD_DIM].astype(o_ref.dtype)


def _gqa_latent(q, k_all, v_all, *, n_seq, seq_len):
    per = seq_len // TQ_GQA
    s_all = k_all.shape[1]
    return pl.pallas_call(
        _gqa_lat_kernel,
        grid=(n_seq, per),
        in_specs=[
            pl.BlockSpec((TQ_GQA, GQ_Q_WIDTH), lambda b, t: (b * per + t, 0)),
            pl.BlockSpec((None, s_all, GQ_KV_WIDTH), lambda b, t: (b, 0, 0)),
            pl.BlockSpec((None, s_all, GQ_KV_WIDTH), lambda b, t: (b, 0, 0)),
        ],
        out_specs=pl.BlockSpec((TQ_GQA, GQ_Q_WIDTH), lambda b, t: (b * per + t, 0)),
        out_shape=jax.ShapeDtypeStruct(q.shape, BF16),
        compiler_params=_cparams(("arbitrary", "arbitrary"), 48),
        name="attn_gqa_lat",
    )(q, k_all, v_all)


def _conv_kernel(u_ref, up_ref, un_ref, w_ref, b_ref, g_ref, beta_ref, o_ref, slab_ref):
    t = pl.program_id(1)
    nt = pl.num_programs(1)
    tl = u_ref.shape[0]

    def glu(v):
        return v[:, :CONV_CH].astype(F32) * _sigmoid(v[:, CONV_CH:].astype(F32))

    slab_ref[0:HALO, :] = jnp.where(t > 0, glu(up_ref[...]), 0.0)
    slab_ref[HALO:HALO + tl, :] = glu(u_ref[...])
    slab_ref[HALO + tl:, :] = jnp.where(t < nt - 1, glu(un_ref[...]), 0.0)
    acc = jnp.zeros((tl, CONV_CH), F32)
    for k in range(CONV_K):
        start = HALO - CONV_K // 2 + k
        acc = acc + slab_ref[start:start + tl, :] * w_ref[k:k + 1, :]
    y = _layer_norm(acc + b_ref[...], g_ref[...], beta_ref[...])
    o_ref[...] = (y * _sigmoid(y)).astype(o_ref.dtype)


def _conformer_conv(u, conv_w, conv_b, ln_g, ln_b, *, n_seq, seq_len):
    tl = TL_CONV
    nt = seq_len // tl
    hb = tl // HALO
    last_hb = seq_len // HALO - 1
    u3 = u.reshape(n_seq, seq_len, 2 * CONV_CH)
    vec = lambda a: a.reshape(1, CONV_CH)
    out = pl.pallas_call(
        _conv_kernel,
        grid=(n_seq, nt),
        in_specs=[
            pl.BlockSpec((None, tl, 2 * CONV_CH), lambda s, t: (s, t, 0)),
            pl.BlockSpec((None, HALO, 2 * CONV_CH), lambda s, t: (s, jnp.maximum(t * hb - 1, 0), 0)),
            pl.BlockSpec((None, HALO, 2 * CONV_CH), lambda s, t: (s, jnp.minimum((t + 1) * hb, last_hb), 0)),
            _resident((CONV_K, CONV_CH)),
            _resident((1, CONV_CH)),
            _resident((1, CONV_CH)),
            _resident((1, CONV_CH)),
        ],
        out_specs=pl.BlockSpec((None, tl, CONV_CH), lambda s, t: (s, t, 0)),
        out_shape=jax.ShapeDtypeStruct((n_seq, seq_len, CONV_CH), BF16),
        scratch_shapes=[pltpu.VMEM((tl + 2 * HALO, CONV_CH), F32)],
        compiler_params=_cparams(("arbitrary", "arbitrary"), 32),
        name="conformer_conv",
    )(u3, u3, u3, conv_w, vec(conv_b), vec(ln_g), vec(ln_b))
    return out.reshape(n_seq * seq_len, CONV_CH)


def _router_gates(logits):
    lane = lax.broadcasted_iota(jnp.int32, logits.shape, 1)
    m = jnp.max(logits, axis=-1, keepdims=True)
    e = jnp.exp(logits - m)
    probs = e / jnp.sum(e, axis=-1, keepdims=True)
    p1 = jnp.max(probs, axis=-1, keepdims=True)
    i1 = jnp.min(jnp.where(probs == p1, lane, LANES), axis=-1, keepdims=True)
    in_group = (lane // EXPERTS_PER_GROUP) == (i1 // EXPERTS_PER_GROUP)
    cand = jnp.where(in_group, jnp.where(lane == i1, -1.0, probs), -1.0)
    p2 = jnp.max(cand, axis=-1, keepdims=True)
    i2 = jnp.min(jnp.where(cand == p2, lane, LANES), axis=-1, keepdims=True)
    den = p1 + p2
    gates = jnp.where(lane == i1, p1 / den, 0.0) + jnp.where(lane == i2, p2 / den, 0.0)
    group = (i1 // EXPERTS_PER_GROUP).astype(F32)
    return jnp.where(lane == GROUP_LANE, group, gates)


def _post_kernel(oa_ref, ob_ref, oc_ref, gt_ref, x_ref, g1_ref, lg_ref, lb_ref, sc2_ref, sh2_ref,
                 wa_ref, wb_ref, wc_ref, wo_ref, wr_ref, br_ref, x1_ref, hg_ref):
    sg = _sigmoid(gt_ref[...].astype(F32))
    merged = (sg[:, :D_MODEL] * _dot(oa_ref[...], wa_ref[...])
              + sg[:, D_MODEL:2 * D_MODEL] * _dot(ob_ref[...], wb_ref[...])
              + sg[:, 2 * D_MODEL:] * _dot(oc_ref[...], wc_ref[...]))
    y = _dot(merged.astype(BF16), wo_ref[...])
    x1 = _layer_norm(ALPHA * x_ref[...] + g1_ref[...] * y, lg_ref[...], lb_ref[...])
    x1_ref[...] = x1
    h2 = x1 * sc2_ref[...] + sh2_ref[...]
    hg_ref[:, :D_MODEL] = h2
    hg_ref[:, D_MODEL:] = _router_gates(_dot3(h2, wr_ref[...]) + br_ref[...])


def _post(oa, ob, oc, gates, x, g1, ln_g, ln_b, scale2, shift2, wa, wb, wc, wo, wr, br, *, latent, seq_len):
    t = x.shape[0]
    tm = TM_POST
    row = lambda i: (i, 0)
    mod_map = _mod_row_map(latent, seq_len // tm)
    mod = pl.BlockSpec((None, 1, D_MODEL), mod_map)
    return pl.pallas_call(
        _post_kernel,
        grid=(t // tm,),
        in_specs=[
            pl.BlockSpec((tm, NA_WIDTH), row), pl.BlockSpec((tm, CONV_CH), row), pl.BlockSpec((tm, GQ_Q_WIDTH), row),
            pl.BlockSpec((tm, 3 * D_MODEL), row), pl.BlockSpec((tm, D_MODEL), row),
            mod, _resident((1, D_MODEL)), _resident((1, D_MODEL)), mod, mod,
            _resident(wa.shape), _resident(wb.shape), _resident(wc.shape), _resident(wo.shape),
            _resident(wr.shape), _resident(br.shape),
        ],
        out_specs=[pl.BlockSpec((tm, D_MODEL), row), pl.BlockSpec((tm, MOE_ROW), row)],
        out_shape=[jax.ShapeDtypeStruct((t, D_MODEL), F32), jax.ShapeDtypeStruct((t, MOE_ROW), F32)],
        compiler_params=_cparams(("arbitrary",), 40),
        name="post_lat" if latent else "post_ctx",
    )(oa, ob, oc, gates, x, g1, ln_g, ln_b, scale2, shift2, wa, wb, wc, wo, wr, br)


def _route(hg, tm):
    n = hg.shape[0]
    n_tiles = n // tm + N_GROUPS
    grp = hg[:, D_MODEL + GROUP_LANE].astype(jnp.int32)
    onehot = (grp[:, None] == jnp.arange(N_GROUPS, dtype=jnp.int32)[None, :]).astype(jnp.int32)
    csum = jnp.cumsum(onehot, axis=0)
    rank = jnp.sum((csum - onehot) * onehot, axis=1)
    tiles = (csum[-1] + tm - 1) // tm
    tile_end = jnp.cumsum(tiles)
    slot = jnp.sum(onehot * (tile_end - tiles)[None, :], axis=1) * tm + rank
    tile_ids = jnp.arange(n_tiles, dtype=jnp.int32)
    tile_group = jnp.minimum(jnp.sum((tile_ids[:, None] >= tile_end[None, :]).astype(jnp.int32), axis=1), N_GROUPS - 1)
    r = jnp.arange(n_tiles * tm, dtype=jnp.int32)
    tok = jnp.arange(n, dtype=jnp.int32)
    src = jnp.zeros((n_tiles * tm,), jnp.int32).at[slot].set(tok)
    dst = (n + ((r // tm) % 2) * tm + r % tm).at[slot].set(tok)
    return tile_group.astype(jnp.int32), src, dst, tile_end[-1:].astype(jnp.int32)


def _moe_kernel(tg_ref, src_ref, dst_ref, nused_ref, h_hbm, wgu_ref, wd_ref, y_hbm, hbuf, ybuf, gsem, ssem):
    t = pl.program_id(0)
    n_used = nused_ref[0]
    slot = t % 2
    tm = hbuf.shape[1]

    def gather_row(tile, s, r):
        tok = src_ref[tile * tm + r]
        return pltpu.make_async_copy(h_hbm.at[pl.ds(tok, 1), :], hbuf.at[s, pl.ds(r, 1), :], gsem.at[s])

    def scatter_row(tile, s, r):
        row = dst_ref[tile * tm + r]
        return pltpu.make_async_copy(ybuf.at[s, pl.ds(r, 1), :], y_hbm.at[pl.ds(row, 1), :], ssem.at[s])

    def for_rows(fn):
        def body(r, carry):
            fn(r)
            return carry
        lax.fori_loop(0, tm, body, 0, unroll=8)

    @pl.when(t == 0)
    def _():
        for_rows(lambda r: gather_row(0, 0, r).start())

    @pl.when(t + 1 < n_used)
    def _():
        for_rows(lambda r: gather_row(t + 1, 1 - slot, r).start())

    @pl.when(t < n_used)
    def _():
        for_rows(lambda r: gather_row(t, slot, r).wait())

        @pl.when(t >= 2)
        def _():
            for_rows(lambda r: scatter_row(t - 2, slot, r).wait())

        first = tg_ref[t] * EXPERTS_PER_GROUP
        x = hbuf[slot]
        h = x[:, :D_MODEL].astype(BF16)
        gates = x[:, D_MODEL:]
        lane = lax.broadcasted_iota(jnp.int32, gates.shape, 1)
        acc = jnp.zeros((tm, D_MODEL), F32)
        for e in range(EXPERTS_PER_GROUP):
            gu = _dot(h, wgu_ref[e])
            a = gu[:, :D_EXPERT]
            ge = jnp.sum(jnp.where(lane == first + e, gates, 0.0), axis=-1, keepdims=True)
            hid = a * _sigmoid(a) * gu[:, D_EXPERT:] * ge
            acc = acc + _dot(hid.astype(BF16), wd_ref[e])
        ybuf[slot] = acc
        for_rows(lambda r: scatter_row(t, slot, r).start())

        @pl.when(t == n_used - 1)
        def _():
            for_rows(lambda r: scatter_row(t, slot, r).wait())

            @pl.when(t >= 1)
            def _():
                for_rows(lambda r: scatter_row(t - 1, 1 - slot, r).wait())

            n = y_hbm.shape[0] - 2 * tm
            ybuf[0] = jnp.zeros((tm, D_MODEL), F32)
            for half in range(2):
                fill = pltpu.make_async_copy(ybuf.at[0], y_hbm.at[pl.ds(n + half * tm, tm), :], ssem.at[half])
                fill.start()
                fill.wait()


def _moe_experts(hg, route, wgu, wd):
    n = hg.shape[0]
    tm = TM_MOE
    tile_group, src, dst, n_used = route
    n_tiles = tile_group.shape[0]
    grid_spec = pltpu.PrefetchScalarGridSpec(
        num_scalar_prefetch=4,
        grid=(n_tiles,),
        in_specs=[
            pl.BlockSpec(memory_space=pl.ANY),
            pl.BlockSpec((EXPERTS_PER_GROUP, D_MODEL, 2 * D_EXPERT), lambda t, tg, s, d, nu: (tg[t], 0, 0)),
            pl.BlockSpec((EXPERTS_PER_GROUP, D_EXPERT, D_MODEL), lambda t, tg, s, d, nu: (tg[t], 0, 0)),
        ],
        out_specs=pl.BlockSpec(memory_space=pl.ANY),
        scratch_shapes=[
            pltpu.VMEM((2, tm, MOE_ROW), F32),
            pltpu.VMEM((2, tm, D_MODEL), F32),
            pltpu.SemaphoreType.DMA((2,)),
            pltpu.SemaphoreType.DMA((2,)),
        ],
    )
    return pl.pallas_call(
        _moe_kernel,
        grid_spec=grid_spec,
        out_shape=jax.ShapeDtypeStruct((n + 2 * tm, D_MODEL), F32),
        compiler_params=_cparams(("arbitrary",), 56),
        name="moe_experts",
    )(tile_group, src, dst, n_used, hg, wgu, wd)


def _moe_out_kernel(x1_ref, y_ref, g2_ref, lg_ref, lb_ref, o_ref):
    o_ref[...] = _layer_norm(ALPHA * x1_ref[...] + g2_ref[...] * y_ref[...], lg_ref[...], lb_ref[...])


def _moe_out(x1, y, g2, ln_g, ln_b, *, latent, seq_len):
    t = x1.shape[0]
    tm = TM_POST
    row = lambda i: (i, 0)
    return pl.pallas_call(
        _moe_out_kernel,
        grid=(t // tm,),
        in_specs=[pl.BlockSpec((tm, D_MODEL), row), pl.BlockSpec((tm, D_MODEL), row),
                  pl.BlockSpec((None, 1, D_MODEL), _mod_row_map(latent, seq_len // tm)),
                  _resident((1, D_MODEL)), _resident((1, D_MODEL))],
        out_specs=pl.BlockSpec((tm, D_MODEL), row),
        out_shape=jax.ShapeDtypeStruct((t, D_MODEL), F32),
        compiler_params=_cparams(("arbitrary",), 32),
        name="moe_out",
    )(x1, y, g2, ln_g, ln_b)


def _moe(hg, x1, g2, ln_g, ln_b, wgu, wd, *, latent, seq_len):
    y = _moe_experts(hg, _route(hg, TM_MOE), wgu, wd)
    return _moe_out(x1, y, g2, ln_g, ln_b, latent=latent, seq_len=seq_len)


def _head_mean_matrix():
    i = jnp.arange(GQ_Q_WIDTH) // HEAD_DIM
    return jnp.where(i[:, None] == i[None, :], 1.0 / HEAD_DIM, 0.0).astype(BF16)


def _rope_tables(n_tokens):
    t = jnp.arange(n_tokens, dtype=jnp.int32)
    axis_dim = HEAD_DIM // 2
    inv_freq = ROPE_THETA ** (-jnp.arange(0, axis_dim, 2, dtype=F32) / axis_dim)
    ar = (t // GRID_W).astype(F32)[:, None] * inv_freq
    ac = (t % GRID_W).astype(F32)[:, None] * inv_freq
    c = jnp.concatenate([jnp.cos(ar), jnp.cos(ar), jnp.cos(ac), jnp.cos(ac)], axis=1)
    s = jnp.concatenate([-jnp.sin(ar), jnp.sin(ar), -jnp.sin(ac), jnp.sin(ac)], axis=1)
    return jnp.tile(c, (1, LANES // HEAD_DIM)), jnp.tile(s, (1, LANES // HEAD_DIM))


def _na_bias_table(rpb, rows):
    col = np.arange(GRID_W)
    col_idx = np.clip(col[:, None] - col[None, :], -(NA_KW - 1), NA_KW - 1) + NA_KW - 1
    onehot = (col_idx[None] == np.arange(2 * NA_KW - 1)[:, None, None]).astype(np.float32)
    col_start = np.clip(col - NA_KW // 2, 0, GRID_W - NA_KW)
    col_ok = (col[:, None] >= col_start[None, :]) & (col[:, None] < col_start[None, :] + NA_KW)
    t = jnp.einsum("hic,ckq->hikq", rpb, onehot, precision=lax.Precision.HIGHEST)
    t = jnp.where(col_ok[None, None], t, NEG_INF)
    masked = jnp.full((NA_HEADS, GRID_W, GRID_W), NEG_INF, F32)
    pats = []
    for first_qrow in (0, NA_KH // 2, rows - NA_QROWS):
        krow0 = int(np.clip(first_qrow - NA_KH // 2, 0, rows - NA_WIN_ROWS))
        krows = []
        for kj in range(NA_WIN_ROWS):
            blocks = []
            for qi in range(NA_QROWS):
                qrow, krow = first_qrow + qi, krow0 + kj
                win0 = int(np.clip(qrow - NA_KH // 2, 0, rows - NA_KH))
                in_window = win0 <= krow < win0 + NA_KH
                blocks.append(t[:, krow - qrow + NA_KH - 1] if in_window else masked)
            krows.append(jnp.concatenate(blocks, axis=-1))
        pats.append(jnp.concatenate(krows, axis=-2))
    return jnp.stack(pats)


def _stream_layer(x, mods, l, wts, *, latent, n_seq, seq_len, cache=None, rope=None):
    sh1, sc1, g1, sh2, sc2, g2 = mods
    qa, ka, va, u, qc, kc, vc, gates = _proj(
        x, 1.0 + sc1, sh1, wts["w_in"][l], wts["gmat"], wts["qn"][l], wts["kn"][l], rope,
        latent=latent, seq_len=seq_len)
    if latent:
        nat_k, nat_v, gqa_k, gqa_v = cache
        swap = lambda a: jnp.swapaxes(a, 1, 2)
        out_a = _na_latent(qa, ka, swap(va.reshape(n_seq, seq_len, NA_WIDTH)), nat_k, swap(nat_v),
                           wts["na_bias"][l], n_seq=n_seq, seq_len=seq_len)
        k_all = jnp.concatenate([gqa_k, kc.reshape(n_seq, seq_len, GQ_KV_WIDTH)], axis=1)
        v_all = jnp.concatenate([gqa_v, vc.reshape(n_seq, seq_len, GQ_KV_WIDTH)], axis=1)
        out_c = _gqa_latent(qc, k_all, v_all, n_seq=n_seq, seq_len=seq_len)
    else:
        out_a, out_c = _ctx_attention(qa, ka, va, qc, kc, vc, seq_len=seq_len)
    out_b = _conformer_conv(u, wts["conv_w"][l], wts["conv_b"][l], wts["conv_ln_g"][l], wts["conv_ln_b"][l],
                            n_seq=n_seq, seq_len=seq_len)
    x1, hg = _post(out_a, out_b, out_c, gates, x, g1, wts["ln1_g"][l], wts["ln1_b"][l], 1.0 + sc2, sh2,
                         wts["w_br_a"][l], wts["w_br_b"][l], wts["w_br_c"][l], wts["w_out"][l],
                         wts["w_router"], wts["b_router"], latent=latent, seq_len=seq_len)
    x2 = _moe(hg, x1, g2, wts["ln2_g"][l], wts["ln2_b"][l], wts["w_gate_up"][l], wts["w_down"][l],
              latent=latent, seq_len=seq_len)
    return x2, (ka, va, kc, vc)


def kernel(x_prompt, x_sample, cache_nat_k, cache_nat_v, cache_gqa_k, cache_gqa_v, c, c_ctx, w_mod, b_mod, w_in, nat_rpb, conv_w, conv_b, conv_ln_g, conv_ln_b, q_norm_g, k_norm_g, w_br_a, w_br_b, w_br_c, w_out, ln1_g, ln1_b, ln2_g, ln2_b, w_router, b_router, w_gate_up, w_down):
    batch, seq, _ = x_prompt.shape
    dec_batch, dec_seq, _ = x_sample.shape
    past = cache_nat_k.shape[2]
    assert dec_batch == 2 and seq % TM_PROJ == 0 and dec_seq % TM_MOE == 0

    cond8 = jnp.zeros((8, D_MODEL), F32).at[0].set(c_ctx).at[1:1 + dec_batch].set(c)
    mods_all = _modulation(cond8, w_mod, b_mod)

    pad = LANES - N_EXPERTS
    vecd = lambda a: a.reshape(DEPTH, 1, -1)
    wts = dict(
        w_in=w_in.astype(BF16),
        gmat=_head_mean_matrix(),
        qn=jnp.tile(q_norm_g, (1, GQ_HEADS)).reshape(DEPTH, 1, GQ_Q_WIDTH),
        kn=jnp.tile(k_norm_g, (1, GQ_KV_HEADS)).reshape(DEPTH, 1, GQ_KV_WIDTH),
        na_bias=jnp.stack([_na_bias_table(nat_rpb[l], dec_seq // GRID_W) for l in range(DEPTH)]),
        conv_w=conv_w, conv_b=conv_b, conv_ln_g=conv_ln_g, conv_ln_b=conv_ln_b,
        ln1_g=vecd(ln1_g), ln1_b=vecd(ln1_b), ln2_g=vecd(ln2_g), ln2_b=vecd(ln2_b),
        w_br_a=w_br_a.astype(BF16), w_br_b=w_br_b.astype(BF16), w_br_c=w_br_c.astype(BF16),
        w_out=w_out.astype(BF16),
        w_router=jnp.pad(w_router, ((0, 0), (0, pad))),
        b_router=jnp.pad(b_router, (0, pad), constant_values=NEG_INF).reshape(1, LANES),
        w_gate_up=w_gate_up.astype(BF16), w_down=w_down.astype(BF16),
    )
    rope = _rope_tables(dec_seq)

    def layer_mods(l):
        m = mods_all[l, :1 + dec_batch].reshape(1 + dec_batch, 6, 1, D_MODEL)
        return [m[:, i] for i in range(6)]

    xp = x_prompt.reshape(batch * seq, D_MODEL)
    xs = x_sample.reshape(dec_batch * dec_seq, D_MODEL)
    new_cache = []
    for l in range(DEPTH):
        mods = layer_mods(l)
        xp, ctx_kv = _stream_layer(xp, mods, l, wts, latent=False, n_seq=batch, seq_len=seq)
        new_cache.append(ctx_kv)
        cache = (cache_nat_k[:, l].reshape(dec_batch, past, NA_WIDTH).astype(BF16),
                 cache_nat_v[:, l].reshape(dec_batch, past, NA_WIDTH).astype(BF16),
                 cache_gqa_k[:, l].reshape(dec_batch, past, GQ_KV_WIDTH).astype(BF16),
                 cache_gqa_v[:, l].reshape(dec_batch, past, GQ_KV_WIDTH).astype(BF16))
        xs, _ = _stream_layer(xs, mods, l, wts, latent=True, n_seq=dec_batch, seq_len=dec_seq, cache=cache, rope=rope)

    def stack(i, heads):
        return jnp.stack([kv[i].reshape(batch, seq, heads, HEAD_DIM) for kv in new_cache], axis=1)

    return (xp.reshape(batch, seq, D_MODEL), xs.reshape(dec_batch, dec_seq, D_MODEL),
            stack(0, NA_HEADS), stack(1, NA_HEADS), stack(2, GQ_KV_HEADS), stack(3, GQ_KV_HEADS))
```

```python
import functools

import numpy as np
import jax
import jax.numpy as jnp
from jax import lax
from jax.experimental import pallas as pl
from jax.experimental.pallas import tpu as pltpu

F32 = jnp.float32
BF16 = jnp.bfloat16

D_MODEL = 1024
DEPTH = 2
GRID_W = 64
HEAD_DIM = 64
NA_HEADS = 8
NA_WIDTH = NA_HEADS * HEAD_DIM
NA_KH = 8
NA_KW = 16
NA_QROWS = 4
NA_WIN_ROWS = NA_QROWS + NA_KH
CONV_CH = 512
CONV_K = 31
GQ_HEADS = 8
GQ_KV_HEADS = 2
GQ_Q_WIDTH = GQ_HEADS * HEAD_DIM
GQ_KV_WIDTH = GQ_KV_HEADS * HEAD_DIM
ROPE_THETA = 10000.0
N_EXPERTS = 16
EXPERTS_PER_GROUP = 4
N_GROUPS = N_EXPERTS // EXPERTS_PER_GROUP
GROUP_LANE = N_EXPERTS
Y_ROWS = D_MODEL // 128
HG_ROWS = 2 * Y_ROWS
D_EXPERT = 512
ALPHA = (2 * DEPTH) ** 0.25
LN_EPS = 1e-6
RMS_EPS = 1e-6
NEG_INF = -1e30
QK_SCALE = HEAD_DIM ** -0.5

_OFF = (0, 512, 1024, 1536, 2560, 3072, 3200, 3328, 6400)
IN_COLS = _OFF[-1]

LANES = 128
V7X_VMEM_BYTES = 64 * 2 ** 20
HALO = 16

TM_PROJ = 256
TM_POST = 256
TM_MOE = 512
TL_CONV = 256
TQ_GQA = 256


def _cparams(sem, vmem_mib):
    assert vmem_mib * 2 ** 20 < V7X_VMEM_BYTES
    return pltpu.CompilerParams(dimension_semantics=sem, vmem_limit_bytes=vmem_mib * 2 ** 20)


def _resident(shape):
    nd = len(shape)
    return pl.BlockSpec(shape, lambda *_: (0,) * nd, pipeline_mode=pl.Buffered(1))


def _dot(a, b):
    return jnp.dot(a, b, preferred_element_type=F32)


def _dot_nt(a, b):
    return lax.dot_general(a, b, (((1,), (1,)), ((), ())), preferred_element_type=F32)


def _split_bf16(a):
    hi = a.astype(BF16)
    lo = (a - hi.astype(F32)).astype(BF16)
    return hi, lo


def _dot3(a, b):
    a_hi, a_lo = _split_bf16(a)
    b_hi, b_lo = _split_bf16(b)
    return _dot(a_hi, b_hi) + _dot(a_lo, b_hi) + _dot(a_hi, b_lo)


def _sigmoid(x):
    return 1.0 / (1.0 + jnp.exp(-x))


def _layer_norm(x, g, b):
    mu = jnp.mean(x, axis=-1, keepdims=True)
    xc = x - mu
    var = jnp.mean(xc * xc, axis=-1, keepdims=True)
    return xc * lax.rsqrt(var + LN_EPS) * g + b


def _mod_kernel(cond_ref, w_ref, b_ref, o_ref):
    c = cond_ref[...]
    o_ref[...] = _dot3(c * _sigmoid(c), w_ref[...]) + b_ref[...]


def _modulation(cond8, w_mod, b_mod):
    tn = 1536
    n = w_mod.shape[-1]
    return pl.pallas_call(
        _mod_kernel,
        grid=(DEPTH, n // tn),
        in_specs=[
            pl.BlockSpec((8, D_MODEL), lambda l, j: (0, 0)),
            pl.BlockSpec((None, D_MODEL, tn), lambda l, j: (l, 0, j)),
            pl.BlockSpec((None, 1, tn), lambda l, j: (l, 0, j)),
        ],
        out_specs=pl.BlockSpec((None, 8, tn), lambda l, j: (l, 0, j)),
        out_shape=jax.ShapeDtypeStruct((DEPTH, 8, n), F32),
        compiler_params=_cparams(("arbitrary", "arbitrary"), 40),
        name="modulation",
    )(cond8, w_mod, b_mod.reshape(DEPTH, 1, n))


def _head_rms(x, gmat, gain):
    hi, lo = _split_bf16(x * x)
    ms = _dot(hi, gmat) + _dot(lo, gmat)
    return x * lax.rsqrt(ms + RMS_EPS) * gain


def _rope(x, c, s):
    w = x.shape[1]
    reps = w // LANES
    if reps > 1:
        c = jnp.concatenate([c] * reps, axis=1)
        s = jnp.concatenate([s] * reps, axis=1)
    lane = lax.broadcasted_iota(jnp.int32, x.shape, 1)
    partner = jnp.where((lane % 32) < 16, pltpu.roll(x, w - 16, 1), pltpu.roll(x, 16, 1))
    return x * c + partner * s


def _proj_kernel(latent, *refs):
    if latent:
        (x_ref, sc_ref, sh_ref, w_ref, gm_ref, qn_ref, kn_ref, c_ref, s_ref,
         qa_ref, ka_ref, va_ref, u_ref, qc_ref, kc_ref, vc_ref, gt_ref) = refs
    else:
        (x_ref, sc_ref, sh_ref, w_ref, gm_ref, qn_ref, kn_ref,
         qa_ref, ka_ref, va_ref, u_ref, qc_ref, kc_ref, vc_ref, gt_ref) = refs
    h = (x_ref[...] * sc_ref[...] + sh_ref[...]).astype(BF16)

    def mm(i):
        return _dot(h, w_ref[:, _OFF[i]:_OFF[i + 1]])

    qa_ref[...] = (mm(0) * QK_SCALE).astype(qa_ref.dtype)
    ka_ref[...] = mm(1).astype(ka_ref.dtype)
    va_ref[...] = mm(2).astype(va_ref.dtype)
    u_ref[...] = mm(3).astype(u_ref.dtype)
    qc = _head_rms(mm(4), gm_ref[...], qn_ref[...])
    kc = _head_rms(mm(5), gm_ref[:GQ_KV_WIDTH, :GQ_KV_WIDTH], kn_ref[...])
    if latent:
        qc = _rope(qc, c_ref[...], s_ref[...])
        kc = _rope(kc, c_ref[...], s_ref[...])
    qc_ref[...] = (qc * QK_SCALE).astype(qc_ref.dtype)
    kc_ref[...] = kc.astype(kc_ref.dtype)
    vc_ref[...] = mm(6).astype(vc_ref.dtype)
    gt_ref[...] = mm(7).astype(gt_ref.dtype)


def _mod_row_map(latent, tiles_per_seq):
    if latent:
        return lambda i: (1 + i // tiles_per_seq, 0, 0)
    return lambda i: (0, 0, 0)


def _proj(x, scale, shift, w_bf, gmat, qn, kn, rope, *, latent, seq_len):
    t = x.shape[0]
    tm = TM_PROJ
    kv_dt = BF16 if latent else F32
    row = lambda i: (i, 0)
    mod_map = _mod_row_map(latent, seq_len // tm)
    in_specs = [
        pl.BlockSpec((tm, D_MODEL), row),
        pl.BlockSpec((None, 1, D_MODEL), mod_map),
        pl.BlockSpec((None, 1, D_MODEL), mod_map),
        _resident((D_MODEL, IN_COLS)),
        _resident((GQ_Q_WIDTH, GQ_Q_WIDTH)),
        _resident((1, GQ_Q_WIDTH)),
        _resident((1, GQ_KV_WIDTH)),
    ]
    args = [x, scale, shift, w_bf, gmat, qn, kn]
    if latent:
        per = seq_len // tm
        in_specs += [pl.BlockSpec((tm, LANES), lambda i: (i % per, 0))] * 2
        args += list(rope)
    widths = (NA_WIDTH, NA_WIDTH, NA_WIDTH, 2 * CONV_CH, GQ_Q_WIDTH, GQ_KV_WIDTH, GQ_KV_WIDTH, 3 * D_MODEL)
    dtypes = (BF16, kv_dt, kv_dt, BF16, BF16, kv_dt, kv_dt, BF16)
    return pl.pallas_call(
        functools.partial(_proj_kernel, latent),
        grid=(t // tm,),
        in_specs=in_specs,
        out_specs=[pl.BlockSpec((tm, w), row) for w in widths],
        out_shape=[jax.ShapeDtypeStruct((t, w), dt) for w, dt in zip(widths, dtypes)],
        compiler_params=_cparams(("arbitrary",), 48),
        name="proj_lat" if latent else "proj_ctx",
    )(*args)


def _pair(ref_or_val, j):
    return ref_or_val[:, j * LANES:(j + 1) * LANES]


def _keep_half(x, half, move_to=None):
    lane = lax.broadcasted_iota(jnp.int32, x.shape, 1)
    keep = (lane >= HEAD_DIM) if half else (lane < HEAD_DIM)
    if move_to is None or move_to == half:
        return jnp.where(keep, x, jnp.zeros_like(x))
    xf = jnp.where(keep, x.astype(F32), 0.0)
    return pltpu.roll(xf, HEAD_DIM, 1).astype(x.dtype)


def _attend_t(qp, pieces):
    scores = []
    for k, _, bias in pieces:
        s = _dot_nt(k, qp)
        scores.append(s if bias is None else s + bias)
    m = None
    for s in scores:
        mi = jnp.max(s, axis=0, keepdims=True)
        m = mi if m is None else jnp.maximum(m, mi)
    l = 0.0
    o = 0.0
    for s, (_, vt, _) in zip(scores, pieces):
        p = jnp.exp(s - m)
        l = l + jnp.sum(p, axis=0, keepdims=True)
        o = o + _dot(vt, p.astype(BF16))
    return o / l


def _select_rows(o_even, o_odd):
    return jnp.concatenate([o_even[:HEAD_DIM], o_odd[HEAD_DIM:]], axis=0)


def _ctx_attn_kernel(qa_ref, ka_ref, va_ref, qc_ref, kc_ref, vc_ref, oa_ref, oc_ref):
    ka = ka_ref[...].astype(BF16)
    vat = va_ref[...].T.astype(BF16)
    outs = []
    for j in range(NA_HEADS // 2):
        k = _pair(ka, j)
        vt = vat[j * LANES:(j + 1) * LANES]
        q = _pair(qa_ref, j)
        outs.append(_select_rows(_attend_t(_keep_half(q, 0), [(k, vt, None)]),
                                 _attend_t(_keep_half(q, 1), [(k, vt, None)])))
    oa_ref[...] = jnp.concatenate(outs, axis=0).T.astype(oa_ref.dtype)
    kc = kc_ref[...].astype(BF16)
    vct = vc_ref[...].T.astype(BF16)
    outs = []
    for h in range(GQ_HEADS):
        g = h // (GQ_HEADS // GQ_KV_HEADS)
        o = _attend_t(_keep_half(_pair(qc_ref, h // 2), h % 2, move_to=g), [(kc, vct, None)])
        outs.append(o[g * HEAD_DIM:(g + 1) * HEAD_DIM])
    oc_ref[...] = jnp.concatenate(outs, axis=0).T.astype(oc_ref.dtype)


def _ctx_attention(qa, ka, va, qc, kc, vc, *, seq_len):
    t = qa.shape[0]
    row = lambda i: (i, 0)
    spec = lambda w: pl.BlockSpec((seq_len, w), row)
    return pl.pallas_call(
        _ctx_attn_kernel,
        grid=(t // seq_len,),
        in_specs=[spec(NA_WIDTH), spec(NA_WIDTH), spec(NA_WIDTH), spec(GQ_Q_WIDTH), spec(GQ_KV_WIDTH), spec(GQ_KV_WIDTH)],
        out_specs=[spec(NA_WIDTH), spec(GQ_Q_WIDTH)],
        out_shape=[jax.ShapeDtypeStruct((t, NA_WIDTH), BF16), jax.ShapeDtypeStruct((t, GQ_Q_WIDTH), BF16)],
        compiler_params=_cparams(("arbitrary",), 32),
        name="attn_ctx",
    )(qa, ka, va, qc, kc, vc)


def _na_key_row_start(blk, rows):
    return jnp.clip(blk * NA_QROWS - NA_KH // 2, 0, rows - NA_WIN_ROWS)


def _na_lat_kernel(rows, q_ref, k_ref, vt_ref, kc_ref, vct_ref, bias_ref, o_ref):
    base = pl.multiple_of(_na_key_row_start(pl.program_id(1), rows) * GRID_W, 2 * LANES)
    nk = NA_WIN_ROWS * GRID_W
    kwin = k_ref[pl.ds(base, nk), :]
    outs = []
    for j in range(NA_HEADS // 2):
        k = _pair(kwin, j)
        vt = vt_ref[j * LANES:(j + 1) * LANES, pl.ds(base, nk)]
        kc = _pair(kc_ref, j)
        vct = vct_ref[j * LANES:(j + 1) * LANES, :]
        q = _pair(q_ref, j)
        halves = [_attend_t(_keep_half(q, half), [(k, vt, bias_ref[2 * j + half]), (kc, vct, None)])
                  for half in range(2)]
        outs.append(_select_rows(*halves))
    o_ref[...] = jnp.concatenate(outs, axis=0).T.astype(o_ref.dtype)


def _na_latent(q, k, vt, k_ctx, vt_ctx, bias, *, n_seq, seq_len):
    rows = seq_len // GRID_W
    nblk = rows // NA_QROWS
    tq = NA_QROWS * GRID_W
    pattern = lambda b, i: (jnp.where(i == 0, 0, jnp.where(i == nblk - 1, 2, 1)), 0, 0, 0)
    return pl.pallas_call(
        functools.partial(_na_lat_kernel, rows),
        grid=(n_seq, nblk),
        in_specs=[
            pl.BlockSpec((tq, NA_WIDTH), lambda b, i: (b * nblk + i, 0)),
            pl.BlockSpec((seq_len, NA_WIDTH), lambda b, i: (b, 0)),
            pl.BlockSpec((None, NA_WIDTH, seq_len), lambda b, i: (b, 0, 0)),
            pl.BlockSpec((None,) + k_ctx.shape[1:], lambda b, i: (b, 0, 0)),
            pl.BlockSpec((None,) + vt_ctx.shape[1:], lambda b, i: (b, 0, 0)),
            pl.BlockSpec((None,) + bias.shape[1:], pattern),
        ],
        out_specs=pl.BlockSpec((tq, NA_WIDTH), lambda b, i: (b * nblk + i, 0)),
        out_shape=jax.ShapeDtypeStruct(q.shape, BF16),
        compiler_params=_cparams(("arbitrary", "arbitrary"), 52),
        name="attn_na_lat",
    )(q, k, vt, k_ctx, vt_ctx, bias)


def _gqa_lat_kernel(q_ref, k_ref, v_ref, o_ref):
    k = k_ref[...]
    v = v_ref[...]
    for h in range(GQ_HEADS):
        g = h // (GQ_HEADS // GQ_KV_HEADS)
        s = _dot_nt(_keep_half(_pair(q_ref, h // 2), h % 2, move_to=g), k)
        p = jnp.exp(s - jnp.max(s, axis=-1, keepdims=True))
        o = _dot(p.astype(BF16), v) / jnp.sum(p, axis=-1, keepdims=True)
        o_ref[:, h * HEAD_DIM:(h + 1) * HEAD_DIM] = o[:, g * HEAD_DIM:(g + 1) * HEAD_DIM].astype(o_ref.dtype)


def _gqa_latent(q, k_all, v_all, *, n_seq, seq_len):
    per = seq_len // TQ_GQA
    s_all = k_all.shape[1]
    return pl.pallas_call(
        _gqa_lat_kernel,
        grid=(n_seq, per),
        in_specs=[
            pl.BlockSpec((TQ_GQA, GQ_Q_WIDTH), lambda b, t: (b * per + t, 0)),
            pl.BlockSpec((None, s_all, GQ_KV_WIDTH), lambda b, t: (b, 0, 0)),
            pl.BlockSpec((None, s_all, GQ_KV_WIDTH), lambda b, t: (b, 0, 0)),
        ],
        out_specs=pl.BlockSpec((TQ_GQA, GQ_Q_WIDTH), lambda b, t: (b * per + t, 0)),
        out_shape=jax.ShapeDtypeStruct(q.shape, BF16),
        compiler_params=_cparams(("arbitrary", "arbitrary"), 48),
        name="attn_gqa_lat",
    )(q, k_all, v_all)


def _conv_kernel(u_ref, up_ref, un_ref, w_ref, b_ref, g_ref, beta_ref, o_ref, slab_ref):
    t = pl.program_id(1)
    nt = pl.num_programs(1)
    tl = u_ref.shape[0]

    def glu(v):
        return v[:, :CONV_CH].astype(F32) * _sigmoid(v[:, CONV_CH:].astype(F32))

    slab_ref[0:HALO, :] = jnp.where(t > 0, glu(up_ref[...]), 0.0)
    slab_ref[HALO:HALO + tl, :] = glu(u_ref[...])
    slab_ref[HALO + tl:, :] = jnp.where(t < nt - 1, glu(un_ref[...]), 0.0)
    acc = jnp.zeros((tl, CONV_CH), F32)
    for k in range(CONV_K):
        start = HALO - CONV_K // 2 + k
        acc = acc + slab_ref[start:start + tl, :] * w_ref[k:k + 1, :]
    y = _layer_norm(acc + b_ref[...], g_ref[...], beta_ref[...])
    o_ref[...] = (y * _sigmoid(y)).astype(o_ref.dtype)


def _conformer_conv(u, conv_w, conv_b, ln_g, ln_b, *, n_seq, seq_len):
    tl = TL_CONV
    nt = seq_len // tl
    hb = tl // HALO
    last_hb = seq_len // HALO - 1
    u3 = u.reshape(n_seq, seq_len, 2 * CONV_CH)
    vec = lambda a: a.reshape(1, CONV_CH)
    out = pl.pallas_call(
        _conv_kernel,
        grid=(n_seq, nt),
        in_specs=[
            pl.BlockSpec((None, tl, 2 * CONV_CH), lambda s, t: (s, t, 0)),
            pl.BlockSpec((None, HALO, 2 * CONV_CH), lambda s, t: (s, jnp.maximum(t * hb - 1, 0), 0)),
            pl.BlockSpec((None, HALO, 2 * CONV_CH), lambda s, t: (s, jnp.minimum((t + 1) * hb, last_hb), 0)),
            _resident((CONV_K, CONV_CH)),
            _resident((1, CONV_CH)),
            _resident((1, CONV_CH)),
            _resident((1, CONV_CH)),
        ],
        out_specs=pl.BlockSpec((None, tl, CONV_CH), lambda s, t: (s, t, 0)),
        out_shape=jax.ShapeDtypeStruct((n_seq, seq_len, CONV_CH), BF16),
        scratch_shapes=[pltpu.VMEM((tl + 2 * HALO, CONV_CH), F32)],
        compiler_params=_cparams(("arbitrary", "arbitrary"), 32),
        name="conformer_conv",
    )(u3, u3, u3, conv_w, vec(conv_b), vec(ln_g), vec(ln_b))
    return out.reshape(n_seq * seq_len, CONV_CH)


def _router_gates(logits):
    lane = lax.broadcasted_iota(jnp.int32, logits.shape, 1)
    m = jnp.max(logits, axis=-1, keepdims=True)
    e = jnp.exp(logits - m)
    probs = e / jnp.sum(e, axis=-1, keepdims=True)
    p1 = jnp.max(probs, axis=-1, keepdims=True)
    i1 = jnp.min(jnp.where(probs == p1, lane, LANES), axis=-1, keepdims=True)
    in_group = (lane // EXPERTS_PER_GROUP) == (i1 // EXPERTS_PER_GROUP)
    cand = jnp.where(in_group, jnp.where(lane == i1, -1.0, probs), -1.0)
    p2 = jnp.max(cand, axis=-1, keepdims=True)
    i2 = jnp.min(jnp.where(cand == p2, lane, LANES), axis=-1, keepdims=True)
    den = p1 + p2
    gates = jnp.where(lane == i1, p1 / den, 0.0) + jnp.where(lane == i2, p2 / den, 0.0)
    group = (i1 // EXPERTS_PER_GROUP).astype(F32)
    return jnp.where(lane == GROUP_LANE, group, gates)


def _post_kernel(oa_ref, ob_ref, oc_ref, gt_ref, x_ref, g1_ref, lg_ref, lb_ref, sc2_ref, sh2_ref,
                 wa_ref, wb_ref, wc_ref, wo_ref, wr_ref, br_ref, x1_ref, hg_ref):
    sg = _sigmoid(gt_ref[...].astype(F32))
    merged = (sg[:, :D_MODEL] * _dot(oa_ref[...], wa_ref[...])
              + sg[:, D_MODEL:2 * D_MODEL] * _dot(ob_ref[...], wb_ref[...])
              + sg[:, 2 * D_MODEL:] * _dot(oc_ref[...], wc_ref[...]))
    y = _dot(merged.astype(BF16), wo_ref[...])
    x1 = _layer_norm(ALPHA * x_ref[...] + g1_ref[...] * y, lg_ref[...], lb_ref[...])
    x1_ref[...] = x1
    h2 = x1 * sc2_ref[...] + sh2_ref[...]
    tm = x1.shape[0]
    for j in range(Y_ROWS):
        hg_ref[pl.ds(j, tm, stride=HG_ROWS), :] = h2[:, j * LANES:(j + 1) * LANES]
    hg_ref[pl.ds(Y_ROWS, tm, stride=HG_ROWS), :] = _router_gates(_dot3(h2, wr_ref[...]) + br_ref[...])
    for j in range(Y_ROWS + 1, HG_ROWS):
        hg_ref[pl.ds(j, tm, stride=HG_ROWS), :] = jnp.zeros((tm, LANES), F32)


def _post(oa, ob, oc, gates, x, g1, ln_g, ln_b, scale2, shift2, wa, wb, wc, wo, wr, br, *, latent, seq_len):
    t = x.shape[0]
    tm = TM_POST
    row = lambda i: (i, 0)
    mod_map = _mod_row_map(latent, seq_len // tm)
    mod = pl.BlockSpec((None, 1, D_MODEL), mod_map)
    return pl.pallas_call(
        _post_kernel,
        grid=(t // tm,),
        in_specs=[
            pl.BlockSpec((tm, NA_WIDTH), row), pl.BlockSpec((tm, CONV_CH), row), pl.BlockSpec((tm, GQ_Q_WIDTH), row),
            pl.BlockSpec((tm, 3 * D_MODEL), row), pl.BlockSpec((tm, D_MODEL), row),
            mod, _resident((1, D_MODEL)), _resident((1, D_MODEL)), mod, mod,
            _resident(wa.shape), _resident(wb.shape), _resident(wc.shape), _resident(wo.shape),
            _resident(wr.shape), _resident(br.shape),
        ],
        out_specs=[pl.BlockSpec((tm, D_MODEL), row), pl.BlockSpec((tm * HG_ROWS, LANES), row)],
        out_shape=[jax.ShapeDtypeStruct((t, D_MODEL), F32), jax.ShapeDtypeStruct((t * HG_ROWS, LANES), F32)],
        compiler_params=_cparams(("arbitrary",), 40),
        name="post_lat" if latent else "post_ctx",
    )(oa, ob, oc, gates, x, g1, ln_g, ln_b, scale2, shift2, wa, wb, wc, wo, wr, br)


def _route(grp, tm):
    n = grp.shape[0]
    n_tiles = n // tm + N_GROUPS
    groups = jnp.arange(N_GROUPS, dtype=jnp.int32)
    onehot = (grp[:, None] == groups[None, :]).astype(jnp.int32)
    csum = jnp.cumsum(onehot, axis=0)
    rank = jnp.sum((csum - onehot) * onehot, axis=1)
    counts = csum[-1]
    tiles = (counts + tm - 1) // tm
    tile_end = jnp.cumsum(tiles)
    tile_start = tile_end - tiles
    slot = jnp.sum(onehot * tile_start[None, :], axis=1) * tm + rank
    tile_ids = jnp.arange(n_tiles, dtype=jnp.int32)
    tile_group = jnp.minimum(jnp.sum((tile_ids[:, None] >= tile_end[None, :]).astype(jnp.int32), axis=1), N_GROUPS - 1)
    src = jnp.zeros((n_tiles * tm,), jnp.int32).at[slot].set(jnp.arange(n, dtype=jnp.int32))
    r = jnp.arange(n_tiles * tm, dtype=jnp.int32)
    slot_tile = r // tm
    in_group = (jnp.repeat(tile_group, tm)[:, None] == groups[None, :]).astype(jnp.int32)
    slot_rank = r - jnp.sum(in_group * tile_start[None, :], axis=1) * tm
    valid = (slot_tile < tile_end[-1]) & (slot_rank < jnp.sum(in_group * counts[None, :], axis=1))
    dst = jnp.where(valid, src, n + (slot_tile % 2) * tm + r % tm)
    return tile_group.astype(jnp.int32), src, dst, tile_end[-1:].astype(jnp.int32)


def _moe_kernel(tg_ref, src_ref, dst_ref, nused_ref, h_hbm, wgu_ref, wd_ref, y_hbm, hbuf, ybuf, gsem, ssem):
    t = pl.program_id(0)
    n_used = nused_ref[0]
    slot = t % 2
    tm = ybuf.shape[1] // Y_ROWS

    def gather_row(tile, s, r):
        tok = src_ref[tile * tm + r]
        return pltpu.make_async_copy(h_hbm.at[pl.ds(pl.multiple_of(tok * HG_ROWS, HG_ROWS), HG_ROWS), :],
                                     hbuf.at[s, pl.ds(pl.multiple_of(r * HG_ROWS, HG_ROWS), HG_ROWS), :], gsem.at[s])

    def scatter_row(tile, s, r):
        row = dst_ref[tile * tm + r]
        return pltpu.make_async_copy(ybuf.at[s, pl.ds(pl.multiple_of(r * Y_ROWS, Y_ROWS), Y_ROWS), :],
                                     y_hbm.at[pl.ds(pl.multiple_of(row * Y_ROWS, Y_ROWS), Y_ROWS), :], ssem.at[s])

    def for_rows(fn):
        def body(r, carry):
            fn(r)
            return carry
        lax.fori_loop(0, tm, body, 0, unroll=8)

    @pl.when(t == 0)
    def _():
        for_rows(lambda r: gather_row(0, 0, r).start())

    @pl.when(t + 1 < n_used)
    def _():
        for_rows(lambda r: gather_row(t + 1, 1 - slot, r).start())

    @pl.when(t < n_used)
    def _():
        for_rows(lambda r: gather_row(t, slot, r).wait())

        @pl.when(t >= 2)
        def _():
            for_rows(lambda r: scatter_row(t - 2, slot, r).wait())

        first = tg_ref[t] * EXPERTS_PER_GROUP
        token_rows = lambda ref, j, per: ref[slot, pl.ds(j, tm, stride=per), :]
        h = jnp.concatenate([token_rows(hbuf, j, HG_ROWS) for j in range(Y_ROWS)], axis=1).astype(BF16)
        gates = token_rows(hbuf, Y_ROWS, HG_ROWS)
        lane = lax.broadcasted_iota(jnp.int32, gates.shape, 1)
        acc = jnp.zeros((tm, D_MODEL), F32)
        for e in range(EXPERTS_PER_GROUP):
            gu = _dot(h, wgu_ref[e])
            a = gu[:, :D_EXPERT]
            ge = jnp.sum(jnp.where(lane == first + e, gates, 0.0), axis=-1, keepdims=True)
            hid = a * _sigmoid(a) * gu[:, D_EXPERT:] * ge
            acc = acc + _dot(hid.astype(BF16), wd_ref[e])
        for j in range(Y_ROWS):
            ybuf[slot, pl.ds(j, tm, stride=Y_ROWS), :] = acc[:, j * LANES:(j + 1) * LANES]
        for_rows(lambda r: scatter_row(t, slot, r).start())

        @pl.when(t == n_used - 1)
        def _():
            for_rows(lambda r: scatter_row(t, slot, r).wait())

            @pl.when(t >= 1)
            def _():
                for_rows(lambda r: scatter_row(t - 1, 1 - slot, r).wait())

            n_rows = y_hbm.shape[0] - 2 * tm * Y_ROWS
            ybuf[0] = jnp.zeros(ybuf.shape[1:], F32)
            for half in range(2):
                fill = pltpu.make_async_copy(
                    ybuf.at[0], y_hbm.at[pl.ds(n_rows + half * tm * Y_ROWS, tm * Y_ROWS), :], ssem.at[half])
                fill.start()
                fill.wait()


def _moe_experts(hg, route, wgu, wd):
    n = hg.shape[0] // HG_ROWS
    tm = TM_MOE
    tile_group, src, dst, n_used = route
    n_tiles = tile_group.shape[0]
    grid_spec = pltpu.PrefetchScalarGridSpec(
        num_scalar_prefetch=4,
        grid=(n_tiles,),
        in_specs=[
            pl.BlockSpec(memory_space=pl.ANY),
            pl.BlockSpec((EXPERTS_PER_GROUP, D_MODEL, 2 * D_EXPERT), lambda t, tg, s, d, nu: (tg[t], 0, 0)),
            pl.BlockSpec((EXPERTS_PER_GROUP, D_EXPERT, D_MODEL), lambda t, tg, s, d, nu: (tg[t], 0, 0)),
        ],
        out_specs=pl.BlockSpec(memory_space=pl.ANY),
        scratch_shapes=[
            pltpu.VMEM((2, tm * HG_ROWS, LANES), F32),
            pltpu.VMEM((2, tm * Y_ROWS, LANES), F32),
            pltpu.SemaphoreType.DMA((2,)),
            pltpu.SemaphoreType.DMA((2,)),
        ],
    )
    return pl.pallas_call(
        _moe_kernel,
        grid_spec=grid_spec,
        out_shape=jax.ShapeDtypeStruct(((n + 2 * tm) * Y_ROWS, LANES), F32),
        compiler_params=_cparams(("arbitrary",), 58),
        name="moe_experts",
    )(tile_group, src, dst, n_used, hg, wgu, wd)


def _moe_out_kernel(x1_ref, y_ref, g2_ref, lg_ref, lb_ref, o_ref):
    tm = x1_ref.shape[0]
    y = jnp.concatenate([y_ref[pl.ds(j, tm, stride=Y_ROWS), :] for j in range(Y_ROWS)], axis=1)
    o_ref[...] = _layer_norm(ALPHA * x1_ref[...] + g2_ref[...] * y, lg_ref[...], lb_ref[...])


def _moe_out(x1, y, g2, ln_g, ln_b, *, latent, seq_len):
    t = x1.shape[0]
    tm = TM_POST
    row = lambda i: (i, 0)
    return pl.pallas_call(
        _moe_out_kernel,
        grid=(t // tm,),
        in_specs=[pl.BlockSpec((tm, D_MODEL), row), pl.BlockSpec((tm * Y_ROWS, LANES), row),
                  pl.BlockSpec((None, 1, D_MODEL), _mod_row_map(latent, seq_len // tm)),
                  _resident((1, D_MODEL)), _resident((1, D_MODEL))],
        out_specs=pl.BlockSpec((tm, D_MODEL), row),
        out_shape=jax.ShapeDtypeStruct((t, D_MODEL), F32),
        compiler_params=_cparams(("arbitrary",), 32),
        name="moe_out",
    )(x1, y, g2, ln_g, ln_b)


def _moe(hg, x1, g2, ln_g, ln_b, wgu, wd, *, latent, seq_len):
    grp = hg.reshape(-1, HG_ROWS, LANES)[:, Y_ROWS, GROUP_LANE].astype(jnp.int32)
    y = _moe_experts(hg, _route(grp, TM_MOE), wgu, wd)
    return _moe_out(x1, y, g2, ln_g, ln_b, latent=latent, seq_len=seq_len)


def _head_mean_matrix():
    i = jnp.arange(GQ_Q_WIDTH) // HEAD_DIM
    return jnp.where(i[:, None] == i[None, :], 1.0 / HEAD_DIM, 0.0).astype(BF16)


def _rope_tables(n_tokens):
    t = jnp.arange(n_tokens, dtype=jnp.int32)
    axis_dim = HEAD_DIM // 2
    inv_freq = ROPE_THETA ** (-jnp.arange(0, axis_dim, 2, dtype=F32) / axis_dim)
    ar = (t // GRID_W).astype(F32)[:, None] * inv_freq
    ac = (t % GRID_W).astype(F32)[:, None] * inv_freq
    c = jnp.concatenate([jnp.cos(ar), jnp.cos(ar), jnp.cos(ac), jnp.cos(ac)], axis=1)
    s = jnp.concatenate([-jnp.sin(ar), jnp.sin(ar), -jnp.sin(ac), jnp.sin(ac)], axis=1)
    return jnp.tile(c, (1, LANES // HEAD_DIM)), jnp.tile(s, (1, LANES // HEAD_DIM))


def _na_bias_table(rpb, rows):
    col = np.arange(GRID_W)
    col_idx = np.clip(col[:, None] - col[None, :], -(NA_KW - 1), NA_KW - 1) + NA_KW - 1
    onehot = (col_idx[None] == np.arange(2 * NA_KW - 1)[:, None, None]).astype(np.float32)
    col_start = np.clip(col - NA_KW // 2, 0, GRID_W - NA_KW)
    col_ok = (col[:, None] >= col_start[None, :]) & (col[:, None] < col_start[None, :] + NA_KW)
    t = jnp.einsum("hic,ckq->hikq", rpb, onehot, precision=lax.Precision.HIGHEST)
    t = jnp.where(col_ok[None, None], t, NEG_INF)
    masked = jnp.full((NA_HEADS, GRID_W, GRID_W), NEG_INF, F32)
    pats = []
    for first_qrow in (0, NA_KH // 2, rows - NA_QROWS):
        krow0 = int(np.clip(first_qrow - NA_KH // 2, 0, rows - NA_WIN_ROWS))
        krows = []
        for kj in range(NA_WIN_ROWS):
            blocks = []
            for qi in range(NA_QROWS):
                qrow, krow = first_qrow + qi, krow0 + kj
                win0 = int(np.clip(qrow - NA_KH // 2, 0, rows - NA_KH))
                in_window = win0 <= krow < win0 + NA_KH
                blocks.append(t[:, krow - qrow + NA_KH - 1] if in_window else masked)
            krows.append(jnp.concatenate(blocks, axis=-1))
        pats.append(jnp.concatenate(krows, axis=-2))
    return jnp.stack(pats)


def _stream_layer(x, mods, l, wts, *, latent, n_seq, seq_len, cache=None, rope=None):
    sh1, sc1, g1, sh2, sc2, g2 = mods
    qa, ka, va, u, qc, kc, vc, gates = _proj(
        x, 1.0 + sc1, sh1, wts["w_in"][l], wts["gmat"], wts["qn"][l], wts["kn"][l], rope,
        latent=latent, seq_len=seq_len)
    if latent:
        nat_k, nat_v, gqa_k, gqa_v = cache
        swap = lambda a: jnp.swapaxes(a, 1, 2)
        out_a = _na_latent(qa, ka, swap(va.reshape(n_seq, seq_len, NA_WIDTH)), nat_k, swap(nat_v),
                           wts["na_bias"][l], n_seq=n_seq, seq_len=seq_len)
        k_all = jnp.concatenate([gqa_k, kc.reshape(n_seq, seq_len, GQ_KV_WIDTH)], axis=1)
        v_all = jnp.concatenate([gqa_v, vc.reshape(n_seq, seq_len, GQ_KV_WIDTH)], axis=1)
        out_c = _gqa_latent(qc, k_all, v_all, n_seq=n_seq, seq_len=seq_len)
    else:
        out_a, out_c = _ctx_attention(qa, ka, va, qc, kc, vc, seq_len=seq_len)
    out_b = _conformer_conv(u, wts["conv_w"][l], wts["conv_b"][l], wts["conv_ln_g"][l], wts["conv_ln_b"][l],
                            n_seq=n_seq, seq_len=seq_len)
    x1, hg = _post(out_a, out_b, out_c, gates, x, g1, wts["ln1_g"][l], wts["ln1_b"][l], 1.0 + sc2, sh2,
                         wts["w_br_a"][l], wts["w_br_b"][l], wts["w_br_c"][l], wts["w_out"][l],
                         wts["w_router"], wts["b_router"], latent=latent, seq_len=seq_len)
    x2 = _moe(hg, x1, g2, wts["ln2_g"][l], wts["ln2_b"][l], wts["w_gate_up"][l], wts["w_down"][l],
              latent=latent, seq_len=seq_len)
    return x2, (ka, va, kc, vc)


def kernel(x_prompt, x_sample, cache_nat_k, cache_nat_v, cache_gqa_k, cache_gqa_v, c, c_ctx, w_mod, b_mod, w_in, nat_rpb, conv_w, conv_b, conv_ln_g, conv_ln_b, q_norm_g, k_norm_g, w_br_a, w_br_b, w_br_c, w_out, ln1_g, ln1_b, ln2_g, ln2_b, w_router, b_router, w_gate_up, w_down):
    batch, seq, _ = x_prompt.shape
    dec_batch, dec_seq, _ = x_sample.shape
    past = cache_nat_k.shape[2]
    assert dec_batch == 2 and seq % TM_PROJ == 0 and dec_seq % TM_MOE == 0

    cond8 = jnp.zeros((8, D_MODEL), F32).at[0].set(c_ctx).at[1:1 + dec_batch].set(c)
    mods_all = _modulation(cond8, w_mod, b_mod)

    pad = LANES - N_EXPERTS
    vecd = lambda a: a.reshape(DEPTH, 1, -1)
    wts = dict(
        w_in=w_in.astype(BF16),
        gmat=_head_mean_matrix(),
        qn=jnp.tile(q_norm_g, (1, GQ_HEADS)).reshape(DEPTH, 1, GQ_Q_WIDTH),
        kn=jnp.tile(k_norm_g, (1, GQ_KV_HEADS)).reshape(DEPTH, 1, GQ_KV_WIDTH),
        na_bias=jnp.stack([_na_bias_table(nat_rpb[l], dec_seq // GRID_W) for l in range(DEPTH)]),
        conv_w=conv_w, conv_b=conv_b, conv_ln_g=conv_ln_g, conv_ln_b=conv_ln_b,
        ln1_g=vecd(ln1_g), ln1_b=vecd(ln1_b), ln2_g=vecd(ln2_g), ln2_b=vecd(ln2_b),
        w_br_a=w_br_a.astype(BF16), w_br_b=w_br_b.astype(BF16), w_br_c=w_br_c.astype(BF16),
        w_out=w_out.astype(BF16),
        w_router=jnp.pad(w_router, ((0, 0), (0, pad))),
        b_router=jnp.pad(b_router, (0, pad), constant_values=NEG_INF).reshape(1, LANES),
        w_gate_up=w_gate_up.astype(BF16), w_down=w_down.astype(BF16),
    )
    rope = _rope_tables(dec_seq)

    def layer_mods(l):
        m = mods_all[l, :1 + dec_batch].reshape(1 + dec_batch, 6, 1, D_MODEL)
        return [m[:, i] for i in range(6)]

    xp = x_prompt.reshape(batch * seq, D_MODEL)
    xs = x_sample.reshape(dec_batch * dec_seq, D_MODEL)
    new_cache = []
    for l in range(DEPTH):
        mods = layer_mods(l)
        xp, ctx_kv = _stream_layer(xp, mods, l, wts, latent=False, n_seq=batch, seq_len=seq)
        new_cache.append(ctx_kv)
        cache = (cache_nat_k[:, l].reshape(dec_batch, past, NA_WIDTH).astype(BF16),
                 cache_nat_v[:, l].reshape(dec_batch, past, NA_WIDTH).astype(BF16),
                 cache_gqa_k[:, l].reshape(dec_batch, past, GQ_KV_WIDTH).astype(BF16),
                 cache_gqa_v[:, l].reshape(dec_batch, past, GQ_KV_WIDTH).astype(BF16))
        xs, _ = _stream_layer(xs, mods, l, wts, latent=True, n_seq=dec_batch, seq_len=dec_seq, cache=cache, rope=rope)

    def stack(i, heads):
        return jnp.stack([kv[i].reshape(batch, seq, heads, HEAD_DIM) for kv in new_cache], axis=1)

    return (xp.reshape(batch, seq, D_MODEL), xs.reshape(dec_batch, dec_seq, D_MODEL),
            stack(0, NA_HEADS), stack(1, NA_HEADS), stack(2, GQ_KV_HEADS), stack(3, GQ_KV_HEADS))
```

```python
import functools

import numpy as np
import jax
import jax.numpy as jnp
from jax import lax
from jax.experimental import pallas as pl
from jax.experimental.pallas import tpu as pltpu

F32 = jnp.float32
BF16 = jnp.bfloat16

D_MODEL = 1024
DEPTH = 2
GRID_W = 64
HEAD_DIM = 64
NA_HEADS = 8
NA_WIDTH = NA_HEADS * HEAD_DIM
NA_KH = 8
NA_KW = 16
NA_QROWS = 4
NA_WIN_ROWS = NA_QROWS + NA_KH
CONV_CH = 512
CONV_K = 31
GQ_HEADS = 8
GQ_KV_HEADS = 2
GQ_Q_WIDTH = GQ_HEADS * HEAD_DIM
GQ_KV_WIDTH = GQ_KV_HEADS * HEAD_DIM
ROPE_THETA = 10000.0
N_EXPERTS = 16
EXPERTS_PER_GROUP = 4
N_GROUPS = N_EXPERTS // EXPERTS_PER_GROUP
GROUP_LANE = N_EXPERTS
Y_ROWS = D_MODEL // 128
HG_ROWS = 2 * Y_ROWS
D_EXPERT = 512
ALPHA = (2 * DEPTH) ** 0.25
LN_EPS = 1e-6
RMS_EPS = 1e-6
NEG_INF = -1e30
QK_SCALE = HEAD_DIM ** -0.5

_OFF = (0, 512, 1024, 1536, 2560, 3072, 3200, 3328, 6400)
IN_COLS = _OFF[-1]

LANES = 128
V7X_VMEM_BYTES = 64 * 2 ** 20
HALO = 16

TM_PROJ = 256
TM_POST = 256
TM_MOE = 512
TL_CONV = 256
TQ_GQA = 256


def _cparams(sem, vmem_mib):
    assert vmem_mib * 2 ** 20 < V7X_VMEM_BYTES
    return pltpu.CompilerParams(dimension_semantics=sem, vmem_limit_bytes=vmem_mib * 2 ** 20)


def _resident(shape):
    nd = len(shape)
    return pl.BlockSpec(shape, lambda *_: (0,) * nd, pipeline_mode=pl.Buffered(1))


def _dot(a, b):
    return jnp.dot(a, b, preferred_element_type=F32)


def _dot_nt(a, b):
    return lax.dot_general(a, b, (((1,), (1,)), ((), ())), preferred_element_type=F32)


def _split_bf16(a):
    hi = a.astype(BF16)
    lo = (a - hi.astype(F32)).astype(BF16)
    return hi, lo


def _dot3(a, b):
    a_hi, a_lo = _split_bf16(a)
    b_hi, b_lo = _split_bf16(b)
    return _dot(a_hi, b_hi) + _dot(a_lo, b_hi) + _dot(a_hi, b_lo)


def _sigmoid(x):
    return 1.0 / (1.0 + jnp.exp(-x))


def _layer_norm(x, g, b):
    mu = jnp.mean(x, axis=-1, keepdims=True)
    xc = x - mu
    var = jnp.mean(xc * xc, axis=-1, keepdims=True)
    return xc * lax.rsqrt(var + LN_EPS) * g + b


def _mod_kernel(cond_ref, w_ref, b_ref, o_ref):
    c = cond_ref[...]
    o_ref[...] = _dot3(c * _sigmoid(c), w_ref[...]) + b_ref[...]


def _modulation(cond8, w_mod, b_mod):
    tn = 1536
    n = w_mod.shape[-1]
    return pl.pallas_call(
        _mod_kernel,
        grid=(DEPTH, n // tn),
        in_specs=[
            pl.BlockSpec((8, D_MODEL), lambda l, j: (0, 0)),
            pl.BlockSpec((None, D_MODEL, tn), lambda l, j: (l, 0, j)),
            pl.BlockSpec((None, 1, tn), lambda l, j: (l, 0, j)),
        ],
        out_specs=pl.BlockSpec((None, 8, tn), lambda l, j: (l, 0, j)),
        out_shape=jax.ShapeDtypeStruct((DEPTH, 8, n), F32),
        compiler_params=_cparams(("arbitrary", "arbitrary"), 40),
        name="modulation",
    )(cond8, w_mod, b_mod.reshape(DEPTH, 1, n))


def _head_rms(x, gmat, gain):
    hi, lo = _split_bf16(x * x)
    ms = _dot(hi, gmat) + _dot(lo, gmat)
    return x * lax.rsqrt(ms + RMS_EPS) * gain


def _rope(x, c, s):
    w = x.shape[1]
    reps = w // LANES
    if reps > 1:
        c = jnp.concatenate([c] * reps, axis=1)
        s = jnp.concatenate([s] * reps, axis=1)
    lane = lax.broadcasted_iota(jnp.int32, x.shape, 1)
    partner = jnp.where((lane % 32) < 16, pltpu.roll(x, w - 16, 1), pltpu.roll(x, 16, 1))
    return x * c + partner * s


def _moe_residual_norm(x1_ref, y_ref, g2_ref, lg_ref, lb_ref):
    tm = x1_ref.shape[0]
    y = jnp.concatenate([y_ref[pl.ds(j, tm, stride=Y_ROWS), :] for j in range(Y_ROWS)], axis=1)
    return _layer_norm(ALPHA * x1_ref[...] + g2_ref[...] * y, lg_ref[...], lb_ref[...])


def _proj_kernel(latent, fused, *refs):
    refs = list(refs)
    n_in = (5 if fused else 1) + 6 + (2 if latent else 0)
    ins, outs = refs[:n_in], refs[n_in:]
    x = _moe_residual_norm(*ins[:5]) if fused else ins[0][...]
    sc_ref, sh_ref, w_ref, gm_ref, qn_ref, kn_ref = ins[n_in - 6 - (2 if latent else 0):][:6]
    if latent:
        c_ref, s_ref = ins[-2:]
    if fused:
        outs.pop(0)[...] = x
    qa_ref, ka_ref, va_ref, u_ref, qc_ref, kc_ref, vc_ref, gt_ref = outs
    h = (x * sc_ref[...] + sh_ref[...]).astype(BF16)

    def mm(i):
        return _dot(h, w_ref[:, _OFF[i]:_OFF[i + 1]])

    qa_ref[...] = (mm(0) * QK_SCALE).astype(qa_ref.dtype)
    ka_ref[...] = mm(1).astype(ka_ref.dtype)
    va_ref[...] = mm(2).astype(va_ref.dtype)
    u_ref[...] = mm(3).astype(u_ref.dtype)
    qc = _head_rms(mm(4), gm_ref[...], qn_ref[...])
    kc = _head_rms(mm(5), gm_ref[:GQ_KV_WIDTH, :GQ_KV_WIDTH], kn_ref[...])
    if latent:
        qc = _rope(qc, c_ref[...], s_ref[...])
        kc = _rope(kc, c_ref[...], s_ref[...])
    qc_ref[...] = (qc * QK_SCALE).astype(qc_ref.dtype)
    kc_ref[...] = kc.astype(kc_ref.dtype)
    vc_ref[...] = mm(6).astype(vc_ref.dtype)
    gt_ref[...] = mm(7).astype(gt_ref.dtype)


def _mod_row_map(latent, tiles_per_seq):
    if latent:
        return lambda i: (1 + i // tiles_per_seq, 0, 0)
    return lambda i: (0, 0, 0)


def _proj(x, scale, shift, w_bf, gmat, qn, kn, rope, *, latent, seq_len):
    fused = isinstance(x, tuple)
    t = (x[0] if fused else x).shape[0]
    tm = TM_PROJ
    kv_dt = BF16 if latent else F32
    row = lambda i: (i, 0)
    mod_map = _mod_row_map(latent, seq_len // tm)
    if fused:
        in_specs = [pl.BlockSpec((tm, D_MODEL), row), pl.BlockSpec((tm * Y_ROWS, LANES), row),
                    pl.BlockSpec((None, 1, D_MODEL), mod_map), _resident((1, D_MODEL)), _resident((1, D_MODEL))]
        args = list(x)
    else:
        in_specs = [pl.BlockSpec((tm, D_MODEL), row)]
        args = [x]
    in_specs += [
        pl.BlockSpec((None, 1, D_MODEL), mod_map),
        pl.BlockSpec((None, 1, D_MODEL), mod_map),
        _resident((D_MODEL, IN_COLS)),
        _resident((GQ_Q_WIDTH, GQ_Q_WIDTH)),
        _resident((1, GQ_Q_WIDTH)),
        _resident((1, GQ_KV_WIDTH)),
    ]
    args += [scale, shift, w_bf, gmat, qn, kn]
    if latent:
        per = seq_len // tm
        in_specs += [pl.BlockSpec((tm, LANES), lambda i: (i % per, 0))] * 2
        args += list(rope)
    widths = (NA_WIDTH, NA_WIDTH, NA_WIDTH, 2 * CONV_CH, GQ_Q_WIDTH, GQ_KV_WIDTH, GQ_KV_WIDTH, 3 * D_MODEL)
    dtypes = (BF16, kv_dt, kv_dt, BF16, BF16, kv_dt, kv_dt, BF16)
    if fused:
        widths = (D_MODEL,) + widths
        dtypes = (F32,) + dtypes
    return pl.pallas_call(
        functools.partial(_proj_kernel, latent, fused),
        grid=(t // tm,),
        in_specs=in_specs,
        out_specs=[pl.BlockSpec((tm, w), row) for w in widths],
        out_shape=[jax.ShapeDtypeStruct((t, w), dt) for w, dt in zip(widths, dtypes)],
        compiler_params=_cparams(("arbitrary",), 48),
        name="proj_lat" if latent else "proj_ctx",
    )(*args)


def _pair(ref_or_val, j):
    return ref_or_val[:, j * LANES:(j + 1) * LANES]


def _keep_half(x, half, move_to=None):
    lane = lax.broadcasted_iota(jnp.int32, x.shape, 1)
    keep = (lane >= HEAD_DIM) if half else (lane < HEAD_DIM)
    if move_to is None or move_to == half:
        return jnp.where(keep, x, jnp.zeros_like(x))
    xf = jnp.where(keep, x.astype(F32), 0.0)
    return pltpu.roll(xf, HEAD_DIM, 1).astype(x.dtype)


def _attend_t(qp, pieces):
    scores = []
    for k, _, bias in pieces:
        s = _dot_nt(k, qp)
        scores.append(s if bias is None else s + bias)
    m = None
    for s in scores:
        mi = jnp.max(s, axis=0, keepdims=True)
        m = mi if m is None else jnp.maximum(m, mi)
    l = 0.0
    o = 0.0
    for s, (_, vt, _) in zip(scores, pieces):
        p = jnp.exp(s - m)
        l = l + jnp.sum(p, axis=0, keepdims=True)
        o = o + _dot(vt, p.astype(BF16))
    return o / l


def _select_rows(o_even, o_odd):
    return jnp.concatenate([o_even[:HEAD_DIM], o_odd[HEAD_DIM:]], axis=0)


def _ctx_attn_kernel(qa_ref, ka_ref, va_ref, qc_ref, kc_ref, vc_ref, oa_ref, oc_ref):
    ka = ka_ref[...].astype(BF16)
    vat = va_ref[...].T.astype(BF16)
    outs = []
    for j in range(NA_HEADS // 2):
        k = _pair(ka, j)
        vt = vat[j * LANES:(j + 1) * LANES]
        q = _pair(qa_ref, j)
        outs.append(_select_rows(_attend_t(_keep_half(q, 0), [(k, vt, None)]),
                                 _attend_t(_keep_half(q, 1), [(k, vt, None)])))
    oa_ref[...] = jnp.concatenate(outs, axis=0).T.astype(oa_ref.dtype)
    kc = kc_ref[...].astype(BF16)
    vct = vc_ref[...].T.astype(BF16)
    outs = []
    for h in range(GQ_HEADS):
        g = h // (GQ_HEADS // GQ_KV_HEADS)
        o = _attend_t(_keep_half(_pair(qc_ref, h // 2), h % 2, move_to=g), [(kc, vct, None)])
        outs.append(o[g * HEAD_DIM:(g + 1) * HEAD_DIM])
    oc_ref[...] = jnp.concatenate(outs, axis=0).T.astype(oc_ref.dtype)


def _ctx_attention(qa, ka, va, qc, kc, vc, *, seq_len):
    t = qa.shape[0]
    row = lambda i: (i, 0)
    spec = lambda w: pl.BlockSpec((seq_len, w), row)
    return pl.pallas_call(
        _ctx_attn_kernel,
        grid=(t // seq_len,),
        in_specs=[spec(NA_WIDTH), spec(NA_WIDTH), spec(NA_WIDTH), spec(GQ_Q_WIDTH), spec(GQ_KV_WIDTH), spec(GQ_KV_WIDTH)],
        out_specs=[spec(NA_WIDTH), spec(GQ_Q_WIDTH)],
        out_shape=[jax.ShapeDtypeStruct((t, NA_WIDTH), BF16), jax.ShapeDtypeStruct((t, GQ_Q_WIDTH), BF16)],
        compiler_params=_cparams(("arbitrary",), 32),
        name="attn_ctx",
    )(qa, ka, va, qc, kc, vc)


def _na_key_row_start(blk, rows):
    return jnp.clip(blk * NA_QROWS - NA_KH // 2, 0, rows - NA_WIN_ROWS)


def _na_lat_kernel(rows, q_ref, k_ref, vt_ref, kc_ref, vct_ref, bias_ref, o_ref):
    base = pl.multiple_of(_na_key_row_start(pl.program_id(1), rows) * GRID_W, 2 * LANES)
    nk = NA_WIN_ROWS * GRID_W
    kwin = k_ref[pl.ds(base, nk), :]
    outs = []
    for j in range(NA_HEADS // 2):
        k = _pair(kwin, j)
        vt = vt_ref[j * LANES:(j + 1) * LANES, pl.ds(base, nk)]
        kc = _pair(kc_ref, j)
        vct = vct_ref[j * LANES:(j + 1) * LANES, :]
        q = _pair(q_ref, j)
        halves = [_attend_t(_keep_half(q, half), [(k, vt, bias_ref[2 * j + half]), (kc, vct, None)])
                  for half in range(2)]
        outs.append(_select_rows(*halves))
    o_ref[...] = jnp.concatenate(outs, axis=0).T.astype(o_ref.dtype)


def _na_latent(q, k, vt, k_ctx, vt_ctx, bias, *, n_seq, seq_len):
    rows = seq_len // GRID_W
    nblk = rows // NA_QROWS
    tq = NA_QROWS * GRID_W
    pattern = lambda b, i: (jnp.where(i == 0, 0, jnp.where(i == nblk - 1, 2, 1)), 0, 0, 0)
    return pl.pallas_call(
        functools.partial(_na_lat_kernel, rows),
        grid=(n_seq, nblk),
        in_specs=[
            pl.BlockSpec((tq, NA_WIDTH), lambda b, i: (b * nblk + i, 0)),
            pl.BlockSpec((seq_len, NA_WIDTH), lambda b, i: (b, 0)),
            pl.BlockSpec((None, NA_WIDTH, seq_len), lambda b, i: (b, 0, 0)),
            pl.BlockSpec((None,) + k_ctx.shape[1:], lambda b, i: (b, 0, 0)),
            pl.BlockSpec((None,) + vt_ctx.shape[1:], lambda b, i: (b, 0, 0)),
            pl.BlockSpec((None,) + bias.shape[1:], pattern),
        ],
        out_specs=pl.BlockSpec((tq, NA_WIDTH), lambda b, i: (b * nblk + i, 0)),
        out_shape=jax.ShapeDtypeStruct(q.shape, BF16),
        compiler_params=_cparams(("arbitrary", "arbitrary"), 52),
        name="attn_na_lat",
    )(q, k, vt, k_ctx, vt_ctx, bias)


def _gqa_lat_kernel(q_ref, k_ref, v_ref, o_ref):
    k = k_ref[...]
    v = v_ref[...]
    for h in range(GQ_HEADS):
        g = h // (GQ_HEADS // GQ_KV_HEADS)
        s = _dot_nt(_keep_half(_pair(q_ref, h // 2), h % 2, move_to=g), k)
        p = jnp.exp(s - jnp.max(s, axis=-1, keepdims=True))
        o = _dot(p.astype(BF16), v) / jnp.sum(p, axis=-1, keepdims=True)
        o_ref[:, h * HEAD_DIM:(h + 1) * HEAD_DIM] = o[:, g * HEAD_DIM:(g + 1) * HEAD_DIM].astype(o_ref.dtype)


def _gqa_latent(q, k_all, v_all, *, n_seq, seq_len):
    per = seq_len // TQ_GQA
    s_all = k_all.shape[1]
    return pl.pallas_call(
        _gqa_lat_kernel,
        grid=(n_seq, per),
        in_specs=[
            pl.BlockSpec((TQ_GQA, GQ_Q_WIDTH), lambda b, t: (b * per + t, 0)),
            pl.BlockSpec((None, s_all, GQ_KV_WIDTH), lambda b, t: (b, 0, 0)),
            pl.BlockSpec((None, s_all, GQ_KV_WIDTH), lambda b, t: (b, 0, 0)),
        ],
        out_specs=pl.BlockSpec((TQ_GQA, GQ_Q_WIDTH), lambda b, t: (b * per + t, 0)),
        out_shape=jax.ShapeDtypeStruct(q.shape, BF16),
        compiler_params=_cparams(("arbitrary", "arbitrary"), 48),
        name="attn_gqa_lat",
    )(q, k_all, v_all)


def _conv_kernel(u_ref, up_ref, un_ref, w_ref, b_ref, g_ref, beta_ref, o_ref, shifted_ref):
    t = pl.program_id(1)
    nt = pl.num_programs(1)
    tl = u_ref.shape[0]

    def glu(v):
        return v[:, :CONV_CH].astype(F32) * _sigmoid(v[:, CONV_CH:].astype(F32))

    slab = jnp.concatenate([jnp.where(t > 0, glu(up_ref[...]), 0.0), glu(u_ref[...]),
                            jnp.where(t < nt - 1, glu(un_ref[...]), 0.0)], axis=0)
    first = HALO - CONV_K // 2
    span = shifted_ref.shape[1]
    assert (first + CONV_K - 1) // 8 * 8 + tl == span and span + 7 <= slab.shape[0]
    acc = jnp.zeros((tl, CONV_CH), F32)
    for shift in range(8):
        shifted_ref[shift % 2] = slab[shift:shift + span]
        for k in range(CONV_K):
            if (first + k) % 8 == shift:
                base = first + k - shift
                acc = acc + shifted_ref[shift % 2, base:base + tl, :] * w_ref[k:k + 1, :]
    y = _layer_norm(acc + b_ref[...], g_ref[...], beta_ref[...])
    o_ref[...] = (y * _sigmoid(y)).astype(o_ref.dtype)


def _conformer_conv(u, conv_w, conv_b, ln_g, ln_b, *, n_seq, seq_len):
    tl = TL_CONV
    nt = seq_len // tl
    hb = tl // HALO
    last_hb = seq_len // HALO - 1
    u3 = u.reshape(n_seq, seq_len, 2 * CONV_CH)
    vec = lambda a: a.reshape(1, CONV_CH)
    out = pl.pallas_call(
        _conv_kernel,
        grid=(n_seq, nt),
        in_specs=[
            pl.BlockSpec((None, tl, 2 * CONV_CH), lambda s, t: (s, t, 0)),
            pl.BlockSpec((None, HALO, 2 * CONV_CH), lambda s, t: (s, jnp.maximum(t * hb - 1, 0), 0)),
            pl.BlockSpec((None, HALO, 2 * CONV_CH), lambda s, t: (s, jnp.minimum((t + 1) * hb, last_hb), 0)),
            _resident((CONV_K, CONV_CH)),
            _resident((1, CONV_CH)),
            _resident((1, CONV_CH)),
            _resident((1, CONV_CH)),
        ],
        out_specs=pl.BlockSpec((None, tl, CONV_CH), lambda s, t: (s, t, 0)),
        out_shape=jax.ShapeDtypeStruct((n_seq, seq_len, CONV_CH), BF16),
        scratch_shapes=[pltpu.VMEM((2, (HALO + CONV_K // 2) // 8 * 8 + tl, CONV_CH), F32)],
        compiler_params=_cparams(("arbitrary", "arbitrary"), 32),
        name="conformer_conv",
    )(u3, u3, u3, conv_w, vec(conv_b), vec(ln_g), vec(ln_b))
    return out.reshape(n_seq * seq_len, CONV_CH)


def _router_gates(logits):
    lane = lax.broadcasted_iota(jnp.int32, logits.shape, 1)
    m = jnp.max(logits, axis=-1, keepdims=True)
    e = jnp.exp(logits - m)
    probs = e / jnp.sum(e, axis=-1, keepdims=True)
    p1 = jnp.max(probs, axis=-1, keepdims=True)
    i1 = jnp.min(jnp.where(probs == p1, lane, LANES), axis=-1, keepdims=True)
    in_group = (lane // EXPERTS_PER_GROUP) == (i1 // EXPERTS_PER_GROUP)
    cand = jnp.where(in_group, jnp.where(lane == i1, -1.0, probs), -1.0)
    p2 = jnp.max(cand, axis=-1, keepdims=True)
    i2 = jnp.min(jnp.where(cand == p2, lane, LANES), axis=-1, keepdims=True)
    den = p1 + p2
    gates = jnp.where(lane == i1, p1 / den, 0.0) + jnp.where(lane == i2, p2 / den, 0.0)
    group = (i1 // EXPERTS_PER_GROUP).astype(F32)
    return jnp.where(lane == GROUP_LANE, group, gates)


def _post_kernel(oa_ref, ob_ref, oc_ref, gt_ref, x_ref, g1_ref, lg_ref, lb_ref, sc2_ref, sh2_ref,
                 wa_ref, wb_ref, wc_ref, wo_ref, wr_ref, br_ref, x1_ref, hg_ref):
    sg = _sigmoid(gt_ref[...].astype(F32))
    merged = (sg[:, :D_MODEL] * _dot(oa_ref[...], wa_ref[...])
              + sg[:, D_MODEL:2 * D_MODEL] * _dot(ob_ref[...], wb_ref[...])
              + sg[:, 2 * D_MODEL:] * _dot(oc_ref[...], wc_ref[...]))
    y = _dot(merged.astype(BF16), wo_ref[...])
    x1 = _layer_norm(ALPHA * x_ref[...] + g1_ref[...] * y, lg_ref[...], lb_ref[...])
    x1_ref[...] = x1
    h2 = x1 * sc2_ref[...] + sh2_ref[...]
    tm = x1.shape[0]
    for j in range(Y_ROWS):
        hg_ref[pl.ds(j, tm, stride=HG_ROWS), :] = h2[:, j * LANES:(j + 1) * LANES]
    hg_ref[pl.ds(Y_ROWS, tm, stride=HG_ROWS), :] = _router_gates(_dot3(h2, wr_ref[...]) + br_ref[...])
    for j in range(Y_ROWS + 1, HG_ROWS):
        hg_ref[pl.ds(j, tm, stride=HG_ROWS), :] = jnp.zeros((tm, LANES), F32)


def _post(oa, ob, oc, gates, x, g1, ln_g, ln_b, scale2, shift2, wa, wb, wc, wo, wr, br, *, latent, seq_len):
    t = x.shape[0]
    tm = TM_POST
    row = lambda i: (i, 0)
    mod_map = _mod_row_map(latent, seq_len // tm)
    mod = pl.BlockSpec((None, 1, D_MODEL), mod_map)
    return pl.pallas_call(
        _post_kernel,
        grid=(t // tm,),
        in_specs=[
            pl.BlockSpec((tm, NA_WIDTH), row), pl.BlockSpec((tm, CONV_CH), row), pl.BlockSpec((tm, GQ_Q_WIDTH), row),
            pl.BlockSpec((tm, 3 * D_MODEL), row), pl.BlockSpec((tm, D_MODEL), row),
            mod, _resident((1, D_MODEL)), _resident((1, D_MODEL)), mod, mod,
            _resident(wa.shape), _resident(wb.shape), _resident(wc.shape), _resident(wo.shape),
            _resident(wr.shape), _resident(br.shape),
        ],
        out_specs=[pl.BlockSpec((tm, D_MODEL), row), pl.BlockSpec((tm * HG_ROWS, LANES), row)],
        out_shape=[jax.ShapeDtypeStruct((t, D_MODEL), F32), jax.ShapeDtypeStruct((t * HG_ROWS, LANES), F32)],
        compiler_params=_cparams(("arbitrary",), 40),
        name="post_lat" if latent else "post_ctx",
    )(oa, ob, oc, gates, x, g1, ln_g, ln_b, scale2, shift2, wa, wb, wc, wo, wr, br)


def _route(grp, tm):
    n = grp.shape[0]
    n_tiles = n // tm + N_GROUPS
    groups = jnp.arange(N_GROUPS, dtype=jnp.int32)
    onehot = (grp[:, None] == groups[None, :]).astype(jnp.int32)
    csum = jnp.cumsum(onehot, axis=0)
    rank = jnp.sum((csum - onehot) * onehot, axis=1)
    counts = csum[-1]
    tiles = (counts + tm - 1) // tm
    tile_end = jnp.cumsum(tiles)
    tile_start = tile_end - tiles
    slot = jnp.sum(onehot * tile_start[None, :], axis=1) * tm + rank
    tile_ids = jnp.arange(n_tiles, dtype=jnp.int32)
    tile_group = jnp.minimum(jnp.sum((tile_ids[:, None] >= tile_end[None, :]).astype(jnp.int32), axis=1), N_GROUPS - 1)
    src = jnp.zeros((n_tiles * tm,), jnp.int32).at[slot].set(jnp.arange(n, dtype=jnp.int32))
    r = jnp.arange(n_tiles * tm, dtype=jnp.int32)
    slot_tile = r // tm
    in_group = (jnp.repeat(tile_group, tm)[:, None] == groups[None, :]).astype(jnp.int32)
    slot_rank = r - jnp.sum(in_group * tile_start[None, :], axis=1) * tm
    valid = (slot_tile < tile_end[-1]) & (slot_rank < jnp.sum(in_group * counts[None, :], axis=1))
    dst = jnp.where(valid, src, n + (slot_tile % 2) * tm + r % tm)
    return tile_group.astype(jnp.int32), src, dst, tile_end[-1:].astype(jnp.int32)


def _moe_kernel(tg_ref, src_ref, dst_ref, nused_ref, h_hbm, wgu_ref, wd_ref, y_hbm, hbuf, ybuf, gsem, ssem):
    t = pl.program_id(0)
    n_used = nused_ref[0]
    slot = t % 2
    tm = ybuf.shape[1] // Y_ROWS

    def gather_row(tile, s, r):
        tok = src_ref[tile * tm + r]
        return pltpu.make_async_copy(h_hbm.at[pl.ds(pl.multiple_of(tok * HG_ROWS, HG_ROWS), HG_ROWS), :],
                                     hbuf.at[s, pl.ds(pl.multiple_of(r * HG_ROWS, HG_ROWS), HG_ROWS), :], gsem.at[s])

    def scatter_row(tile, s, r):
        row = dst_ref[tile * tm + r]
        return pltpu.make_async_copy(ybuf.at[s, pl.ds(pl.multiple_of(r * Y_ROWS, Y_ROWS), Y_ROWS), :],
                                     y_hbm.at[pl.ds(pl.multiple_of(row * Y_ROWS, Y_ROWS), Y_ROWS), :], ssem.at[s])

    def for_rows(fn):
        def body(r, carry):
            fn(r)
            return carry
        lax.fori_loop(0, tm, body, 0, unroll=8)

    @pl.when(t == 0)
    def _():
        for_rows(lambda r: gather_row(0, 0, r).start())

    @pl.when(t + 1 < n_used)
    def _():
        for_rows(lambda r: gather_row(t + 1, 1 - slot, r).start())

    @pl.when(t < n_used)
    def _():
        for_rows(lambda r: gather_row(t, slot, r).wait())

        @pl.when(t >= 2)
        def _():
            for_rows(lambda r: scatter_row(t - 2, slot, r).wait())

        first = tg_ref[t] * EXPERTS_PER_GROUP
        token_rows = lambda ref, j, per: ref[slot, pl.ds(j, tm, stride=per), :]
        h = jnp.concatenate([token_rows(hbuf, j, HG_ROWS) for j in range(Y_ROWS)], axis=1).astype(BF16)
        gates = token_rows(hbuf, Y_ROWS, HG_ROWS)
        lane = lax.broadcasted_iota(jnp.int32, gates.shape, 1)
        acc = jnp.zeros((tm, D_MODEL), F32)
        for e in range(EXPERTS_PER_GROUP):
            gu = _dot(h, wgu_ref[e])
            a = gu[:, :D_EXPERT]
            ge = jnp.sum(jnp.where(lane == first + e, gates, 0.0), axis=-1, keepdims=True)
            hid = a * _sigmoid(a) * gu[:, D_EXPERT:] * ge
            acc = acc + _dot(hid.astype(BF16), wd_ref[e])
        for j in range(Y_ROWS):
            ybuf[slot, pl.ds(j, tm, stride=Y_ROWS), :] = acc[:, j * LANES:(j + 1) * LANES]
        for_rows(lambda r: scatter_row(t, slot, r).start())

        @pl.when(t == n_used - 1)
        def _():
            for_rows(lambda r: scatter_row(t, slot, r).wait())

            @pl.when(t >= 1)
            def _():
                for_rows(lambda r: scatter_row(t - 1, 1 - slot, r).wait())

            n_rows = y_hbm.shape[0] - 2 * tm * Y_ROWS
            ybuf[0] = jnp.zeros(ybuf.shape[1:], F32)
            for half in range(2):
                fill = pltpu.make_async_copy(
                    ybuf.at[0], y_hbm.at[pl.ds(n_rows + half * tm * Y_ROWS, tm * Y_ROWS), :], ssem.at[half])
                fill.start()
                fill.wait()


def _moe_experts(hg, route, wgu, wd):
    n = hg.shape[0] // HG_ROWS
    tm = TM_MOE
    tile_group, src, dst, n_used = route
    n_tiles = tile_group.shape[0]
    grid_spec = pltpu.PrefetchScalarGridSpec(
        num_scalar_prefetch=4,
        grid=(n_tiles,),
        in_specs=[
            pl.BlockSpec(memory_space=pl.ANY),
            pl.BlockSpec((EXPERTS_PER_GROUP, D_MODEL, 2 * D_EXPERT), lambda t, tg, s, d, nu: (tg[t], 0, 0)),
            pl.BlockSpec((EXPERTS_PER_GROUP, D_EXPERT, D_MODEL), lambda t, tg, s, d, nu: (tg[t], 0, 0)),
        ],
        out_specs=pl.BlockSpec(memory_space=pl.ANY),
        scratch_shapes=[
            pltpu.VMEM((2, tm * HG_ROWS, LANES), F32),
            pltpu.VMEM((2, tm * Y_ROWS, LANES), F32),
            pltpu.SemaphoreType.DMA((2,)),
            pltpu.SemaphoreType.DMA((2,)),
        ],
    )
    return pl.pallas_call(
        _moe_kernel,
        grid_spec=grid_spec,
        out_shape=jax.ShapeDtypeStruct(((n + 2 * tm) * Y_ROWS, LANES), F32),
        compiler_params=_cparams(("arbitrary",), 58),
        name="moe_experts",
    )(tile_group, src, dst, n_used, hg, wgu, wd)


def _moe_out_kernel(x1_ref, y_ref, g2_ref, lg_ref, lb_ref, o_ref):
    o_ref[...] = _moe_residual_norm(x1_ref, y_ref, g2_ref, lg_ref, lb_ref)


def _moe_out(x1, y, g2, ln_g, ln_b, *, latent, seq_len):
    t = x1.shape[0]
    tm = TM_POST
    row = lambda i: (i, 0)
    return pl.pallas_call(
        _moe_out_kernel,
        grid=(t // tm,),
        in_specs=[pl.BlockSpec((tm, D_MODEL), row), pl.BlockSpec((tm * Y_ROWS, LANES), row),
                  pl.BlockSpec((None, 1, D_MODEL), _mod_row_map(latent, seq_len // tm)),
                  _resident((1, D_MODEL)), _resident((1, D_MODEL))],
        out_specs=pl.BlockSpec((tm, D_MODEL), row),
        out_shape=jax.ShapeDtypeStruct((t, D_MODEL), F32),
        compiler_params=_cparams(("arbitrary",), 32),
        name="moe_out",
    )(x1, y, g2, ln_g, ln_b)


def _moe(hg, wgu, wd):
    grp = hg.reshape(-1, HG_ROWS, LANES)[:, Y_ROWS, GROUP_LANE].astype(jnp.int32)
    return _moe_experts(hg, _route(grp, TM_MOE), wgu, wd)


def _head_mean_matrix():
    i = jnp.arange(GQ_Q_WIDTH) // HEAD_DIM
    return jnp.where(i[:, None] == i[None, :], 1.0 / HEAD_DIM, 0.0).astype(BF16)


def _rope_tables(n_tokens):
    t = jnp.arange(n_tokens, dtype=jnp.int32)
    axis_dim = HEAD_DIM // 2
    inv_freq = ROPE_THETA ** (-jnp.arange(0, axis_dim, 2, dtype=F32) / axis_dim)
    ar = (t // GRID_W).astype(F32)[:, None] * inv_freq
    ac = (t % GRID_W).astype(F32)[:, None] * inv_freq
    c = jnp.concatenate([jnp.cos(ar), jnp.cos(ar), jnp.cos(ac), jnp.cos(ac)], axis=1)
    s = jnp.concatenate([-jnp.sin(ar), jnp.sin(ar), -jnp.sin(ac), jnp.sin(ac)], axis=1)
    return jnp.tile(c, (1, LANES // HEAD_DIM)), jnp.tile(s, (1, LANES // HEAD_DIM))


def _na_bias_table(rpb, rows):
    col = np.arange(GRID_W)
    col_idx = np.clip(col[:, None] - col[None, :], -(NA_KW - 1), NA_KW - 1) + NA_KW - 1
    onehot = (col_idx[None] == np.arange(2 * NA_KW - 1)[:, None, None]).astype(np.float32)
    col_start = np.clip(col - NA_KW // 2, 0, GRID_W - NA_KW)
    col_ok = (col[:, None] >= col_start[None, :]) & (col[:, None] < col_start[None, :] + NA_KW)
    t = jnp.einsum("hic,ckq->hikq", rpb, onehot, precision=lax.Precision.HIGHEST)
    t = jnp.where(col_ok[None, None], t, NEG_INF)
    masked = jnp.full((NA_HEADS, GRID_W, GRID_W), NEG_INF, F32)
    pats = []
    for first_qrow in (0, NA_KH // 2, rows - NA_QROWS):
        krow0 = int(np.clip(first_qrow - NA_KH // 2, 0, rows - NA_WIN_ROWS))
        krows = []
        for kj in range(NA_WIN_ROWS):
            blocks = []
            for qi in range(NA_QROWS):
                qrow, krow = first_qrow + qi, krow0 + kj
                win0 = int(np.clip(qrow - NA_KH // 2, 0, rows - NA_KH))
                in_window = win0 <= krow < win0 + NA_KH
                blocks.append(t[:, krow - qrow + NA_KH - 1] if in_window else masked)
            krows.append(jnp.concatenate(blocks, axis=-1))
        pats.append(jnp.concatenate(krows, axis=-2))
    return jnp.stack(pats)


def _stream_layer(x, mods, l, wts, *, latent, n_seq, seq_len, cache=None, rope=None):
    sh1, sc1, g1, sh2, sc2, g2 = mods
    pieces = _proj(x, 1.0 + sc1, sh1, wts["w_in"][l], wts["gmat"], wts["qn"][l], wts["kn"][l], rope,
                   latent=latent, seq_len=seq_len)
    if isinstance(x, tuple):
        x, pieces = pieces[0], pieces[1:]
    qa, ka, va, u, qc, kc, vc, gates = pieces
    if latent:
        nat_k, nat_v, gqa_k, gqa_v = cache
        swap = lambda a: jnp.swapaxes(a, 1, 2)
        out_a = _na_latent(qa, ka, swap(va.reshape(n_seq, seq_len, NA_WIDTH)), nat_k, swap(nat_v),
                           wts["na_bias"][l], n_seq=n_seq, seq_len=seq_len)
        k_all = jnp.concatenate([gqa_k, kc.reshape(n_seq, seq_len, GQ_KV_WIDTH)], axis=1)
        v_all = jnp.concatenate([gqa_v, vc.reshape(n_seq, seq_len, GQ_KV_WIDTH)], axis=1)
        out_c = _gqa_latent(qc, k_all, v_all, n_seq=n_seq, seq_len=seq_len)
    else:
        out_a, out_c = _ctx_attention(qa, ka, va, qc, kc, vc, seq_len=seq_len)
    out_b = _conformer_conv(u, wts["conv_w"][l], wts["conv_b"][l], wts["conv_ln_g"][l], wts["conv_ln_b"][l],
                            n_seq=n_seq, seq_len=seq_len)
    x1, hg = _post(out_a, out_b, out_c, gates, x, g1, wts["ln1_g"][l], wts["ln1_b"][l], 1.0 + sc2, sh2,
                         wts["w_br_a"][l], wts["w_br_b"][l], wts["w_br_c"][l], wts["w_out"][l],
                         wts["w_router"], wts["b_router"], latent=latent, seq_len=seq_len)
    y = _moe(hg, wts["w_gate_up"][l], wts["w_down"][l])
    return (x1, y, g2, wts["ln2_g"][l], wts["ln2_b"][l]), (ka, va, kc, vc)


def kernel(x_prompt, x_sample, cache_nat_k, cache_nat_v, cache_gqa_k, cache_gqa_v, c, c_ctx, w_mod, b_mod, w_in, nat_rpb, conv_w, conv_b, conv_ln_g, conv_ln_b, q_norm_g, k_norm_g, w_br_a, w_br_b, w_br_c, w_out, ln1_g, ln1_b, ln2_g, ln2_b, w_router, b_router, w_gate_up, w_down):
    batch, seq, _ = x_prompt.shape
    dec_batch, dec_seq, _ = x_sample.shape
    past = cache_nat_k.shape[2]
    assert dec_batch == 2 and seq % TM_PROJ == 0 and dec_seq % TM_MOE == 0

    cond8 = jnp.zeros((8, D_MODEL), F32).at[0].set(c_ctx).at[1:1 + dec_batch].set(c)
    mods_all = _modulation(cond8, w_mod, b_mod)

    pad = LANES - N_EXPERTS
    vecd = lambda a: a.reshape(DEPTH, 1, -1)
    wts = dict(
        w_in=w_in.astype(BF16),
        gmat=_head_mean_matrix(),
        qn=jnp.tile(q_norm_g, (1, GQ_HEADS)).reshape(DEPTH, 1, GQ_Q_WIDTH),
        kn=jnp.tile(k_norm_g, (1, GQ_KV_HEADS)).reshape(DEPTH, 1, GQ_KV_WIDTH),
        na_bias=jnp.stack([_na_bias_table(nat_rpb[l], dec_seq // GRID_W) for l in range(DEPTH)]),
        conv_w=conv_w, conv_b=conv_b, conv_ln_g=conv_ln_g, conv_ln_b=conv_ln_b,
        ln1_g=vecd(ln1_g), ln1_b=vecd(ln1_b), ln2_g=vecd(ln2_g), ln2_b=vecd(ln2_b),
        w_br_a=w_br_a.astype(BF16), w_br_b=w_br_b.astype(BF16), w_br_c=w_br_c.astype(BF16),
        w_out=w_out.astype(BF16),
        w_router=jnp.pad(w_router, ((0, 0), (0, pad))),
        b_router=jnp.pad(b_router, (0, pad), constant_values=NEG_INF).reshape(1, LANES),
        w_gate_up=w_gate_up.astype(BF16), w_down=w_down.astype(BF16),
    )
    rope = _rope_tables(dec_seq)

    def layer_mods(l):
        m = mods_all[l, :1 + dec_batch].reshape(1 + dec_batch, 6, 1, D_MODEL)
        return [m[:, i] for i in range(6)]

    xp = x_prompt.reshape(batch * seq, D_MODEL)
    xs = x_sample.reshape(dec_batch * dec_seq, D_MODEL)
    new_cache = []
    for l in range(DEPTH):
        mods = layer_mods(l)
        xp, ctx_kv = _stream_layer(xp, mods, l, wts, latent=False, n_seq=batch, seq_len=seq)
        new_cache.append(ctx_kv)
        cache = (cache_nat_k[:, l].reshape(dec_batch, past, NA_WIDTH).astype(BF16),
                 cache_nat_v[:, l].reshape(dec_batch, past, NA_WIDTH).astype(BF16),
                 cache_gqa_k[:, l].reshape(dec_batch, past, GQ_KV_WIDTH).astype(BF16),
                 cache_gqa_v[:, l].reshape(dec_batch, past, GQ_KV_WIDTH).astype(BF16))
        xs, _ = _stream_layer(xs, mods, l, wts, latent=True, n_seq=dec_batch, seq_len=dec_seq, cache=cache, rope=rope)
    xp = _moe_out(*xp, latent=False, seq_len=seq)
    xs = _moe_out(*xs, latent=True, seq_len=dec_seq)

    def stack(i, heads):
        return jnp.stack([kv[i].reshape(batch, seq, heads, HEAD_DIM) for kv in new_cache], axis=1)

    return (xp.reshape(batch, seq, D_MODEL), xs.reshape(dec_batch, dec_seq, D_MODEL),
            stack(0, NA_HEADS), stack(1, NA_HEADS), stack(2, GQ_KV_HEADS), stack(3, GQ_KV_HEADS))
```

```python
import functools

import numpy as np
import jax
import jax.numpy as jnp
from jax import lax
from jax.experimental import pallas as pl
from jax.experimental.pallas import tpu as pltpu

F32 = jnp.float32
BF16 = jnp.bfloat16

D_MODEL = 1024
DEPTH = 2
GRID_W = 64
HEAD_DIM = 64
NA_HEADS = 8
NA_WIDTH = NA_HEADS * HEAD_DIM
NA_KH = 8
NA_KW = 16
NA_QROWS = 4
NA_WIN_ROWS = NA_QROWS + NA_KH
CONV_CH = 512
CONV_K = 31
GQ_HEADS = 8
GQ_KV_HEADS = 2
GQ_Q_WIDTH = GQ_HEADS * HEAD_DIM
GQ_KV_WIDTH = GQ_KV_HEADS * HEAD_DIM
ROPE_THETA = 10000.0
N_EXPERTS = 16
EXPERTS_PER_GROUP = 4
N_GROUPS = N_EXPERTS // EXPERTS_PER_GROUP
GROUP_LANE = N_EXPERTS
Y_ROWS = D_MODEL // 128
HG_ROWS = 2 * Y_ROWS
HG_USED = Y_ROWS + 1
D_EXPERT = 512
ALPHA = (2 * DEPTH) ** 0.25
LN_EPS = 1e-6
RMS_EPS = 1e-6
NEG_INF = -1e30
QK_SCALE = HEAD_DIM ** -0.5

_OFF = (0, 512, 1024, 1536, 2560, 3072, 3200, 3328, 6400)
IN_COLS = _OFF[-1]

LANES = 128
V7X_VMEM_BYTES = 64 * 2 ** 20
HALO = 16

TM_PROJ = 256
TM_POST = 256
TM_MOE = 512
TL_CONV = 256
TQ_GQA = 256


def _cparams(sem, vmem_mib):
    assert vmem_mib * 2 ** 20 < V7X_VMEM_BYTES
    return pltpu.CompilerParams(dimension_semantics=sem, vmem_limit_bytes=vmem_mib * 2 ** 20)


def _resident(shape):
    nd = len(shape)
    return pl.BlockSpec(shape, lambda *_: (0,) * nd, pipeline_mode=pl.Buffered(1))


def _dot(a, b):
    return jnp.dot(a, b, preferred_element_type=F32)


def _dot_nt(a, b):
    return lax.dot_general(a, b, (((1,), (1,)), ((), ())), preferred_element_type=F32)


def _split_bf16(a):
    hi = a.astype(BF16)
    lo = (a - hi.astype(F32)).astype(BF16)
    return hi, lo


def _dot3(a, b):
    a_hi, a_lo = _split_bf16(a)
    b_hi, b_lo = _split_bf16(b)
    return _dot(a_hi, b_hi) + _dot(a_lo, b_hi) + _dot(a_hi, b_lo)


def _sigmoid(x):
    return 1.0 / (1.0 + jnp.exp(-x))


def _layer_norm(x, g, b):
    mu = jnp.mean(x, axis=-1, keepdims=True)
    xc = x - mu
    var = jnp.mean(xc * xc, axis=-1, keepdims=True)
    return xc * lax.rsqrt(var + LN_EPS) * g + b


def _mod_kernel(cond_ref, w_ref, b_ref, o_ref):
    c = cond_ref[...]
    o_ref[...] = _dot3(c * _sigmoid(c), w_ref[...]) + b_ref[...]


def _modulation(cond8, w_mod, b_mod):
    tn = 1536
    n = w_mod.shape[-1]
    return pl.pallas_call(
        _mod_kernel,
        grid=(DEPTH, n // tn),
        in_specs=[
            pl.BlockSpec((8, D_MODEL), lambda l, j: (0, 0)),
            pl.BlockSpec((None, D_MODEL, tn), lambda l, j: (l, 0, j)),
            pl.BlockSpec((None, 1, tn), lambda l, j: (l, 0, j)),
        ],
        out_specs=pl.BlockSpec((None, 8, tn), lambda l, j: (l, 0, j)),
        out_shape=jax.ShapeDtypeStruct((DEPTH, 8, n), F32),
        compiler_params=_cparams(("arbitrary", "arbitrary"), 40),
        name="modulation",
    )(cond8, w_mod, b_mod.reshape(DEPTH, 1, n))


def _head_rms(x, gmat, gain):
    hi, lo = _split_bf16(x * x)
    ms = _dot(hi, gmat) + _dot(lo, gmat)
    return x * lax.rsqrt(ms + RMS_EPS) * gain


def _rope(x, c, s):
    w = x.shape[1]
    reps = w // LANES
    if reps > 1:
        c = jnp.concatenate([c] * reps, axis=1)
        s = jnp.concatenate([s] * reps, axis=1)
    lane = lax.broadcasted_iota(jnp.int32, x.shape, 1)
    partner = jnp.where((lane % 32) < 16, pltpu.roll(x, w - 16, 1), pltpu.roll(x, 16, 1))
    return x * c + partner * s


def _moe_residual_norm(x1_ref, y_ref, g2_ref, lg_ref, lb_ref):
    tm = x1_ref.shape[0]
    y = jnp.concatenate([y_ref[pl.ds(j, tm, stride=Y_ROWS), :] for j in range(Y_ROWS)], axis=1)
    return _layer_norm(ALPHA * x1_ref[...] + g2_ref[...] * y, lg_ref[...], lb_ref[...])


def _proj_kernel(latent, fused, *refs):
    refs = list(refs)
    n_in = (5 if fused else 1) + 6 + (2 if latent else 0)
    ins, outs = refs[:n_in], refs[n_in:]
    x = _moe_residual_norm(*ins[:5]) if fused else ins[0][...]
    sc_ref, sh_ref, w_ref, gm_ref, qn_ref, kn_ref = ins[n_in - 6 - (2 if latent else 0):][:6]
    if latent:
        c_ref, s_ref = ins[-2:]
    if fused:
        outs.pop(0)[...] = x
    qa_ref, ka_ref, va_ref, u_ref, qc_ref, kc_ref, vc_ref, gt_ref = outs
    h = (x * sc_ref[...] + sh_ref[...]).astype(BF16)

    def mm(i):
        return _dot(h, w_ref[:, _OFF[i]:_OFF[i + 1]])

    qa_ref[...] = (mm(0) * QK_SCALE).astype(qa_ref.dtype)
    ka_ref[...] = mm(1).astype(ka_ref.dtype)
    va_ref[...] = mm(2).astype(va_ref.dtype)
    u_ref[...] = mm(3).astype(u_ref.dtype)
    qc = _head_rms(mm(4), gm_ref[...], qn_ref[...])
    kc = _head_rms(mm(5), gm_ref[:GQ_KV_WIDTH, :GQ_KV_WIDTH], kn_ref[...])
    if latent:
        qc = _rope(qc, c_ref[...], s_ref[...])
        kc = _rope(kc, c_ref[...], s_ref[...])
    qc_ref[...] = (qc * QK_SCALE).astype(qc_ref.dtype)
    kc_ref[...] = kc.astype(kc_ref.dtype)
    vc_ref[...] = mm(6).astype(vc_ref.dtype)
    gt_ref[...] = mm(7).astype(gt_ref.dtype)


def _mod_row_map(latent, tiles_per_seq):
    if latent:
        return lambda i: (1 + i // tiles_per_seq, 0, 0)
    return lambda i: (0, 0, 0)


def _proj(x, scale, shift, w_bf, gmat, qn, kn, rope, *, latent, seq_len):
    fused = isinstance(x, tuple)
    t = (x[0] if fused else x).shape[0]
    tm = TM_PROJ
    kv_dt = BF16 if latent else F32
    row = lambda i: (i, 0)
    mod_map = _mod_row_map(latent, seq_len // tm)
    if fused:
        in_specs = [pl.BlockSpec((tm, D_MODEL), row), pl.BlockSpec((tm * Y_ROWS, LANES), row),
                    pl.BlockSpec((None, 1, D_MODEL), mod_map), _resident((1, D_MODEL)), _resident((1, D_MODEL))]
        args = list(x)
    else:
        in_specs = [pl.BlockSpec((tm, D_MODEL), row)]
        args = [x]
    in_specs += [
        pl.BlockSpec((None, 1, D_MODEL), mod_map),
        pl.BlockSpec((None, 1, D_MODEL), mod_map),
        _resident((D_MODEL, IN_COLS)),
        _resident((GQ_Q_WIDTH, GQ_Q_WIDTH)),
        _resident((1, GQ_Q_WIDTH)),
        _resident((1, GQ_KV_WIDTH)),
    ]
    args += [scale, shift, w_bf, gmat, qn, kn]
    if latent:
        per = seq_len // tm
        in_specs += [pl.BlockSpec((tm, LANES), lambda i: (i % per, 0))] * 2
        args += list(rope)
    widths = (NA_WIDTH, NA_WIDTH, NA_WIDTH, 2 * CONV_CH, GQ_Q_WIDTH, GQ_KV_WIDTH, GQ_KV_WIDTH, 3 * D_MODEL)
    dtypes = (BF16, kv_dt, kv_dt, BF16, BF16, kv_dt, kv_dt, BF16)
    if fused:
        widths = (D_MODEL,) + widths
        dtypes = (F32,) + dtypes
    return pl.pallas_call(
        functools.partial(_proj_kernel, latent, fused),
        grid=(t // tm,),
        in_specs=in_specs,
        out_specs=[pl.BlockSpec((tm, w), row) for w in widths],
        out_shape=[jax.ShapeDtypeStruct((t, w), dt) for w, dt in zip(widths, dtypes)],
        compiler_params=_cparams(("arbitrary",), 48),
        name="proj_lat" if latent else "proj_ctx",
    )(*args)


def _pair(ref_or_val, j):
    return ref_or_val[:, j * LANES:(j + 1) * LANES]


def _keep_half(x, half, move_to=None):
    lane = lax.broadcasted_iota(jnp.int32, x.shape, 1)
    keep = (lane >= HEAD_DIM) if half else (lane < HEAD_DIM)
    if move_to is None or move_to == half:
        return jnp.where(keep, x, jnp.zeros_like(x))
    xf = jnp.where(keep, x.astype(F32), 0.0)
    return pltpu.roll(xf, HEAD_DIM, 1).astype(x.dtype)


def _attend_t(qp, pieces):
    scores = []
    for k, _, bias in pieces:
        s = _dot_nt(k, qp)
        scores.append(s if bias is None else s + bias)
    m = None
    for s in scores:
        mi = jnp.max(s, axis=0, keepdims=True)
        m = mi if m is None else jnp.maximum(m, mi)
    l = 0.0
    o = 0.0
    for s, (_, vt, _) in zip(scores, pieces):
        p = jnp.exp(s - m)
        l = l + jnp.sum(p, axis=0, keepdims=True)
        o = o + _dot(vt, p.astype(BF16))
    return o / l


def _select_rows(o_even, o_odd):
    return jnp.concatenate([o_even[:HEAD_DIM], o_odd[HEAD_DIM:]], axis=0)


def _ctx_attn_kernel(qa_ref, ka_ref, va_ref, qc_ref, kc_ref, vc_ref, oa_ref, oc_ref):
    ka = ka_ref[...].astype(BF16)
    vat = va_ref[...].T.astype(BF16)
    outs = []
    for j in range(NA_HEADS // 2):
        k = _pair(ka, j)
        vt = vat[j * LANES:(j + 1) * LANES]
        q = _pair(qa_ref, j)
        outs.append(_select_rows(_attend_t(_keep_half(q, 0), [(k, vt, None)]),
                                 _attend_t(_keep_half(q, 1), [(k, vt, None)])))
    oa_ref[...] = jnp.concatenate(outs, axis=0).T.astype(oa_ref.dtype)
    kc = kc_ref[...].astype(BF16)
    vct = vc_ref[...].T.astype(BF16)
    outs = []
    for h in range(GQ_HEADS):
        g = h // (GQ_HEADS // GQ_KV_HEADS)
        o = _attend_t(_keep_half(_pair(qc_ref, h // 2), h % 2, move_to=g), [(kc, vct, None)])
        outs.append(o[g * HEAD_DIM:(g + 1) * HEAD_DIM])
    oc_ref[...] = jnp.concatenate(outs, axis=0).T.astype(oc_ref.dtype)


def _ctx_attention(qa, ka, va, qc, kc, vc, *, seq_len):
    t = qa.shape[0]
    row = lambda i: (i, 0)
    spec = lambda w: pl.BlockSpec((seq_len, w), row)
    return pl.pallas_call(
        _ctx_attn_kernel,
        grid=(t // seq_len,),
        in_specs=[spec(NA_WIDTH), spec(NA_WIDTH), spec(NA_WIDTH), spec(GQ_Q_WIDTH), spec(GQ_KV_WIDTH), spec(GQ_KV_WIDTH)],
        out_specs=[spec(NA_WIDTH), spec(GQ_Q_WIDTH)],
        out_shape=[jax.ShapeDtypeStruct((t, NA_WIDTH), BF16), jax.ShapeDtypeStruct((t, GQ_Q_WIDTH), BF16)],
        compiler_params=_cparams(("arbitrary",), 32),
        name="attn_ctx",
    )(qa, ka, va, qc, kc, vc)


def _na_key_row_start(blk, rows):
    return jnp.clip(blk * NA_QROWS - NA_KH // 2, 0, rows - NA_WIN_ROWS)


def _na_lat_kernel(rows, q_ref, k_ref, vt_ref, kc_ref, vct_ref, bias_ref, o_ref):
    base = pl.multiple_of(_na_key_row_start(pl.program_id(1), rows) * GRID_W, 2 * LANES)
    nk = NA_WIN_ROWS * GRID_W
    kwin = k_ref[pl.ds(base, nk), :]
    outs = []
    for j in range(NA_HEADS // 2):
        k = _pair(kwin, j)
        vt = vt_ref[j * LANES:(j + 1) * LANES, pl.ds(base, nk)]
        kc = _pair(kc_ref, j)
        vct = vct_ref[j * LANES:(j + 1) * LANES, :]
        q = _pair(q_ref, j)
        halves = [_attend_t(_keep_half(q, half), [(k, vt, bias_ref[2 * j + half]), (kc, vct, None)])
                  for half in range(2)]
        outs.append(_select_rows(*halves))
    o_ref[...] = jnp.concatenate(outs, axis=0).T.astype(o_ref.dtype)


def _na_latent(q, k, vt, k_ctx, vt_ctx, bias, *, n_seq, seq_len):
    rows = seq_len // GRID_W
    nblk = rows // NA_QROWS
    tq = NA_QROWS * GRID_W
    pattern = lambda b, i: (jnp.where(i == 0, 0, jnp.where(i == nblk - 1, 2, 1)), 0, 0, 0)
    return pl.pallas_call(
        functools.partial(_na_lat_kernel, rows),
        grid=(n_seq, nblk),
        in_specs=[
            pl.BlockSpec((tq, NA_WIDTH), lambda b, i: (b * nblk + i, 0)),
            pl.BlockSpec((seq_len, NA_WIDTH), lambda b, i: (b, 0)),
            pl.BlockSpec((None, NA_WIDTH, seq_len), lambda b, i: (b, 0, 0)),
            pl.BlockSpec((None,) + k_ctx.shape[1:], lambda b, i: (b, 0, 0)),
            pl.BlockSpec((None,) + vt_ctx.shape[1:], lambda b, i: (b, 0, 0)),
            pl.BlockSpec((None,) + bias.shape[1:], pattern),
        ],
        out_specs=pl.BlockSpec((tq, NA_WIDTH), lambda b, i: (b * nblk + i, 0)),
        out_shape=jax.ShapeDtypeStruct(q.shape, BF16),
        compiler_params=_cparams(("arbitrary", "arbitrary"), 52),
        name="attn_na_lat",
    )(q, k, vt, k_ctx, vt_ctx, bias)


def _gqa_lat_kernel(q_ref, k_ref, v_ref, o_ref):
    k = k_ref[...]
    v = v_ref[...]
    for h in range(GQ_HEADS):
        g = h // (GQ_HEADS // GQ_KV_HEADS)
        s = _dot_nt(_keep_half(_pair(q_ref, h // 2), h % 2, move_to=g), k)
        p = jnp.exp(s - jnp.max(s, axis=-1, keepdims=True))
        o = _dot(p.astype(BF16), v) / jnp.sum(p, axis=-1, keepdims=True)
        o_ref[:, h * HEAD_DIM:(h + 1) * HEAD_DIM] = o[:, g * HEAD_DIM:(g + 1) * HEAD_DIM].astype(o_ref.dtype)


def _gqa_latent(q, k_all, v_all, *, n_seq, seq_len):
    per = seq_len // TQ_GQA
    s_all = k_all.shape[1]
    return pl.pallas_call(
        _gqa_lat_kernel,
        grid=(n_seq, per),
        in_specs=[
            pl.BlockSpec((TQ_GQA, GQ_Q_WIDTH), lambda b, t: (b * per + t, 0)),
            pl.BlockSpec((None, s_all, GQ_KV_WIDTH), lambda b, t: (b, 0, 0)),
            pl.BlockSpec((None, s_all, GQ_KV_WIDTH), lambda b, t: (b, 0, 0)),
        ],
        out_specs=pl.BlockSpec((TQ_GQA, GQ_Q_WIDTH), lambda b, t: (b * per + t, 0)),
        out_shape=jax.ShapeDtypeStruct(q.shape, BF16),
        compiler_params=_cparams(("arbitrary", "arbitrary"), 48),
        name="attn_gqa_lat",
    )(q, k_all, v_all)


def _conv_kernel(u_ref, up_ref, un_ref, w_ref, b_ref, g_ref, beta_ref, o_ref, shifted_ref):
    t = pl.program_id(1)
    nt = pl.num_programs(1)
    tl = u_ref.shape[0]

    def glu(v):
        return v[:, :CONV_CH].astype(F32) * _sigmoid(v[:, CONV_CH:].astype(F32))

    slab = jnp.concatenate([jnp.where(t > 0, glu(up_ref[...]), 0.0), glu(u_ref[...]),
                            jnp.where(t < nt - 1, glu(un_ref[...]), 0.0)], axis=0)
    first = HALO - CONV_K // 2
    span = shifted_ref.shape[1]
    assert (first + CONV_K - 1) // 8 * 8 + tl == span and span + 7 <= slab.shape[0]
    acc = jnp.zeros((tl, CONV_CH), F32)
    for shift in range(8):
        shifted_ref[shift % 2] = slab[shift:shift + span]
        for k in range(CONV_K):
            if (first + k) % 8 == shift:
                base = first + k - shift
                acc = acc + shifted_ref[shift % 2, base:base + tl, :] * w_ref[k:k + 1, :]
    y = _layer_norm(acc + b_ref[...], g_ref[...], beta_ref[...])
    o_ref[...] = (y * _sigmoid(y)).astype(o_ref.dtype)


def _conformer_conv(u, conv_w, conv_b, ln_g, ln_b, *, n_seq, seq_len):
    tl = TL_CONV
    nt = seq_len // tl
    hb = tl // HALO
    last_hb = seq_len // HALO - 1
    u3 = u.reshape(n_seq, seq_len, 2 * CONV_CH)
    vec = lambda a: a.reshape(1, CONV_CH)
    out = pl.pallas_call(
        _conv_kernel,
        grid=(n_seq, nt),
        in_specs=[
            pl.BlockSpec((None, tl, 2 * CONV_CH), lambda s, t: (s, t, 0)),
            pl.BlockSpec((None, HALO, 2 * CONV_CH), lambda s, t: (s, jnp.maximum(t * hb - 1, 0), 0)),
            pl.BlockSpec((None, HALO, 2 * CONV_CH), lambda s, t: (s, jnp.minimum((t + 1) * hb, last_hb), 0)),
            _resident((CONV_K, CONV_CH)),
            _resident((1, CONV_CH)),
            _resident((1, CONV_CH)),
            _resident((1, CONV_CH)),
        ],
        out_specs=pl.BlockSpec((None, tl, CONV_CH), lambda s, t: (s, t, 0)),
        out_shape=jax.ShapeDtypeStruct((n_seq, seq_len, CONV_CH), BF16),
        scratch_shapes=[pltpu.VMEM((2, (HALO + CONV_K // 2) // 8 * 8 + tl, CONV_CH), F32)],
        compiler_params=_cparams(("arbitrary", "arbitrary"), 32),
        name="conformer_conv",
    )(u3, u3, u3, conv_w, vec(conv_b), vec(ln_g), vec(ln_b))
    return out.reshape(n_seq * seq_len, CONV_CH)


def _router_gates(logits):
    lane = lax.broadcasted_iota(jnp.int32, logits.shape, 1)
    m = jnp.max(logits, axis=-1, keepdims=True)
    e = jnp.exp(logits - m)
    probs = e / jnp.sum(e, axis=-1, keepdims=True)
    p1 = jnp.max(probs, axis=-1, keepdims=True)
    i1 = jnp.min(jnp.where(probs == p1, lane, LANES), axis=-1, keepdims=True)
    in_group = (lane // EXPERTS_PER_GROUP) == (i1 // EXPERTS_PER_GROUP)
    cand = jnp.where(in_group, jnp.where(lane == i1, -1.0, probs), -1.0)
    p2 = jnp.max(cand, axis=-1, keepdims=True)
    i2 = jnp.min(jnp.where(cand == p2, lane, LANES), axis=-1, keepdims=True)
    den = p1 + p2
    gates = jnp.where(lane == i1, p1 / den, 0.0) + jnp.where(lane == i2, p2 / den, 0.0)
    group = (i1 // EXPERTS_PER_GROUP).astype(F32)
    return jnp.where(lane == GROUP_LANE, group, gates)


def _post_kernel(oa_ref, ob_ref, oc_ref, gt_ref, x_ref, g1_ref, lg_ref, lb_ref, sc2_ref, sh2_ref,
                 wa_ref, wb_ref, wc_ref, wo_ref, wr_ref, br_ref, x1_ref, hg_ref, gate_ref):
    sg = _sigmoid(gt_ref[...].astype(F32))
    merged = (sg[:, :D_MODEL] * _dot(oa_ref[...], wa_ref[...])
              + sg[:, D_MODEL:2 * D_MODEL] * _dot(ob_ref[...], wb_ref[...])
              + sg[:, 2 * D_MODEL:] * _dot(oc_ref[...], wc_ref[...]))
    y = _dot(merged.astype(BF16), wo_ref[...])
    x1 = _layer_norm(ALPHA * x_ref[...] + g1_ref[...] * y, lg_ref[...], lb_ref[...])
    x1_ref[...] = x1
    h2 = x1 * sc2_ref[...] + sh2_ref[...]
    tm = x1.shape[0]
    for j in range(Y_ROWS):
        hg_ref[pl.ds(j, tm, stride=HG_ROWS), :] = h2[:, j * LANES:(j + 1) * LANES]
    gates = _router_gates(_dot3(h2, wr_ref[...]) + br_ref[...])
    hg_ref[pl.ds(Y_ROWS, tm, stride=HG_ROWS), :] = gates
    gate_ref[...] = gates
    for j in range(Y_ROWS + 1, HG_ROWS):
        hg_ref[pl.ds(j, tm, stride=HG_ROWS), :] = jnp.zeros((tm, LANES), F32)


def _post(oa, ob, oc, gates, x, g1, ln_g, ln_b, scale2, shift2, wa, wb, wc, wo, wr, br, *, latent, seq_len):
    t = x.shape[0]
    tm = TM_POST
    row = lambda i: (i, 0)
    mod_map = _mod_row_map(latent, seq_len // tm)
    mod = pl.BlockSpec((None, 1, D_MODEL), mod_map)
    return pl.pallas_call(
        _post_kernel,
        grid=(t // tm,),
        in_specs=[
            pl.BlockSpec((tm, NA_WIDTH), row), pl.BlockSpec((tm, CONV_CH), row), pl.BlockSpec((tm, GQ_Q_WIDTH), row),
            pl.BlockSpec((tm, 3 * D_MODEL), row), pl.BlockSpec((tm, D_MODEL), row),
            mod, _resident((1, D_MODEL)), _resident((1, D_MODEL)), mod, mod,
            _resident(wa.shape), _resident(wb.shape), _resident(wc.shape), _resident(wo.shape),
            _resident(wr.shape), _resident(br.shape),
        ],
        out_specs=[pl.BlockSpec((tm, D_MODEL), row), pl.BlockSpec((tm * HG_ROWS, LANES), row),
                   pl.BlockSpec((tm, LANES), row)],
        out_shape=[jax.ShapeDtypeStruct((t, D_MODEL), F32), jax.ShapeDtypeStruct((t * HG_ROWS, LANES), F32),
                   jax.ShapeDtypeStruct((t, LANES), F32)],
        compiler_params=_cparams(("arbitrary",), 40),
        name="post_lat" if latent else "post_ctx",
    )(oa, ob, oc, gates, x, g1, ln_g, ln_b, scale2, shift2, wa, wb, wc, wo, wr, br)


def _route(grp, tm):
    n = grp.shape[0]
    n_tiles = n // tm + N_GROUPS
    groups = jnp.arange(N_GROUPS, dtype=jnp.int32)
    onehot = (grp[:, None] == groups[None, :]).astype(jnp.int32)
    csum = jnp.cumsum(onehot, axis=0)
    rank = jnp.sum((csum - onehot) * onehot, axis=1)
    counts = csum[-1]
    tiles = (counts + tm - 1) // tm
    tile_end = jnp.cumsum(tiles)
    tile_start = tile_end - tiles
    slot = jnp.sum(onehot * tile_start[None, :], axis=1) * tm + rank
    tile_ids = jnp.arange(n_tiles, dtype=jnp.int32)
    tile_group = jnp.minimum(jnp.sum((tile_ids[:, None] >= tile_end[None, :]).astype(jnp.int32), axis=1), N_GROUPS - 1)
    src = jnp.zeros((n_tiles * tm,), jnp.int32).at[slot].set(jnp.arange(n, dtype=jnp.int32))
    r = jnp.arange(n_tiles * tm, dtype=jnp.int32)
    slot_tile = r // tm
    in_group = (jnp.repeat(tile_group, tm)[:, None] == groups[None, :]).astype(jnp.int32)
    slot_rank = r - jnp.sum(in_group * tile_start[None, :], axis=1) * tm
    valid = (slot_tile < tile_end[-1]) & (slot_rank < jnp.sum(in_group * counts[None, :], axis=1))
    dst = jnp.where(valid, src, n + (slot_tile % 2) * tm + r % tm)
    return tile_group.astype(jnp.int32), src, dst, tile_end[-1:].astype(jnp.int32)


def _moe_kernel(tg_ref, src_ref, dst_ref, nused_ref, h_hbm, wgu_ref, wd_ref, y_hbm, hbuf, ybuf, gsem, ssem):
    t = pl.program_id(0)
    n_used = nused_ref[0]
    slot = t % 2
    tm = ybuf.shape[1] // Y_ROWS

    def gather_row(tile, s, r):
        tok = src_ref[tile * tm + r]
        return pltpu.make_async_copy(h_hbm.at[pl.ds(pl.multiple_of(tok * HG_ROWS, HG_ROWS), HG_USED), :],
                                     hbuf.at[s, pl.ds(pl.multiple_of(r * HG_ROWS, HG_ROWS), HG_USED), :], gsem.at[s])

    def scatter_row(tile, s, r):
        row = dst_ref[tile * tm + r]
        return pltpu.make_async_copy(ybuf.at[s, pl.ds(pl.multiple_of(r * Y_ROWS, Y_ROWS), Y_ROWS), :],
                                     y_hbm.at[pl.ds(pl.multiple_of(row * Y_ROWS, Y_ROWS), Y_ROWS), :], ssem.at[s])

    def for_rows(fn):
        def body(r, carry):
            fn(r)
            return carry
        lax.fori_loop(0, tm, body, 0, unroll=8)

    @pl.when(t == 0)
    def _():
        for_rows(lambda r: gather_row(0, 0, r).start())

    @pl.when(t + 1 < n_used)
    def _():
        for_rows(lambda r: gather_row(t + 1, 1 - slot, r).start())

    @pl.when(t < n_used)
    def _():
        for_rows(lambda r: gather_row(t, slot, r).wait())

        @pl.when(t >= 2)
        def _():
            for_rows(lambda r: scatter_row(t - 2, slot, r).wait())

        first = tg_ref[t] * EXPERTS_PER_GROUP
        token_rows = lambda ref, j, per: ref[slot, pl.ds(j, tm, stride=per), :]
        h = jnp.concatenate([token_rows(hbuf, j, HG_ROWS) for j in range(Y_ROWS)], axis=1).astype(BF16)
        gates = token_rows(hbuf, Y_ROWS, HG_ROWS)
        lane = lax.broadcasted_iota(jnp.int32, gates.shape, 1)
        acc = jnp.zeros((tm, D_MODEL), F32)
        for e in range(EXPERTS_PER_GROUP):
            gu = _dot(h, wgu_ref[e])
            a = gu[:, :D_EXPERT]
            ge = jnp.sum(jnp.where(lane == first + e, gates, 0.0), axis=-1, keepdims=True)
            hid = a * _sigmoid(a) * gu[:, D_EXPERT:] * ge
            acc = acc + _dot(hid.astype(BF16), wd_ref[e])
        for j in range(Y_ROWS):
            ybuf[slot, pl.ds(j, tm, stride=Y_ROWS), :] = acc[:, j * LANES:(j + 1) * LANES]
        for_rows(lambda r: scatter_row(t, slot, r).start())

        @pl.when(t == n_used - 1)
        def _():
            for_rows(lambda r: scatter_row(t, slot, r).wait())

            @pl.when(t >= 1)
            def _():
                for_rows(lambda r: scatter_row(t - 1, 1 - slot, r).wait())

            n_rows = y_hbm.shape[0] - 2 * tm * Y_ROWS
            ybuf[0] = jnp.zeros(ybuf.shape[1:], F32)
            for half in range(2):
                fill = pltpu.make_async_copy(
                    ybuf.at[0], y_hbm.at[pl.ds(n_rows + half * tm * Y_ROWS, tm * Y_ROWS), :], ssem.at[half])
                fill.start()
                fill.wait()


def _moe_experts(hg, route, wgu, wd):
    n = hg.shape[0] // HG_ROWS
    tm = TM_MOE
    tile_group, src, dst, n_used = route
    n_tiles = tile_group.shape[0]
    grid_spec = pltpu.PrefetchScalarGridSpec(
        num_scalar_prefetch=4,
        grid=(n_tiles,),
        in_specs=[
            pl.BlockSpec(memory_space=pl.ANY),
            pl.BlockSpec((EXPERTS_PER_GROUP, D_MODEL, 2 * D_EXPERT), lambda t, tg, s, d, nu: (tg[t], 0, 0)),
            pl.BlockSpec((EXPERTS_PER_GROUP, D_EXPERT, D_MODEL), lambda t, tg, s, d, nu: (tg[t], 0, 0)),
        ],
        out_specs=pl.BlockSpec(memory_space=pl.ANY),
        scratch_shapes=[
            pltpu.VMEM((2, tm * HG_ROWS, LANES), F32),
            pltpu.VMEM((2, tm * Y_ROWS, LANES), F32),
            pltpu.SemaphoreType.DMA((2,)),
            pltpu.SemaphoreType.DMA((2,)),
        ],
    )
    return pl.pallas_call(
        _moe_kernel,
        grid_spec=grid_spec,
        out_shape=jax.ShapeDtypeStruct(((n + 2 * tm) * Y_ROWS, LANES), F32),
        compiler_params=_cparams(("arbitrary",), 58),
        name="moe_experts",
    )(tile_group, src, dst, n_used, hg, wgu, wd)


def _moe_out_kernel(x1_ref, y_ref, g2_ref, lg_ref, lb_ref, o_ref):
    o_ref[...] = _moe_residual_norm(x1_ref, y_ref, g2_ref, lg_ref, lb_ref)


def _moe_out(x1, y, g2, ln_g, ln_b, *, latent, seq_len):
    t = x1.shape[0]
    tm = TM_POST
    row = lambda i: (i, 0)
    return pl.pallas_call(
        _moe_out_kernel,
        grid=(t // tm,),
        in_specs=[pl.BlockSpec((tm, D_MODEL), row), pl.BlockSpec((tm * Y_ROWS, LANES), row),
                  pl.BlockSpec((None, 1, D_MODEL), _mod_row_map(latent, seq_len // tm)),
                  _resident((1, D_MODEL)), _resident((1, D_MODEL))],
        out_specs=pl.BlockSpec((tm, D_MODEL), row),
        out_shape=jax.ShapeDtypeStruct((t, D_MODEL), F32),
        compiler_params=_cparams(("arbitrary",), 32),
        name="moe_out",
    )(x1, y, g2, ln_g, ln_b)


def _moe(hg, gates, wgu, wd):
    return _moe_experts(hg, _route(gates[:, GROUP_LANE].astype(jnp.int32), TM_MOE), wgu, wd)


def _head_mean_matrix():
    i = jnp.arange(GQ_Q_WIDTH) // HEAD_DIM
    return jnp.where(i[:, None] == i[None, :], 1.0 / HEAD_DIM, 0.0).astype(BF16)


def _rope_tables(n_tokens):
    t = jnp.arange(n_tokens, dtype=jnp.int32)
    axis_dim = HEAD_DIM // 2
    inv_freq = ROPE_THETA ** (-jnp.arange(0, axis_dim, 2, dtype=F32) / axis_dim)
    ar = (t // GRID_W).astype(F32)[:, None] * inv_freq
    ac = (t % GRID_W).astype(F32)[:, None] * inv_freq
    c = jnp.concatenate([jnp.cos(ar), jnp.cos(ar), jnp.cos(ac), jnp.cos(ac)], axis=1)
    s = jnp.concatenate([-jnp.sin(ar), jnp.sin(ar), -jnp.sin(ac), jnp.sin(ac)], axis=1)
    return jnp.tile(c, (1, LANES // HEAD_DIM)), jnp.tile(s, (1, LANES // HEAD_DIM))


def _na_bias_table(rpb, rows):
    col = np.arange(GRID_W)
    col_idx = np.clip(col[:, None] - col[None, :], -(NA_KW - 1), NA_KW - 1) + NA_KW - 1
    onehot = (col_idx[None] == np.arange(2 * NA_KW - 1)[:, None, None]).astype(np.float32)
    col_start = np.clip(col - NA_KW // 2, 0, GRID_W - NA_KW)
    col_ok = (col[:, None] >= col_start[None, :]) & (col[:, None] < col_start[None, :] + NA_KW)
    t = jnp.einsum("hic,ckq->hikq", rpb, onehot, precision=lax.Precision.HIGHEST)
    t = jnp.where(col_ok[None, None], t, NEG_INF)
    masked = jnp.full((NA_HEADS, GRID_W, GRID_W), NEG_INF, F32)
    pats = []
    for first_qrow in (0, NA_KH // 2, rows - NA_QROWS):
        krow0 = int(np.clip(first_qrow - NA_KH // 2, 0, rows - NA_WIN_ROWS))
        krows = []
        for kj in range(NA_WIN_ROWS):
            blocks = []
            for qi in range(NA_QROWS):
                qrow, krow = first_qrow + qi, krow0 + kj
                win0 = int(np.clip(qrow - NA_KH // 2, 0, rows - NA_KH))
                in_window = win0 <= krow < win0 + NA_KH
                blocks.append(t[:, krow - qrow + NA_KH - 1] if in_window else masked)
            krows.append(jnp.concatenate(blocks, axis=-1))
        pats.append(jnp.concatenate(krows, axis=-2))
    return jnp.stack(pats)


def _stream_layer(x, mods, l, wts, *, latent, n_seq, seq_len, cache=None, rope=None):
    sh1, sc1, g1, sh2, sc2, g2 = mods
    pieces = _proj(x, 1.0 + sc1, sh1, wts["w_in"][l], wts["gmat"], wts["qn"][l], wts["kn"][l], rope,
                   latent=latent, seq_len=seq_len)
    if isinstance(x, tuple):
        x, pieces = pieces[0], pieces[1:]
    qa, ka, va, u, qc, kc, vc, gates = pieces
    if latent:
        nat_k, nat_v, gqa_k, gqa_v = cache
        swap = lambda a: jnp.swapaxes(a, 1, 2)
        out_a = _na_latent(qa, ka, swap(va.reshape(n_seq, seq_len, NA_WIDTH)), nat_k, swap(nat_v),
                           wts["na_bias"][l], n_seq=n_seq, seq_len=seq_len)
        k_all = jnp.concatenate([gqa_k, kc.reshape(n_seq, seq_len, GQ_KV_WIDTH)], axis=1)
        v_all = jnp.concatenate([gqa_v, vc.reshape(n_seq, seq_len, GQ_KV_WIDTH)], axis=1)
        out_c = _gqa_latent(qc, k_all, v_all, n_seq=n_seq, seq_len=seq_len)
    else:
        out_a, out_c = _ctx_attention(qa, ka, va, qc, kc, vc, seq_len=seq_len)
    out_b = _conformer_conv(u, wts["conv_w"][l], wts["conv_b"][l], wts["conv_ln_g"][l], wts["conv_ln_b"][l],
                            n_seq=n_seq, seq_len=seq_len)
    x1, hg, moe_gates = _post(out_a, out_b, out_c, gates, x, g1, wts["ln1_g"][l], wts["ln1_b"][l], 1.0 + sc2, sh2,
                         wts["w_br_a"][l], wts["w_br_b"][l], wts["w_br_c"][l], wts["w_out"][l],
                         wts["w_router"], wts["b_router"], latent=latent, seq_len=seq_len)
    y = _moe(hg, moe_gates, wts["w_gate_up"][l], wts["w_down"][l])
    return (x1, y, g2, wts["ln2_g"][l], wts["ln2_b"][l]), (ka, va, kc, vc)


def kernel(x_prompt, x_sample, cache_nat_k, cache_nat_v, cache_gqa_k, cache_gqa_v, c, c_ctx, w_mod, b_mod, w_in, nat_rpb, conv_w, conv_b, conv_ln_g, conv_ln_b, q_norm_g, k_norm_g, w_br_a, w_br_b, w_br_c, w_out, ln1_g, ln1_b, ln2_g, ln2_b, w_router, b_router, w_gate_up, w_down):
    batch, seq, _ = x_prompt.shape
    dec_batch, dec_seq, _ = x_sample.shape
    past = cache_nat_k.shape[2]
    assert dec_batch == 2 and seq % TM_PROJ == 0 and dec_seq % TM_MOE == 0

    cond8 = jnp.zeros((8, D_MODEL), F32).at[0].set(c_ctx).at[1:1 + dec_batch].set(c)
    mods_all = _modulation(cond8, w_mod, b_mod)

    pad = LANES - N_EXPERTS
    vecd = lambda a: a.reshape(DEPTH, 1, -1)
    wts = dict(
        w_in=w_in.astype(BF16),
        gmat=_head_mean_matrix(),
        qn=jnp.tile(q_norm_g, (1, GQ_HEADS)).reshape(DEPTH, 1, GQ_Q_WIDTH),
        kn=jnp.tile(k_norm_g, (1, GQ_KV_HEADS)).reshape(DEPTH, 1, GQ_KV_WIDTH),
        na_bias=jnp.stack([_na_bias_table(nat_rpb[l], dec_seq // GRID_W) for l in range(DEPTH)]),
        conv_w=conv_w, conv_b=conv_b, conv_ln_g=conv_ln_g, conv_ln_b=conv_ln_b,
        ln1_g=vecd(ln1_g), ln1_b=vecd(ln1_b), ln2_g=vecd(ln2_g), ln2_b=vecd(ln2_b),
        w_br_a=w_br_a.astype(BF16), w_br_b=w_br_b.astype(BF16), w_br_c=w_br_c.astype(BF16),
        w_out=w_out.astype(BF16),
        w_router=jnp.pad(w_router, ((0, 0), (0, pad))),
        b_router=jnp.pad(b_router, (0, pad), constant_values=NEG_INF).reshape(1, LANES),
        w_gate_up=w_gate_up.astype(BF16), w_down=w_down.astype(BF16),
    )
    rope = _rope_tables(dec_seq)

    def layer_mods(l):
        m = mods_all[l, :1 + dec_batch].reshape(1 + dec_batch, 6, 1, D_MODEL)
        return [m[:, i] for i in range(6)]

    xp = x_prompt.reshape(batch * seq, D_MODEL)
    xs = x_sample.reshape(dec_batch * dec_seq, D_MODEL)
    new_cache = []
    for l in range(DEPTH):
        mods = layer_mods(l)
        xp, ctx_kv = _stream_layer(xp, mods, l, wts, latent=False, n_seq=batch, seq_len=seq)
        new_cache.append(ctx_kv)
        cache = (cache_nat_k[:, l].reshape(dec_batch, past, NA_WIDTH).astype(BF16),
                 cache_nat_v[:, l].reshape(dec_batch, past, NA_WIDTH).astype(BF16),
                 cache_gqa_k[:, l].reshape(dec_batch, past, GQ_KV_WIDTH).astype(BF16),
                 cache_gqa_v[:, l].reshape(dec_batch, past, GQ_KV_WIDTH).astype(BF16))
        xs, _ = _stream_layer(xs, mods, l, wts, latent=True, n_seq=dec_batch, seq_len=dec_seq, cache=cache, rope=rope)
    xp = _moe_out(*xp, latent=False, seq_len=seq)
    xs = _moe_out(*xs, latent=True, seq_len=dec_seq)

    def stack(i, heads):
        return jnp.stack([kv[i].reshape(batch, seq, heads, HEAD_DIM) for kv in new_cache], axis=1)

    return (xp.reshape(batch, seq, D_MODEL), xs.reshape(dec_batch, dec_seq, D_MODEL),
            stack(0, NA_HEADS), stack(1, NA_HEADS), stack(2, GQ_KV_HEADS), stack(3, GQ_KV_HEADS))
```

```python
import functools

import numpy as np
import jax
import jax.numpy as jnp
from jax import lax
from jax.experimental import pallas as pl
from jax.experimental.pallas import tpu as pltpu

F32 = jnp.float32
BF16 = jnp.bfloat16

D_MODEL = 1024
DEPTH = 2
GRID_W = 64
HEAD_DIM = 64
NA_HEADS = 8
NA_WIDTH = NA_HEADS * HEAD_DIM
NA_KH = 8
NA_KW = 16
NA_QROWS = 4
NA_WIN_ROWS = NA_QROWS + NA_KH
CONV_CH = 512
CONV_K = 31
GQ_HEADS = 8
GQ_KV_HEADS = 2
GQ_Q_WIDTH = GQ_HEADS * HEAD_DIM
GQ_KV_WIDTH = GQ_KV_HEADS * HEAD_DIM
ROPE_THETA = 10000.0
N_EXPERTS = 16
EXPERTS_PER_GROUP = 4
N_GROUPS = N_EXPERTS // EXPERTS_PER_GROUP
GROUP_LANE = N_EXPERTS
Y_ROWS = D_MODEL // 128
HG_ROWS = 2 * Y_ROWS
HG_USED = Y_ROWS + 1
D_EXPERT = 512
ALPHA = (2 * DEPTH) ** 0.25
LN_EPS = 1e-6
RMS_EPS = 1e-6
NEG_INF = -1e30
QK_SCALE = HEAD_DIM ** -0.5

_OFF = (0, 512, 1024, 1536, 2560, 3072, 3200, 3328, 6400)
IN_COLS = _OFF[-1]

LANES = 128
V7X_VMEM_BYTES = 64 * 2 ** 20
HALO = 16

TM_PROJ = 256
TM_POST = 256
TM_MOE = 512
TL_CONV = 256
TQ_GQA = 256


def _cparams(sem, vmem_mib):
    assert vmem_mib * 2 ** 20 < V7X_VMEM_BYTES
    return pltpu.CompilerParams(dimension_semantics=sem, vmem_limit_bytes=vmem_mib * 2 ** 20)


def _resident(shape):
    nd = len(shape)
    return pl.BlockSpec(shape, lambda *_: (0,) * nd, pipeline_mode=pl.Buffered(1))


def _dot(a, b):
    return jnp.dot(a, b, preferred_element_type=F32)


def _dot_nt(a, b):
    return lax.dot_general(a, b, (((1,), (1,)), ((), ())), preferred_element_type=F32)


def _split_bf16(a):
    hi = a.astype(BF16)
    lo = (a - hi.astype(F32)).astype(BF16)
    return hi, lo


def _dot3(a, b):
    a_hi, a_lo = _split_bf16(a)
    b_hi, b_lo = _split_bf16(b)
    return _dot(a_hi, b_hi) + _dot(a_lo, b_hi) + _dot(a_hi, b_lo)


def _sigmoid(x):
    return 1.0 / (1.0 + jnp.exp(-x))


def _layer_norm(x, g, b):
    mu = jnp.mean(x, axis=-1, keepdims=True)
    xc = x - mu
    var = jnp.mean(xc * xc, axis=-1, keepdims=True)
    return xc * lax.rsqrt(var + LN_EPS) * g + b


def _mod_kernel(cond_ref, w_ref, b_ref, o_ref):
    c = cond_ref[...]
    o_ref[...] = _dot3(c * _sigmoid(c), w_ref[...]) + b_ref[...]


def _modulation(cond8, w_mod, b_mod):
    tn = 1536
    n = w_mod.shape[-1]
    return pl.pallas_call(
        _mod_kernel,
        grid=(DEPTH, n // tn),
        in_specs=[
            pl.BlockSpec((8, D_MODEL), lambda l, j: (0, 0)),
            pl.BlockSpec((None, D_MODEL, tn), lambda l, j: (l, 0, j)),
            pl.BlockSpec((None, 1, tn), lambda l, j: (l, 0, j)),
        ],
        out_specs=pl.BlockSpec((None, 8, tn), lambda l, j: (l, 0, j)),
        out_shape=jax.ShapeDtypeStruct((DEPTH, 8, n), F32),
        compiler_params=_cparams(("arbitrary", "arbitrary"), 40),
        name="modulation",
    )(cond8, w_mod, b_mod.reshape(DEPTH, 1, n))


def _head_rms(x, gmat, gain):
    hi, lo = _split_bf16(x * x)
    ms = _dot(hi, gmat) + _dot(lo, gmat)
    return x * lax.rsqrt(ms + RMS_EPS) * gain


def _rope(x, c, s):
    w = x.shape[1]
    reps = w // LANES
    if reps > 1:
        c = jnp.concatenate([c] * reps, axis=1)
        s = jnp.concatenate([s] * reps, axis=1)
    lane = lax.broadcasted_iota(jnp.int32, x.shape, 1)
    partner = jnp.where((lane % 32) < 16, pltpu.roll(x, w - 16, 1), pltpu.roll(x, 16, 1))
    return x * c + partner * s


def _moe_residual_norm(x1_ref, y_ref, g2_ref, lg_ref, lb_ref):
    tm = x1_ref.shape[0]
    y = jnp.concatenate([y_ref[pl.ds(j, tm, stride=Y_ROWS), :] for j in range(Y_ROWS)], axis=1)
    return _layer_norm(ALPHA * x1_ref[...] + g2_ref[...] * y, lg_ref[...], lb_ref[...])


def _proj_kernel(latent, fused, *refs):
    refs = list(refs)
    n_in = (5 if fused else 1) + 6 + (2 if latent else 0)
    ins, outs = refs[:n_in], refs[n_in:]
    x = _moe_residual_norm(*ins[:5]) if fused else ins[0][...]
    sc_ref, sh_ref, w_ref, gm_ref, qn_ref, kn_ref = ins[n_in - 6 - (2 if latent else 0):][:6]
    if latent:
        c_ref, s_ref = ins[-2:]
    if fused:
        outs.pop(0)[...] = x
    qa_ref, ka_ref, va_ref, u_ref, qc_ref, kc_ref, vc_ref, gt_ref = outs
    h = (x * sc_ref[...] + sh_ref[...]).astype(BF16)

    def mm(i):
        return _dot(h, w_ref[:, _OFF[i]:_OFF[i + 1]])

    qa_ref[...] = (mm(0) * QK_SCALE).astype(qa_ref.dtype)
    ka_ref[...] = mm(1).astype(ka_ref.dtype)
    va_ref[...] = mm(2).astype(va_ref.dtype)
    u_ref[...] = mm(3).astype(u_ref.dtype)
    qc = _head_rms(mm(4), gm_ref[...], qn_ref[...])
    kc = _head_rms(mm(5), gm_ref[:GQ_KV_WIDTH, :GQ_KV_WIDTH], kn_ref[...])
    if latent:
        qc = _rope(qc, c_ref[...], s_ref[...])
        kc = _rope(kc, c_ref[...], s_ref[...])
    qc_ref[...] = (qc * QK_SCALE).astype(qc_ref.dtype)
    kc_ref[...] = kc.astype(kc_ref.dtype)
    vc_ref[...] = mm(6).astype(vc_ref.dtype)
    gt_ref[...] = mm(7).astype(gt_ref.dtype)


def _mod_row_map(latent, tiles_per_seq):
    if latent:
        return lambda i: (1 + i // tiles_per_seq, 0, 0)
    return lambda i: (0, 0, 0)


def _proj(x, scale, shift, w_bf, gmat, qn, kn, rope, *, latent, seq_len):
    fused = isinstance(x, tuple)
    t = (x[0] if fused else x).shape[0]
    tm = TM_PROJ
    kv_dt = BF16 if latent else F32
    row = lambda i: (i, 0)
    mod_map = _mod_row_map(latent, seq_len // tm)
    if fused:
        in_specs = [pl.BlockSpec((tm, D_MODEL), row), pl.BlockSpec((tm * Y_ROWS, LANES), row),
                    pl.BlockSpec((None, 1, D_MODEL), mod_map), _resident((1, D_MODEL)), _resident((1, D_MODEL))]
        args = list(x)
    else:
        in_specs = [pl.BlockSpec((tm, D_MODEL), row)]
        args = [x]
    in_specs += [
        pl.BlockSpec((None, 1, D_MODEL), mod_map),
        pl.BlockSpec((None, 1, D_MODEL), mod_map),
        _resident((D_MODEL, IN_COLS)),
        _resident((GQ_Q_WIDTH, GQ_Q_WIDTH)),
        _resident((1, GQ_Q_WIDTH)),
        _resident((1, GQ_KV_WIDTH)),
    ]
    args += [scale, shift, w_bf, gmat, qn, kn]
    if latent:
        per = seq_len // tm
        in_specs += [pl.BlockSpec((tm, LANES), lambda i: (i % per, 0))] * 2
        args += list(rope)
    widths = (NA_WIDTH, NA_WIDTH, NA_WIDTH, 2 * CONV_CH, GQ_Q_WIDTH, GQ_KV_WIDTH, GQ_KV_WIDTH, 3 * D_MODEL)
    dtypes = (BF16, kv_dt, kv_dt, BF16, BF16, kv_dt, kv_dt, BF16)
    if fused:
        widths = (D_MODEL,) + widths
        dtypes = (F32,) + dtypes
    return pl.pallas_call(
        functools.partial(_proj_kernel, latent, fused),
        grid=(t // tm,),
        in_specs=in_specs,
        out_specs=[pl.BlockSpec((tm, w), row) for w in widths],
        out_shape=[jax.ShapeDtypeStruct((t, w), dt) for w, dt in zip(widths, dtypes)],
        compiler_params=_cparams(("arbitrary",), 48),
        name="proj_lat" if latent else "proj_ctx",
    )(*args)


def _pair(ref_or_val, j):
    return ref_or_val[:, j * LANES:(j + 1) * LANES]


def _keep_half(x, half, move_to=None):
    lane = lax.broadcasted_iota(jnp.int32, x.shape, 1)
    keep = (lane >= HEAD_DIM) if half else (lane < HEAD_DIM)
    if move_to is None or move_to == half:
        return jnp.where(keep, x, jnp.zeros_like(x))
    xf = jnp.where(keep, x.astype(F32), 0.0)
    return pltpu.roll(xf, HEAD_DIM, 1).astype(x.dtype)


def _attend_t(qp, pieces):
    scores = []
    for k, _, bias in pieces:
        s = _dot_nt(k, qp)
        scores.append(s if bias is None else s + bias)
    m = None
    for s in scores:
        mi = jnp.max(s, axis=0, keepdims=True)
        m = mi if m is None else jnp.maximum(m, mi)
    l = 0.0
    o = 0.0
    for s, (_, vt, _) in zip(scores, pieces):
        p = jnp.exp(s - m)
        l = l + jnp.sum(p, axis=0, keepdims=True)
        o = o + _dot(vt, p.astype(BF16))
    return o / l


def _select_rows(o_even, o_odd):
    return jnp.concatenate([o_even[:HEAD_DIM], o_odd[HEAD_DIM:]], axis=0)


def _ctx_attn_kernel(qa_ref, ka_ref, va_ref, qc_ref, kc_ref, vc_ref, oa_ref, oc_ref):
    ka = ka_ref[...].astype(BF16)
    vat = va_ref[...].T.astype(BF16)
    outs = []
    for j in range(NA_HEADS // 2):
        k = _pair(ka, j)
        vt = vat[j * LANES:(j + 1) * LANES]
        q = _pair(qa_ref, j)
        outs.append(_select_rows(_attend_t(_keep_half(q, 0), [(k, vt, None)]),
                                 _attend_t(_keep_half(q, 1), [(k, vt, None)])))
    oa_ref[...] = jnp.concatenate(outs, axis=0).T.astype(oa_ref.dtype)
    kc = kc_ref[...].astype(BF16)
    vct = vc_ref[...].T.astype(BF16)
    outs = []
    for h in range(GQ_HEADS):
        g = h // (GQ_HEADS // GQ_KV_HEADS)
        o = _attend_t(_keep_half(_pair(qc_ref, h // 2), h % 2, move_to=g), [(kc, vct, None)])
        outs.append(o[g * HEAD_DIM:(g + 1) * HEAD_DIM])
    oc_ref[...] = jnp.concatenate(outs, axis=0).T.astype(oc_ref.dtype)


def _ctx_attention(qa, ka, va, qc, kc, vc, *, seq_len):
    t = qa.shape[0]
    row = lambda i: (i, 0)
    spec = lambda w: pl.BlockSpec((seq_len, w), row)
    return pl.pallas_call(
        _ctx_attn_kernel,
        grid=(t // seq_len,),
        in_specs=[spec(NA_WIDTH), spec(NA_WIDTH), spec(NA_WIDTH), spec(GQ_Q_WIDTH), spec(GQ_KV_WIDTH), spec(GQ_KV_WIDTH)],
        out_specs=[spec(NA_WIDTH), spec(GQ_Q_WIDTH)],
        out_shape=[jax.ShapeDtypeStruct((t, NA_WIDTH), BF16), jax.ShapeDtypeStruct((t, GQ_Q_WIDTH), BF16)],
        compiler_params=_cparams(("arbitrary",), 32),
        name="attn_ctx",
    )(qa, ka, va, qc, kc, vc)


def _na_key_row_start(blk, rows):
    return jnp.clip(blk * NA_QROWS - NA_KH // 2, 0, rows - NA_WIN_ROWS)


def _na_lat_kernel(rows, q_ref, k_ref, vt_ref, kc_ref, vct_ref, bias_ref, o_ref):
    base = pl.multiple_of(_na_key_row_start(pl.program_id(1), rows) * GRID_W, 2 * LANES)
    nk = NA_WIN_ROWS * GRID_W
    kwin = k_ref[pl.ds(base, nk), :]
    outs = []
    for j in range(NA_HEADS // 2):
        k = _pair(kwin, j)
        vt = vt_ref[j * LANES:(j + 1) * LANES, pl.ds(base, nk)]
        kc = _pair(kc_ref, j)
        vct = vct_ref[j * LANES:(j + 1) * LANES, :]
        q = _pair(q_ref, j)
        halves = [_attend_t(_keep_half(q, half), [(k, vt, bias_ref[2 * j + half]), (kc, vct, None)])
                  for half in range(2)]
        outs.append(_select_rows(*halves))
    o_ref[...] = jnp.concatenate(outs, axis=0).T.astype(o_ref.dtype)


def _na_latent(q, k, vt, k_ctx, vt_ctx, bias, *, n_seq, seq_len):
    rows = seq_len // GRID_W
    nblk = rows // NA_QROWS
    tq = NA_QROWS * GRID_W
    pattern = lambda b, i: (jnp.where(i == 0, 0, jnp.where(i == nblk - 1, 2, 1)), 0, 0, 0)
    return pl.pallas_call(
        functools.partial(_na_lat_kernel, rows),
        grid=(n_seq, nblk),
        in_specs=[
            pl.BlockSpec((tq, NA_WIDTH), lambda b, i: (b * nblk + i, 0)),
            pl.BlockSpec((seq_len, NA_WIDTH), lambda b, i: (b, 0)),
            pl.BlockSpec((None, NA_WIDTH, seq_len), lambda b, i: (b, 0, 0)),
            pl.BlockSpec((None,) + k_ctx.shape[1:], lambda b, i: (b, 0, 0)),
            pl.BlockSpec((None,) + vt_ctx.shape[1:], lambda b, i: (b, 0, 0)),
            pl.BlockSpec((None,) + bias.shape[1:], pattern),
        ],
        out_specs=pl.BlockSpec((tq, NA_WIDTH), lambda b, i: (b * nblk + i, 0)),
        out_shape=jax.ShapeDtypeStruct(q.shape, BF16),
        compiler_params=_cparams(("arbitrary", "arbitrary"), 52),
        name="attn_na_lat",
    )(q, k, vt, k_ctx, vt_ctx, bias)


def _gqa_lat_kernel(q_ref, k_ref, v_ref, o_ref):
    k = k_ref[...]
    for h in range(GQ_HEADS):
        g = h // (GQ_HEADS // GQ_KV_HEADS)
        s = _dot_nt(_keep_half(_pair(q_ref, h // 2), h % 2, move_to=g), k)
        p = jnp.exp(s - jnp.max(s, axis=-1, keepdims=True)).astype(BF16)
        o = _dot(p, v_ref[g])
        o_ref[:, h * HEAD_DIM:(h + 1) * HEAD_DIM] = (o[:, :HEAD_DIM] / o[:, HEAD_DIM:HEAD_DIM + 1]).astype(o_ref.dtype)


def _gqa_values_with_ones(v_all):
    b, s, _ = v_all.shape
    tail = jnp.concatenate([jnp.ones((b, s, 1), v_all.dtype), jnp.zeros((b, s, HEAD_DIM - 1), v_all.dtype)], axis=2)
    return jnp.stack([jnp.concatenate([v_all[:, :, g * HEAD_DIM:(g + 1) * HEAD_DIM], tail], axis=2)
                      for g in range(GQ_KV_HEADS)], axis=1)


def _gqa_latent(q, k_all, v_all, *, n_seq, seq_len):
    per = seq_len // TQ_GQA
    s_all = k_all.shape[1]
    return pl.pallas_call(
        _gqa_lat_kernel,
        grid=(n_seq, per),
        in_specs=[
            pl.BlockSpec((TQ_GQA, GQ_Q_WIDTH), lambda b, t: (b * per + t, 0)),
            pl.BlockSpec((None, s_all, GQ_KV_WIDTH), lambda b, t: (b, 0, 0)),
            pl.BlockSpec((None, GQ_KV_HEADS, s_all, 2 * HEAD_DIM), lambda b, t: (b, 0, 0, 0)),
        ],
        out_specs=pl.BlockSpec((TQ_GQA, GQ_Q_WIDTH), lambda b, t: (b * per + t, 0)),
        out_shape=jax.ShapeDtypeStruct(q.shape, BF16),
        compiler_params=_cparams(("arbitrary", "arbitrary"), 48),
        name="attn_gqa_lat",
    )(q, k_all, _gqa_values_with_ones(v_all))


def _conv_kernel(u_ref, up_ref, un_ref, w_ref, b_ref, g_ref, beta_ref, o_ref, shifted_ref):
    t = pl.program_id(1)
    nt = pl.num_programs(1)
    tl = u_ref.shape[0]

    def glu(v):
        return v[:, :CONV_CH].astype(F32) * _sigmoid(v[:, CONV_CH:].astype(F32))

    slab = jnp.concatenate([jnp.where(t > 0, glu(up_ref[...]), 0.0), glu(u_ref[...]),
                            jnp.where(t < nt - 1, glu(un_ref[...]), 0.0)], axis=0)
    first = HALO - CONV_K // 2
    span = shifted_ref.shape[1]
    assert (first + CONV_K - 1) // 8 * 8 + tl == span and span + 7 <= slab.shape[0]
    acc = jnp.zeros((tl, CONV_CH), F32)
    for shift in range(8):
        shifted_ref[shift % 2] = slab[shift:shift + span]
        for k in range(CONV_K):
            if (first + k) % 8 == shift:
                base = first + k - shift
                acc = acc + shifted_ref[shift % 2, base:base + tl, :] * w_ref[k:k + 1, :]
    y = _layer_norm(acc + b_ref[...], g_ref[...], beta_ref[...])
    o_ref[...] = (y * _sigmoid(y)).astype(o_ref.dtype)


def _conformer_conv(u, conv_w, conv_b, ln_g, ln_b, *, n_seq, seq_len):
    tl = TL_CONV
    nt = seq_len // tl
    hb = tl // HALO
    last_hb = seq_len // HALO - 1
    u3 = u.reshape(n_seq, seq_len, 2 * CONV_CH)
    vec = lambda a: a.reshape(1, CONV_CH)
    out = pl.pallas_call(
        _conv_kernel,
        grid=(n_seq, nt),
        in_specs=[
            pl.BlockSpec((None, tl, 2 * CONV_CH), lambda s, t: (s, t, 0)),
            pl.BlockSpec((None, HALO, 2 * CONV_CH), lambda s, t: (s, jnp.maximum(t * hb - 1, 0), 0)),
            pl.BlockSpec((None, HALO, 2 * CONV_CH), lambda s, t: (s, jnp.minimum((t + 1) * hb, last_hb), 0)),
            _resident((CONV_K, CONV_CH)),
            _resident((1, CONV_CH)),
            _resident((1, CONV_CH)),
            _resident((1, CONV_CH)),
        ],
        out_specs=pl.BlockSpec((None, tl, CONV_CH), lambda s, t: (s, t, 0)),
        out_shape=jax.ShapeDtypeStruct((n_seq, seq_len, CONV_CH), BF16),
        scratch_shapes=[pltpu.VMEM((2, (HALO + CONV_K // 2) // 8 * 8 + tl, CONV_CH), F32)],
        compiler_params=_cparams(("arbitrary", "arbitrary"), 32),
        name="conformer_conv",
    )(u3, u3, u3, conv_w, vec(conv_b), vec(ln_g), vec(ln_b))
    return out.reshape(n_seq * seq_len, CONV_CH)


def _router_gates(logits):
    lane = lax.broadcasted_iota(jnp.int32, logits.shape, 1)
    m = jnp.max(logits, axis=-1, keepdims=True)
    e = jnp.exp(logits - m)
    probs = e / jnp.sum(e, axis=-1, keepdims=True)
    p1 = jnp.max(probs, axis=-1, keepdims=True)
    i1 = jnp.min(jnp.where(probs == p1, lane, LANES), axis=-1, keepdims=True)
    in_group = (lane // EXPERTS_PER_GROUP) == (i1 // EXPERTS_PER_GROUP)
    cand = jnp.where(in_group, jnp.where(lane == i1, -1.0, probs), -1.0)
    p2 = jnp.max(cand, axis=-1, keepdims=True)
    i2 = jnp.min(jnp.where(cand == p2, lane, LANES), axis=-1, keepdims=True)
    den = p1 + p2
    gates = jnp.where(lane == i1, p1 / den, 0.0) + jnp.where(lane == i2, p2 / den, 0.0)
    group = (i1 // EXPERTS_PER_GROUP).astype(F32)
    return jnp.where(lane == GROUP_LANE, group, gates)


def _post_kernel(oa_ref, ob_ref, oc_ref, gt_ref, x_ref, g1_ref, lg_ref, lb_ref, sc2_ref, sh2_ref,
                 wa_ref, wb_ref, wc_ref, wo_ref, wr_ref, br_ref, x1_ref, hg_ref, gate_ref):
    sg = _sigmoid(gt_ref[...].astype(F32))
    merged = (sg[:, :D_MODEL] * _dot(oa_ref[...], wa_ref[...])
              + sg[:, D_MODEL:2 * D_MODEL] * _dot(ob_ref[...], wb_ref[...])
              + sg[:, 2 * D_MODEL:] * _dot(oc_ref[...], wc_ref[...]))
    y = _dot(merged.astype(BF16), wo_ref[...])
    x1 = _layer_norm(ALPHA * x_ref[...] + g1_ref[...] * y, lg_ref[...], lb_ref[...])
    x1_ref[...] = x1
    h2 = x1 * sc2_ref[...] + sh2_ref[...]
    tm = x1.shape[0]
    gates = _router_gates(_dot3(h2, wr_ref[...]) + br_ref[...])
    gate_ref[...] = gates
    block = jnp.concatenate([h2, gates, jnp.zeros((tm, (HG_ROWS - HG_USED) * LANES), F32)], axis=1)
    hg_ref[...] = pltpu.einshape("t(jl)->(tj)l", block, j=HG_ROWS)


def _post(oa, ob, oc, gates, x, g1, ln_g, ln_b, scale2, shift2, wa, wb, wc, wo, wr, br, *, latent, seq_len):
    t = x.shape[0]
    tm = TM_POST
    row = lambda i: (i, 0)
    mod_map = _mod_row_map(latent, seq_len // tm)
    mod = pl.BlockSpec((None, 1, D_MODEL), mod_map)
    return pl.pallas_call(
        _post_kernel,
        grid=(t // tm,),
        in_specs=[
            pl.BlockSpec((tm, NA_WIDTH), row), pl.BlockSpec((tm, CONV_CH), row), pl.BlockSpec((tm, GQ_Q_WIDTH), row),
            pl.BlockSpec((tm, 3 * D_MODEL), row), pl.BlockSpec((tm, D_MODEL), row),
            mod, _resident((1, D_MODEL)), _resident((1, D_MODEL)), mod, mod,
            _resident(wa.shape), _resident(wb.shape), _resident(wc.shape), _resident(wo.shape),
            _resident(wr.shape), _resident(br.shape),
        ],
        out_specs=[pl.BlockSpec((tm, D_MODEL), row), pl.BlockSpec((tm * HG_ROWS, LANES), row),
                   pl.BlockSpec((tm, LANES), row)],
        out_shape=[jax.ShapeDtypeStruct((t, D_MODEL), F32), jax.ShapeDtypeStruct((t * HG_ROWS, LANES), F32),
                   jax.ShapeDtypeStruct((t, LANES), F32)],
        compiler_params=_cparams(("arbitrary",), 40),
        name="post_lat" if latent else "post_ctx",
    )(oa, ob, oc, gates, x, g1, ln_g, ln_b, scale2, shift2, wa, wb, wc, wo, wr, br)


def _route(grp, tm):
    n = grp.shape[0]
    n_tiles = n // tm + N_GROUPS
    groups = jnp.arange(N_GROUPS, dtype=jnp.int32)
    onehot = (grp[:, None] == groups[None, :]).astype(jnp.int32)
    csum = jnp.cumsum(onehot, axis=0)
    rank = jnp.sum((csum - onehot) * onehot, axis=1)
    counts = csum[-1]
    tiles = (counts + tm - 1) // tm
    tile_end = jnp.cumsum(tiles)
    tile_start = tile_end - tiles
    slot = jnp.sum(onehot * tile_start[None, :], axis=1) * tm + rank
    tile_ids = jnp.arange(n_tiles, dtype=jnp.int32)
    tile_group = jnp.minimum(jnp.sum((tile_ids[:, None] >= tile_end[None, :]).astype(jnp.int32), axis=1), N_GROUPS - 1)
    src = jnp.zeros((n_tiles * tm,), jnp.int32).at[slot].set(jnp.arange(n, dtype=jnp.int32))
    r = jnp.arange(n_tiles * tm, dtype=jnp.int32)
    slot_tile = r // tm
    in_group = (jnp.repeat(tile_group, tm)[:, None] == groups[None, :]).astype(jnp.int32)
    slot_rank = r - jnp.sum(in_group * tile_start[None, :], axis=1) * tm
    valid = (slot_tile < tile_end[-1]) & (slot_rank < jnp.sum(in_group * counts[None, :], axis=1))
    dst = jnp.where(valid, src, n + (slot_tile % 2) * tm + r % tm)
    return tile_group.astype(jnp.int32), src, dst, tile_end[-1:].astype(jnp.int32)


def _moe_kernel(tg_ref, src_ref, dst_ref, nused_ref, h_hbm, wgu_ref, wd_ref, y_hbm, hbuf, ybuf, gsem, ssem):
    t = pl.program_id(0)
    n_used = nused_ref[0]
    slot = t % 2
    tm = ybuf.shape[1] // Y_ROWS

    def gather_row(tile, s, r):
        tok = src_ref[tile * tm + r]
        return pltpu.make_async_copy(h_hbm.at[pl.ds(pl.multiple_of(tok * HG_ROWS, HG_ROWS), HG_USED), :],
                                     hbuf.at[s, pl.ds(pl.multiple_of(r * HG_ROWS, HG_ROWS), HG_USED), :], gsem.at[s])

    def scatter_row(tile, s, r):
        row = dst_ref[tile * tm + r]
        return pltpu.make_async_copy(ybuf.at[s, pl.ds(pl.multiple_of(r * Y_ROWS, Y_ROWS), Y_ROWS), :],
                                     y_hbm.at[pl.ds(pl.multiple_of(row * Y_ROWS, Y_ROWS), Y_ROWS), :], ssem.at[s])

    def for_rows(fn):
        def body(r, carry):
            fn(r)
            return carry
        lax.fori_loop(0, tm, body, 0, unroll=8)

    @pl.when(t == 0)
    def _():
        for_rows(lambda r: gather_row(0, 0, r).start())

    @pl.when(t + 1 < n_used)
    def _():
        for_rows(lambda r: gather_row(t + 1, 1 - slot, r).start())

    @pl.when(t < n_used)
    def _():
        for_rows(lambda r: gather_row(t, slot, r).wait())

        @pl.when(t >= 2)
        def _():
            for_rows(lambda r: scatter_row(t - 2, slot, r).wait())

        first = tg_ref[t] * EXPERTS_PER_GROUP
        token_rows = lambda ref, j, per: ref[slot, pl.ds(j, tm, stride=per), :]
        h = jnp.concatenate([token_rows(hbuf, j, HG_ROWS) for j in range(Y_ROWS)], axis=1).astype(BF16)
        gates = token_rows(hbuf, Y_ROWS, HG_ROWS)
        lane = lax.broadcasted_iota(jnp.int32, gates.shape, 1)
        acc = jnp.zeros((tm, D_MODEL), F32)
        for e in range(EXPERTS_PER_GROUP):
            gu = _dot(h, wgu_ref[e])
            a = gu[:, :D_EXPERT]
            ge = jnp.sum(jnp.where(lane == first + e, gates, 0.0), axis=-1, keepdims=True)
            hid = a * _sigmoid(a) * gu[:, D_EXPERT:] * ge
            acc = acc + _dot(hid.astype(BF16), wd_ref[e])
        for j in range(Y_ROWS):
            ybuf[slot, pl.ds(j, tm, stride=Y_ROWS), :] = acc[:, j * LANES:(j + 1) * LANES]
        for_rows(lambda r: scatter_row(t, slot, r).start())

        @pl.when(t == n_used - 1)
        def _():
            for_rows(lambda r: scatter_row(t, slot, r).wait())

            @pl.when(t >= 1)
            def _():
                for_rows(lambda r: scatter_row(t - 1, 1 - slot, r).wait())

            n_rows = y_hbm.shape[0] - 2 * tm * Y_ROWS
            ybuf[0] = jnp.zeros(ybuf.shape[1:], F32)
            for half in range(2):
                fill = pltpu.make_async_copy(
                    ybuf.at[0], y_hbm.at[pl.ds(n_rows + half * tm * Y_ROWS, tm * Y_ROWS), :], ssem.at[half])
                fill.start()
                fill.wait()


def _moe_experts(hg, route, wgu, wd):
    n = hg.shape[0] // HG_ROWS
    tm = TM_MOE
    tile_group, src, dst, n_used = route
    n_tiles = tile_group.shape[0]
    grid_spec = pltpu.PrefetchScalarGridSpec(
        num_scalar_prefetch=4,
        grid=(n_tiles,),
        in_specs=[
            pl.BlockSpec(memory_space=pl.ANY),
            pl.BlockSpec((EXPERTS_PER_GROUP, D_MODEL, 2 * D_EXPERT), lambda t, tg, s, d, nu: (tg[t], 0, 0)),
            pl.BlockSpec((EXPERTS_PER_GROUP, D_EXPERT, D_MODEL), lambda t, tg, s, d, nu: (tg[t], 0, 0)),
        ],
        out_specs=pl.BlockSpec(memory_space=pl.ANY),
        scratch_shapes=[
            pltpu.VMEM((2, tm * HG_ROWS, LANES), F32),
            pltpu.VMEM((2, tm * Y_ROWS, LANES), F32),
            pltpu.SemaphoreType.DMA((2,)),
            pltpu.SemaphoreType.DMA((2,)),
        ],
    )
    return pl.pallas_call(
        _moe_kernel,
        grid_spec=grid_spec,
        out_shape=jax.ShapeDtypeStruct(((n + 2 * tm) * Y_ROWS, LANES), F32),
        compiler_params=_cparams(("arbitrary",), 58),
        name="moe_experts",
    )(tile_group, src, dst, n_used, hg, wgu, wd)


def _moe_out_kernel(x1_ref, y_ref, g2_ref, lg_ref, lb_ref, o_ref):
    o_ref[...] = _moe_residual_norm(x1_ref, y_ref, g2_ref, lg_ref, lb_ref)


def _moe_out(x1, y, g2, ln_g, ln_b, *, latent, seq_len):
    t = x1.shape[0]
    tm = TM_POST
    row = lambda i: (i, 0)
    return pl.pallas_call(
        _moe_out_kernel,
        grid=(t // tm,),
        in_specs=[pl.BlockSpec((tm, D_MODEL), row), pl.BlockSpec((tm * Y_ROWS, LANES), row),
                  pl.BlockSpec((None, 1, D_MODEL), _mod_row_map(latent, seq_len // tm)),
                  _resident((1, D_MODEL)), _resident((1, D_MODEL))],
        out_specs=pl.BlockSpec((tm, D_MODEL), row),
        out_shape=jax.ShapeDtypeStruct((t, D_MODEL), F32),
        compiler_params=_cparams(("arbitrary",), 32),
        name="moe_out",
    )(x1, y, g2, ln_g, ln_b)


def _moe(hg, gates, wgu, wd):
    return _moe_experts(hg, _route(gates[:, GROUP_LANE].astype(jnp.int32), TM_MOE), wgu, wd)


def _head_mean_matrix():
    i = jnp.arange(GQ_Q_WIDTH) // HEAD_DIM
    return jnp.where(i[:, None] == i[None, :], 1.0 / HEAD_DIM, 0.0).astype(BF16)


def _rope_tables(n_tokens):
    t = jnp.arange(n_tokens, dtype=jnp.int32)
    axis_dim = HEAD_DIM // 2
    inv_freq = ROPE_THETA ** (-jnp.arange(0, axis_dim, 2, dtype=F32) / axis_dim)
    ar = (t // GRID_W).astype(F32)[:, None] * inv_freq
    ac = (t % GRID_W).astype(F32)[:, None] * inv_freq
    c = jnp.concatenate([jnp.cos(ar), jnp.cos(ar), jnp.cos(ac), jnp.cos(ac)], axis=1)
    s = jnp.concatenate([-jnp.sin(ar), jnp.sin(ar), -jnp.sin(ac), jnp.sin(ac)], axis=1)
    return jnp.tile(c, (1, LANES // HEAD_DIM)), jnp.tile(s, (1, LANES // HEAD_DIM))


def _na_bias_table(rpb, rows):
    col = np.arange(GRID_W)
    col_idx = np.clip(col[:, None] - col[None, :], -(NA_KW - 1), NA_KW - 1) + NA_KW - 1
    onehot = (col_idx[None] == np.arange(2 * NA_KW - 1)[:, None, None]).astype(np.float32)
    col_start = np.clip(col - NA_KW // 2, 0, GRID_W - NA_KW)
    col_ok = (col[:, None] >= col_start[None, :]) & (col[:, None] < col_start[None, :] + NA_KW)
    t = jnp.einsum("hic,ckq->hikq", rpb, onehot, precision=lax.Precision.HIGHEST)
    t = jnp.where(col_ok[None, None], t, NEG_INF)
    masked = jnp.full((NA_HEADS, GRID_W, GRID_W), NEG_INF, F32)
    pats = []
    for first_qrow in (0, NA_KH // 2, rows - NA_QROWS):
        krow0 = int(np.clip(first_qrow - NA_KH // 2, 0, rows - NA_WIN_ROWS))
        krows = []
        for kj in range(NA_WIN_ROWS):
            blocks = []
            for qi in range(NA_QROWS):
                qrow, krow = first_qrow + qi, krow0 + kj
                win0 = int(np.clip(qrow - NA_KH // 2, 0, rows - NA_KH))
                in_window = win0 <= krow < win0 + NA_KH
                blocks.append(t[:, krow - qrow + NA_KH - 1] if in_window else masked)
            krows.append(jnp.concatenate(blocks, axis=-1))
        pats.append(jnp.concatenate(krows, axis=-2))
    return jnp.stack(pats)


def _stream_layer(x, mods, l, wts, *, latent, n_seq, seq_len, cache=None, rope=None):
    sh1, sc1, g1, sh2, sc2, g2 = mods
    pieces = _proj(x, 1.0 + sc1, sh1, wts["w_in"][l], wts["gmat"], wts["qn"][l], wts["kn"][l], rope,
                   latent=latent, seq_len=seq_len)
    if isinstance(x, tuple):
        x, pieces = pieces[0], pieces[1:]
    qa, ka, va, u, qc, kc, vc, gates = pieces
    if latent:
        nat_k, nat_v, gqa_k, gqa_v = cache
        swap = lambda a: jnp.swapaxes(a, 1, 2)
        out_a = _na_latent(qa, ka, swap(va.reshape(n_seq, seq_len, NA_WIDTH)), nat_k, swap(nat_v),
                           wts["na_bias"][l], n_seq=n_seq, seq_len=seq_len)
        k_all = jnp.concatenate([gqa_k, kc.reshape(n_seq, seq_len, GQ_KV_WIDTH)], axis=1)
        v_all = jnp.concatenate([gqa_v, vc.reshape(n_seq, seq_len, GQ_KV_WIDTH)], axis=1)
        out_c = _gqa_latent(qc, k_all, v_all, n_seq=n_seq, seq_len=seq_len)
    else:
        out_a, out_c = _ctx_attention(qa, ka, va, qc, kc, vc, seq_len=seq_len)
    out_b = _conformer_conv(u, wts["conv_w"][l], wts["conv_b"][l], wts["conv_ln_g"][l], wts["conv_ln_b"][l],
                            n_seq=n_seq, seq_len=seq_len)
    x1, hg, moe_gates = _post(out_a, out_b, out_c, gates, x, g1, wts["ln1_g"][l], wts["ln1_b"][l], 1.0 + sc2, sh2,
                         wts["w_br_a"][l], wts["w_br_b"][l], wts["w_br_c"][l], wts["w_out"][l],
                         wts["w_router"], wts["b_router"], latent=latent, seq_len=seq_len)
    y = _moe(hg, moe_gates, wts["w_gate_up"][l], wts["w_down"][l])
    return (x1, y, g2, wts["ln2_g"][l], wts["ln2_b"][l]), (ka, va, kc, vc)


def kernel(x_prompt, x_sample, cache_nat_k, cache_nat_v, cache_gqa_k, cache_gqa_v, c, c_ctx, w_mod, b_mod, w_in, nat_rpb, conv_w, conv_b, conv_ln_g, conv_ln_b, q_norm_g, k_norm_g, w_br_a, w_br_b, w_br_c, w_out, ln1_g, ln1_b, ln2_g, ln2_b, w_router, b_router, w_gate_up, w_down):
    batch, seq, _ = x_prompt.shape
    dec_batch, dec_seq, _ = x_sample.shape
    past = cache_nat_k.shape[2]
    assert dec_batch == 2 and (batch * seq) % TM_MOE == 0 and dec_seq % TM_MOE == 0 and seq % TL_CONV == 0
    assert TM_MOE % TM_PROJ == 0 and TM_MOE % TM_POST == 0

    cond8 = jnp.zeros((8, D_MODEL), F32).at[0].set(c_ctx).at[1:1 + dec_batch].set(c)
    mods_all = _modulation(cond8, w_mod, b_mod)

    pad = LANES - N_EXPERTS
    vecd = lambda a: a.reshape(DEPTH, 1, -1)
    wts = dict(
        w_in=w_in.astype(BF16),
        gmat=_head_mean_matrix(),
        qn=jnp.tile(q_norm_g, (1, GQ_HEADS)).reshape(DEPTH, 1, GQ_Q_WIDTH),
        kn=jnp.tile(k_norm_g, (1, GQ_KV_HEADS)).reshape(DEPTH, 1, GQ_KV_WIDTH),
        na_bias=jnp.stack([_na_bias_table(nat_rpb[l], dec_seq // GRID_W) for l in range(DEPTH)]),
        conv_w=conv_w, conv_b=conv_b, conv_ln_g=conv_ln_g, conv_ln_b=conv_ln_b,
        ln1_g=vecd(ln1_g), ln1_b=vecd(ln1_b), ln2_g=vecd(ln2_g), ln2_b=vecd(ln2_b),
        w_br_a=w_br_a.astype(BF16), w_br_b=w_br_b.astype(BF16), w_br_c=w_br_c.astype(BF16),
        w_out=w_out.astype(BF16),
        w_router=jnp.pad(w_router, ((0, 0), (0, pad))),
        b_router=jnp.pad(b_router, (0, pad), constant_values=NEG_INF).reshape(1, LANES),
        w_gate_up=w_gate_up.astype(BF16), w_down=w_down.astype(BF16),
    )
    rope = _rope_tables(dec_seq)

    def layer_mods(l):
        m = mods_all[l, :1 + dec_batch].reshape(1 + dec_batch, 6, 1, D_MODEL)
        return [m[:, i] for i in range(6)]

    xp = x_prompt.reshape(batch * seq, D_MODEL)
    xs = x_sample.reshape(dec_batch * dec_seq, D_MODEL)
    new_cache = []
    for l in range(DEPTH):
        mods = layer_mods(l)
        xp, ctx_kv = _stream_layer(xp, mods, l, wts, latent=False, n_seq=batch, seq_len=seq)
        new_cache.append(ctx_kv)
        cache = (cache_nat_k[:, l].reshape(dec_batch, past, NA_WIDTH).astype(BF16),
                 cache_nat_v[:, l].reshape(dec_batch, past, NA_WIDTH).astype(BF16),
                 cache_gqa_k[:, l].reshape(dec_batch, past, GQ_KV_WIDTH).astype(BF16),
                 cache_gqa_v[:, l].reshape(dec_batch, past, GQ_KV_WIDTH).astype(BF16))
        xs, _ = _stream_layer(xs, mods, l, wts, latent=True, n_seq=dec_batch, seq_len=dec_seq, cache=cache, rope=rope)
    xp = _moe_out(*xp, latent=False, seq_len=seq)
    xs = _moe_out(*xs, latent=True, seq_len=dec_seq)

    def stack(i, heads):
        return jnp.stack([kv[i].reshape(batch, seq, heads, HEAD_DIM) for kv in new_cache], axis=1)

    return (xp.reshape(batch, seq, D_MODEL), xs.reshape(dec_batch, dec_seq, D_MODEL),
            stack(0, NA_HEADS), stack(1, NA_HEADS), stack(2, GQ_KV_HEADS), stack(3, GQ_KV_HEADS))
```

```python
import functools

import numpy as np
import jax
import jax.numpy as jnp
from jax import lax
from jax.experimental import pallas as pl
from jax.experimental.pallas import tpu as pltpu

F32 = jnp.float32
BF16 = jnp.bfloat16

D_MODEL = 1024
DEPTH = 2
GRID_W = 64
HEAD_DIM = 64
NA_HEADS = 8
NA_WIDTH = NA_HEADS * HEAD_DIM
NA_KH = 8
NA_KW = 16
NA_QROWS = 4
NA_WIN_ROWS = NA_QROWS + NA_KH
CONV_CH = 512
CONV_K = 31
GQ_HEADS = 8
GQ_KV_HEADS = 2
GQ_Q_WIDTH = GQ_HEADS * HEAD_DIM
GQ_KV_WIDTH = GQ_KV_HEADS * HEAD_DIM
ROPE_THETA = 10000.0
N_EXPERTS = 16
EXPERTS_PER_GROUP = 4
N_GROUPS = N_EXPERTS // EXPERTS_PER_GROUP
GROUP_LANE = N_EXPERTS
Y_ROWS = D_MODEL // 128
HG_ROWS = 2 * Y_ROWS
HG_USED = Y_ROWS + 1
D_EXPERT = 512
ALPHA = (2 * DEPTH) ** 0.25
LN_EPS = 1e-6
RMS_EPS = 1e-6
NEG_INF = -1e30
QK_SCALE = HEAD_DIM ** -0.5

_OFF = (0, 512, 1024, 1536, 2560, 3072, 3200, 3328, 6400)
IN_COLS = _OFF[-1]

LANES = 128
V7X_VMEM_BYTES = 64 * 2 ** 20
HALO = 16

TM_PROJ = 256
TM_POST = 256
TM_MOE = 512
TL_CONV = 256
TQ_GQA = 256


def _cparams(sem, vmem_mib):
    assert vmem_mib * 2 ** 20 < V7X_VMEM_BYTES
    return pltpu.CompilerParams(dimension_semantics=sem, vmem_limit_bytes=vmem_mib * 2 ** 20)


def _resident(shape):
    nd = len(shape)
    return pl.BlockSpec(shape, lambda *_: (0,) * nd, pipeline_mode=pl.Buffered(1))


def _dot(a, b):
    return jnp.dot(a, b, preferred_element_type=F32)


def _dot_nt(a, b):
    return lax.dot_general(a, b, (((1,), (1,)), ((), ())), preferred_element_type=F32)


def _split_bf16(a):
    hi = a.astype(BF16)
    lo = (a - hi.astype(F32)).astype(BF16)
    return hi, lo


def _dot3(a, b):
    a_hi, a_lo = _split_bf16(a)
    b_hi, b_lo = _split_bf16(b)
    return _dot(a_hi, b_hi) + _dot(a_lo, b_hi) + _dot(a_hi, b_lo)


def _sigmoid(x):
    return 1.0 / (1.0 + jnp.exp(-x))


def _layer_norm(x, g, b):
    mu = jnp.mean(x, axis=-1, keepdims=True)
    xc = x - mu
    var = jnp.mean(xc * xc, axis=-1, keepdims=True)
    return xc * lax.rsqrt(var + LN_EPS) * g + b


def _mod_kernel(cond_ref, w_ref, b_ref, o_ref):
    c = cond_ref[...]
    o_ref[...] = _dot3(c * _sigmoid(c), w_ref[...]) + b_ref[...]


def _modulation(cond8, w_mod, b_mod):
    tn = 1536
    n = w_mod.shape[-1]
    return pl.pallas_call(
        _mod_kernel,
        grid=(DEPTH, n // tn),
        in_specs=[
            pl.BlockSpec((8, D_MODEL), lambda l, j: (0, 0)),
            pl.BlockSpec((None, D_MODEL, tn), lambda l, j: (l, 0, j)),
            pl.BlockSpec((None, 1, tn), lambda l, j: (l, 0, j)),
        ],
        out_specs=pl.BlockSpec((None, 8, tn), lambda l, j: (l, 0, j)),
        out_shape=jax.ShapeDtypeStruct((DEPTH, 8, n), F32),
        compiler_params=_cparams(("arbitrary", "arbitrary"), 40),
        name="modulation",
    )(cond8, w_mod, b_mod.reshape(DEPTH, 1, n))


def _head_rms(x, gmat, gain):
    hi, lo = _split_bf16(x * x)
    ms = _dot(hi, gmat) + _dot(lo, gmat)
    return x * lax.rsqrt(ms + RMS_EPS) * gain


def _rope(x, c, s):
    w = x.shape[1]
    reps = w // LANES
    if reps > 1:
        c = jnp.concatenate([c] * reps, axis=1)
        s = jnp.concatenate([s] * reps, axis=1)
    lane = lax.broadcasted_iota(jnp.int32, x.shape, 1)
    partner = jnp.where((lane % 32) < 16, pltpu.roll(x, w - 16, 1), pltpu.roll(x, 16, 1))
    return x * c + partner * s


def _moe_residual_norm(x1_ref, y_ref, g2_ref, lg_ref, lb_ref):
    tm = x1_ref.shape[0]
    y = jnp.concatenate([y_ref[pl.ds(j, tm, stride=Y_ROWS), :] for j in range(Y_ROWS)], axis=1)
    return _layer_norm(ALPHA * x1_ref[...] + g2_ref[...] * y, lg_ref[...], lb_ref[...])


def _proj_kernel(latent, fused, *refs):
    refs = list(refs)
    n_in = (5 if fused else 1) + 6 + (2 if latent else 0)
    ins, outs = refs[:n_in], refs[n_in:]
    x = _moe_residual_norm(*ins[:5]) if fused else ins[0][...]
    sc_ref, sh_ref, w_ref, gm_ref, qn_ref, kn_ref = ins[n_in - 6 - (2 if latent else 0):][:6]
    if latent:
        c_ref, s_ref = ins[-2:]
    if fused:
        outs.pop(0)[...] = x
    qa_ref, ka_ref, va_ref, u_ref, qc_ref, kc_ref, vc_ref, gt_ref = outs
    h = (x * sc_ref[...] + sh_ref[...]).astype(BF16)

    def mm(i):
        return _dot(h, w_ref[:, _OFF[i]:_OFF[i + 1]])

    qa_ref[...] = (mm(0) * QK_SCALE).astype(qa_ref.dtype)
    ka_ref[...] = mm(1).astype(ka_ref.dtype)
    va_ref[...] = mm(2).astype(va_ref.dtype)
    u_ref[...] = mm(3).astype(u_ref.dtype)
    qc = _head_rms(mm(4), gm_ref[...], qn_ref[...])
    kc = _head_rms(mm(5), gm_ref[:GQ_KV_WIDTH, :GQ_KV_WIDTH], kn_ref[...])
    if latent:
        qc = _rope(qc, c_ref[...], s_ref[...])
        kc = _rope(kc, c_ref[...], s_ref[...])
    qc_ref[...] = (qc * QK_SCALE).astype(qc_ref.dtype)
    kc_ref[...] = kc.astype(kc_ref.dtype)
    vc_ref[...] = mm(6).astype(vc_ref.dtype)
    gt_ref[...] = mm(7).astype(gt_ref.dtype)


def _mod_row_map(latent, tiles_per_seq):
    if latent:
        return lambda i: (1 + i // tiles_per_seq, 0, 0)
    return lambda i: (0, 0, 0)


def _proj(x, scale, shift, w_bf, gmat, qn, kn, rope, *, latent, seq_len):
    fused = isinstance(x, tuple)
    t = (x[0] if fused else x).shape[0]
    tm = TM_PROJ
    kv_dt = BF16 if latent else F32
    row = lambda i: (i, 0)
    mod_map = _mod_row_map(latent, seq_len // tm)
    if fused:
        in_specs = [pl.BlockSpec((tm, D_MODEL), row), pl.BlockSpec((tm * Y_ROWS, LANES), row),
                    pl.BlockSpec((None, 1, D_MODEL), mod_map), _resident((1, D_MODEL)), _resident((1, D_MODEL))]
        args = list(x)
    else:
        in_specs = [pl.BlockSpec((tm, D_MODEL), row)]
        args = [x]
    in_specs += [
        pl.BlockSpec((None, 1, D_MODEL), mod_map),
        pl.BlockSpec((None, 1, D_MODEL), mod_map),
        _resident((D_MODEL, IN_COLS)),
        _resident((GQ_Q_WIDTH, GQ_Q_WIDTH)),
        _resident((1, GQ_Q_WIDTH)),
        _resident((1, GQ_KV_WIDTH)),
    ]
    args += [scale, shift, w_bf, gmat, qn, kn]
    if latent:
        per = seq_len // tm
        in_specs += [pl.BlockSpec((tm, LANES), lambda i: (i % per, 0))] * 2
        args += list(rope)
    widths = (NA_WIDTH, NA_WIDTH, NA_WIDTH, 2 * CONV_CH, GQ_Q_WIDTH, GQ_KV_WIDTH, GQ_KV_WIDTH, 3 * D_MODEL)
    dtypes = (BF16, kv_dt, kv_dt, BF16, BF16, kv_dt, kv_dt, BF16)
    if fused:
        widths = (D_MODEL,) + widths
        dtypes = (F32,) + dtypes
    return pl.pallas_call(
        functools.partial(_proj_kernel, latent, fused),
        grid=(t // tm,),
        in_specs=in_specs,
        out_specs=[pl.BlockSpec((tm, w), row) for w in widths],
        out_shape=[jax.ShapeDtypeStruct((t, w), dt) for w, dt in zip(widths, dtypes)],
        compiler_params=_cparams(("arbitrary",), 48),
        name="proj_lat" if latent else "proj_ctx",
    )(*args)


def _pair(ref_or_val, j):
    return ref_or_val[:, j * LANES:(j + 1) * LANES]


def _keep_half(x, half, move_to=None):
    lane = lax.broadcasted_iota(jnp.int32, x.shape, 1)
    keep = (lane >= HEAD_DIM) if half else (lane < HEAD_DIM)
    if move_to is None or move_to == half:
        return jnp.where(keep, x, jnp.zeros_like(x))
    xf = jnp.where(keep, x.astype(F32), 0.0)
    return pltpu.roll(xf, HEAD_DIM, 1).astype(x.dtype)


def _attend_t(qp, pieces):
    scores = []
    for k, _, bias in pieces:
        s = _dot_nt(k, qp)
        scores.append(s if bias is None else s + bias)
    m = None
    for s in scores:
        mi = jnp.max(s, axis=0, keepdims=True)
        m = mi if m is None else jnp.maximum(m, mi)
    l = 0.0
    o = 0.0
    for s, (_, vt, _) in zip(scores, pieces):
        p = jnp.exp(s - m)
        l = l + jnp.sum(p, axis=0, keepdims=True)
        o = o + _dot(vt, p.astype(BF16))
    return o / l


def _select_rows(o_even, o_odd):
    return jnp.concatenate([o_even[:HEAD_DIM], o_odd[HEAD_DIM:]], axis=0)


def _ctx_attn_kernel(qa_ref, ka_ref, va_ref, qc_ref, kc_ref, vc_ref, oa_ref, oc_ref):
    ka = ka_ref[...].astype(BF16)
    vat = va_ref[...].T.astype(BF16)
    outs = []
    for j in range(NA_HEADS // 2):
        k = _pair(ka, j)
        vt = vat[j * LANES:(j + 1) * LANES]
        q = _pair(qa_ref, j)
        outs.append(_select_rows(_attend_t(_keep_half(q, 0), [(k, vt, None)]),
                                 _attend_t(_keep_half(q, 1), [(k, vt, None)])))
    oa_ref[...] = jnp.concatenate(outs, axis=0).T.astype(oa_ref.dtype)
    kc = kc_ref[...].astype(BF16)
    vct = vc_ref[...].T.astype(BF16)
    outs = []
    for h in range(GQ_HEADS):
        g = h // (GQ_HEADS // GQ_KV_HEADS)
        o = _attend_t(_keep_half(_pair(qc_ref, h // 2), h % 2, move_to=g), [(kc, vct, None)])
        outs.append(o[g * HEAD_DIM:(g + 1) * HEAD_DIM])
    oc_ref[...] = jnp.concatenate(outs, axis=0).T.astype(oc_ref.dtype)


def _ctx_attention(qa, ka, va, qc, kc, vc, *, seq_len):
    t = qa.shape[0]
    row = lambda i: (i, 0)
    spec = lambda w: pl.BlockSpec((seq_len, w), row)
    return pl.pallas_call(
        _ctx_attn_kernel,
        grid=(t // seq_len,),
        in_specs=[spec(NA_WIDTH), spec(NA_WIDTH), spec(NA_WIDTH), spec(GQ_Q_WIDTH), spec(GQ_KV_WIDTH), spec(GQ_KV_WIDTH)],
        out_specs=[spec(NA_WIDTH), spec(GQ_Q_WIDTH)],
        out_shape=[jax.ShapeDtypeStruct((t, NA_WIDTH), BF16), jax.ShapeDtypeStruct((t, GQ_Q_WIDTH), BF16)],
        compiler_params=_cparams(("arbitrary",), 32),
        name="attn_ctx",
    )(qa, ka, va, qc, kc, vc)


def _na_key_row_start(blk, rows):
    return jnp.clip(blk * NA_QROWS - NA_KH // 2, 0, rows - NA_WIN_ROWS)


def _na_lat_kernel(rows, q_ref, k_ref, vt_ref, kc_ref, vct_ref, bias_ref, o_ref):
    base = pl.multiple_of(_na_key_row_start(pl.program_id(1), rows) * GRID_W, 2 * LANES)
    nk = NA_WIN_ROWS * GRID_W
    kwin = k_ref[pl.ds(base, nk), :]
    outs = []
    for j in range(NA_HEADS // 2):
        k = _pair(kwin, j)
        vt = vt_ref[j * LANES:(j + 1) * LANES, pl.ds(base, nk)]
        kc = _pair(kc_ref, j)
        vct = vct_ref[j * LANES:(j + 1) * LANES, :]
        q = _pair(q_ref, j)
        halves = [_attend_t(_keep_half(q, half), [(k, vt, bias_ref[2 * j + half]), (kc, vct, None)])
                  for half in range(2)]
        outs.append(_select_rows(*halves))
    o_ref[...] = jnp.concatenate(outs, axis=0).T.astype(o_ref.dtype)


def _na_latent(q, k, vt, k_ctx, vt_ctx, bias, *, n_seq, seq_len):
    rows = seq_len // GRID_W
    nblk = rows // NA_QROWS
    tq = NA_QROWS * GRID_W
    pattern = lambda b, i: (jnp.where(i == 0, 0, jnp.where(i == nblk - 1, 2, 1)), 0, 0, 0)
    return pl.pallas_call(
        functools.partial(_na_lat_kernel, rows),
        grid=(n_seq, nblk),
        in_specs=[
            pl.BlockSpec((tq, NA_WIDTH), lambda b, i: (b * nblk + i, 0)),
            pl.BlockSpec((seq_len, NA_WIDTH), lambda b, i: (b, 0)),
            pl.BlockSpec((None, NA_WIDTH, seq_len), lambda b, i: (b, 0, 0)),
            pl.BlockSpec((None,) + k_ctx.shape[1:], lambda b, i: (b, 0, 0)),
            pl.BlockSpec((None,) + vt_ctx.shape[1:], lambda b, i: (b, 0, 0)),
            pl.BlockSpec((None,) + bias.shape[1:], pattern),
        ],
        out_specs=pl.BlockSpec((tq, NA_WIDTH), lambda b, i: (b * nblk + i, 0)),
        out_shape=jax.ShapeDtypeStruct(q.shape, BF16),
        compiler_params=_cparams(("arbitrary", "arbitrary"), 52),
        name="attn_na_lat",
    )(q, k, vt, k_ctx, vt_ctx, bias)


def _gqa_lat_kernel(q_ref, k_ref, v_ref, o_ref):
    k = k_ref[...]
    for h in range(GQ_HEADS):
        g = h // (GQ_HEADS // GQ_KV_HEADS)
        s = _dot_nt(_keep_half(_pair(q_ref, h // 2), h % 2, move_to=g), k)
        p = jnp.exp(s - jnp.max(s, axis=-1, keepdims=True)).astype(BF16)
        o = _dot(p, v_ref[g])
        o_ref[:, h * HEAD_DIM:(h + 1) * HEAD_DIM] = (o[:, :HEAD_DIM] / o[:, HEAD_DIM:HEAD_DIM + 1]).astype(o_ref.dtype)


def _gqa_values_with_ones(v_all):
    b, s, _ = v_all.shape
    tail = jnp.concatenate([jnp.ones((b, s, 1), v_all.dtype), jnp.zeros((b, s, HEAD_DIM - 1), v_all.dtype)], axis=2)
    return jnp.stack([jnp.concatenate([v_all[:, :, g * HEAD_DIM:(g + 1) * HEAD_DIM], tail], axis=2)
                      for g in range(GQ_KV_HEADS)], axis=1)


def _gqa_latent(q, k_all, v_all, *, n_seq, seq_len):
    per = seq_len // TQ_GQA
    s_all = k_all.shape[1]
    return pl.pallas_call(
        _gqa_lat_kernel,
        grid=(n_seq, per),
        in_specs=[
            pl.BlockSpec((TQ_GQA, GQ_Q_WIDTH), lambda b, t: (b * per + t, 0)),
            pl.BlockSpec((None, s_all, GQ_KV_WIDTH), lambda b, t: (b, 0, 0)),
            pl.BlockSpec((None, GQ_KV_HEADS, s_all, 2 * HEAD_DIM), lambda b, t: (b, 0, 0, 0)),
        ],
        out_specs=pl.BlockSpec((TQ_GQA, GQ_Q_WIDTH), lambda b, t: (b * per + t, 0)),
        out_shape=jax.ShapeDtypeStruct(q.shape, BF16),
        compiler_params=_cparams(("arbitrary", "arbitrary"), 48),
        name="attn_gqa_lat",
    )(q, k_all, _gqa_values_with_ones(v_all))


def _conv_kernel(u_ref, up_ref, un_ref, w_ref, b_ref, g_ref, beta_ref, o_ref, shifted_ref):
    t = pl.program_id(1)
    nt = pl.num_programs(1)
    tl = u_ref.shape[0]

    def glu(v):
        return v[:, :CONV_CH].astype(F32) * _sigmoid(v[:, CONV_CH:].astype(F32))

    slab = jnp.concatenate([jnp.where(t > 0, glu(up_ref[...]), 0.0), glu(u_ref[...]),
                            jnp.where(t < nt - 1, glu(un_ref[...]), 0.0)], axis=0)
    first = HALO - CONV_K // 2
    span = shifted_ref.shape[1]
    assert (first + CONV_K - 1) // 8 * 8 + tl == span and span + 7 <= slab.shape[0]
    acc = jnp.zeros((tl, CONV_CH), F32)
    for shift in range(8):
        shifted_ref[shift % 2] = slab[shift:shift + span]
        for k in range(CONV_K):
            if (first + k) % 8 == shift:
                base = first + k - shift
                acc = acc + shifted_ref[shift % 2, base:base + tl, :] * w_ref[k:k + 1, :]
    y = _layer_norm(acc + b_ref[...], g_ref[...], beta_ref[...])
    o_ref[...] = (y * _sigmoid(y)).astype(o_ref.dtype)


def _conformer_conv(u, conv_w, conv_b, ln_g, ln_b, *, n_seq, seq_len):
    tl = TL_CONV
    nt = seq_len // tl
    hb = tl // HALO
    last_hb = seq_len // HALO - 1
    u3 = u.reshape(n_seq, seq_len, 2 * CONV_CH)
    vec = lambda a: a.reshape(1, CONV_CH)
    out = pl.pallas_call(
        _conv_kernel,
        grid=(n_seq, nt),
        in_specs=[
            pl.BlockSpec((None, tl, 2 * CONV_CH), lambda s, t: (s, t, 0)),
            pl.BlockSpec((None, HALO, 2 * CONV_CH), lambda s, t: (s, jnp.maximum(t * hb - 1, 0), 0)),
            pl.BlockSpec((None, HALO, 2 * CONV_CH), lambda s, t: (s, jnp.minimum((t + 1) * hb, last_hb), 0)),
            _resident((CONV_K, CONV_CH)),
            _resident((1, CONV_CH)),
            _resident((1, CONV_CH)),
            _resident((1, CONV_CH)),
        ],
        out_specs=pl.BlockSpec((None, tl, CONV_CH), lambda s, t: (s, t, 0)),
        out_shape=jax.ShapeDtypeStruct((n_seq, seq_len, CONV_CH), BF16),
        scratch_shapes=[pltpu.VMEM((2, (HALO + CONV_K // 2) // 8 * 8 + tl, CONV_CH), F32)],
        compiler_params=_cparams(("arbitrary", "arbitrary"), 32),
        name="conformer_conv",
    )(u3, u3, u3, conv_w, vec(conv_b), vec(ln_g), vec(ln_b))
    return out.reshape(n_seq * seq_len, CONV_CH)


def _router_gates(logits):
    lane = lax.broadcasted_iota(jnp.int32, logits.shape, 1)
    m = jnp.max(logits, axis=-1, keepdims=True)
    e = jnp.exp(logits - m)
    probs = e / jnp.sum(e, axis=-1, keepdims=True)
    p1 = jnp.max(probs, axis=-1, keepdims=True)
    i1 = jnp.min(jnp.where(probs == p1, lane, LANES), axis=-1, keepdims=True)
    in_group = (lane // EXPERTS_PER_GROUP) == (i1 // EXPERTS_PER_GROUP)
    cand = jnp.where(in_group, jnp.where(lane == i1, -1.0, probs), -1.0)
    p2 = jnp.max(cand, axis=-1, keepdims=True)
    i2 = jnp.min(jnp.where(cand == p2, lane, LANES), axis=-1, keepdims=True)
    den = p1 + p2
    gates = jnp.where(lane == i1, p1 / den, 0.0) + jnp.where(lane == i2, p2 / den, 0.0)
    group = (i1 // EXPERTS_PER_GROUP).astype(F32)
    return jnp.where(lane == GROUP_LANE, group, gates)


def _post_kernel(oa_ref, ob_ref, oc_ref, gt_ref, x_ref, g1_ref, lg_ref, lb_ref, sc2_ref, sh2_ref,
                 wa_ref, wb_ref, wc_ref, wo_ref, wr_ref, br_ref, x1_ref, hg_ref, gate_ref):
    sg = _sigmoid(gt_ref[...].astype(F32))
    merged = (sg[:, :D_MODEL] * _dot(oa_ref[...], wa_ref[...])
              + sg[:, D_MODEL:2 * D_MODEL] * _dot(ob_ref[...], wb_ref[...])
              + sg[:, 2 * D_MODEL:] * _dot(oc_ref[...], wc_ref[...]))
    y = _dot(merged.astype(BF16), wo_ref[...])
    x1 = _layer_norm(ALPHA * x_ref[...] + g1_ref[...] * y, lg_ref[...], lb_ref[...])
    x1_ref[...] = x1
    h2 = x1 * sc2_ref[...] + sh2_ref[...]
    tm = x1.shape[0]
    gates = _router_gates(_dot3(h2, wr_ref[...]) + br_ref[...])
    gate_ref[...] = gates
    block = jnp.concatenate([h2, gates, jnp.zeros((tm, (HG_ROWS - HG_USED) * LANES), F32)], axis=1)
    hg_ref[...] = pltpu.einshape("t(jl)->(tj)l", block, j=HG_ROWS)


def _post(oa, ob, oc, gates, x, g1, ln_g, ln_b, scale2, shift2, wa, wb, wc, wo, wr, br, *, latent, seq_len):
    t = x.shape[0]
    tm = TM_POST
    row = lambda i: (i, 0)
    mod_map = _mod_row_map(latent, seq_len // tm)
    mod = pl.BlockSpec((None, 1, D_MODEL), mod_map)
    return pl.pallas_call(
        _post_kernel,
        grid=(t // tm,),
        in_specs=[
            pl.BlockSpec((tm, NA_WIDTH), row), pl.BlockSpec((tm, CONV_CH), row), pl.BlockSpec((tm, GQ_Q_WIDTH), row),
            pl.BlockSpec((tm, 3 * D_MODEL), row), pl.BlockSpec((tm, D_MODEL), row),
            mod, _resident((1, D_MODEL)), _resident((1, D_MODEL)), mod, mod,
            _resident(wa.shape), _resident(wb.shape), _resident(wc.shape), _resident(wo.shape),
            _resident(wr.shape), _resident(br.shape),
        ],
        out_specs=[pl.BlockSpec((tm, D_MODEL), row), pl.BlockSpec((tm * HG_ROWS, LANES), row),
                   pl.BlockSpec((tm, LANES), row)],
        out_shape=[jax.ShapeDtypeStruct((t, D_MODEL), F32), jax.ShapeDtypeStruct((t * HG_ROWS, LANES), F32),
                   jax.ShapeDtypeStruct((t, LANES), F32)],
        compiler_params=_cparams(("arbitrary",), 40),
        name="post_lat" if latent else "post_ctx",
    )(oa, ob, oc, gates, x, g1, ln_g, ln_b, scale2, shift2, wa, wb, wc, wo, wr, br)


def _route(grp, tm):
    n = grp.shape[0]
    n_tiles = n // tm + N_GROUPS
    groups = jnp.arange(N_GROUPS, dtype=jnp.int32)
    onehot = (grp[:, None] == groups[None, :]).astype(jnp.int32)
    csum = jnp.cumsum(onehot, axis=0)
    rank = jnp.sum((csum - onehot) * onehot, axis=1)
    counts = csum[-1]
    tiles = (counts + tm - 1) // tm
    tile_end = jnp.cumsum(tiles)
    tile_start = tile_end - tiles
    slot = jnp.sum(onehot * tile_start[None, :], axis=1) * tm + rank
    tile_ids = jnp.arange(n_tiles, dtype=jnp.int32)
    tile_group = jnp.minimum(jnp.sum((tile_ids[:, None] >= tile_end[None, :]).astype(jnp.int32), axis=1), N_GROUPS - 1)
    src = jnp.zeros((n_tiles * tm,), jnp.int32).at[slot].set(jnp.arange(n, dtype=jnp.int32))
    r = jnp.arange(n_tiles * tm, dtype=jnp.int32)
    slot_tile = r // tm
    in_group = (jnp.repeat(tile_group, tm)[:, None] == groups[None, :]).astype(jnp.int32)
    slot_rank = r - jnp.sum(in_group * tile_start[None, :], axis=1) * tm
    valid = (slot_tile < tile_end[-1]) & (slot_rank < jnp.sum(in_group * counts[None, :], axis=1))
    dst = jnp.where(valid, src, n + (slot_tile % 2) * tm + r % tm)
    return tile_group.astype(jnp.int32), src, dst, tile_end[-1:].astype(jnp.int32)


def _moe_kernel(tg_ref, src_ref, dst_ref, nused_ref, h_hbm, wgu_ref, wd_ref, y_hbm, hbuf, ybuf, gsem, ssem):
    t = pl.program_id(0)
    n_used = nused_ref[0]
    slot = t % 2
    tm = ybuf.shape[1] // Y_ROWS

    def gather_row(tile, s, r):
        tok = src_ref[tile * tm + r]
        return pltpu.make_async_copy(h_hbm.at[pl.ds(pl.multiple_of(tok * HG_ROWS, HG_ROWS), HG_USED), :],
                                     hbuf.at[s, pl.ds(pl.multiple_of(r * HG_ROWS, HG_ROWS), HG_USED), :], gsem.at[s])

    def scatter_row(tile, s, r):
        row = dst_ref[tile * tm + r]
        return pltpu.make_async_copy(ybuf.at[s, pl.ds(pl.multiple_of(r * Y_ROWS, Y_ROWS), Y_ROWS), :],
                                     y_hbm.at[pl.ds(pl.multiple_of(row * Y_ROWS, Y_ROWS), Y_ROWS), :], ssem.at[s])

    def for_rows(fn):
        def body(r, carry):
            fn(r)
            return carry
        lax.fori_loop(0, tm, body, 0, unroll=8)

    @pl.when(t == 0)
    def _():
        for_rows(lambda r: gather_row(0, 0, r).start())

    @pl.when(t + 1 < n_used)
    def _():
        for_rows(lambda r: gather_row(t + 1, 1 - slot, r).start())

    @pl.when(t < n_used)
    def _():
        for_rows(lambda r: gather_row(t, slot, r).wait())

        @pl.when(t >= 2)
        def _():
            for_rows(lambda r: scatter_row(t - 2, slot, r).wait())

        first = tg_ref[t] * EXPERTS_PER_GROUP
        token_rows = lambda ref, j, per: ref[slot, pl.ds(j, tm, stride=per), :]
        h = jnp.concatenate([token_rows(hbuf, j, HG_ROWS) for j in range(Y_ROWS)], axis=1).astype(BF16)
        gates = token_rows(hbuf, Y_ROWS, HG_ROWS)
        lane = lax.broadcasted_iota(jnp.int32, gates.shape, 1)
        acc = jnp.zeros((tm, D_MODEL), F32)
        for e in range(EXPERTS_PER_GROUP):
            gu = _dot(h, wgu_ref[e])
            a = gu[:, :D_EXPERT]
            ge = jnp.sum(jnp.where(lane == first + e, gates, 0.0), axis=-1, keepdims=True)
            hid = a * _sigmoid(a) * gu[:, D_EXPERT:] * ge
            acc = acc + _dot(hid.astype(BF16), wd_ref[e])
        for j in range(Y_ROWS):
            ybuf[slot, pl.ds(j, tm, stride=Y_ROWS), :] = acc[:, j * LANES:(j + 1) * LANES]
        for_rows(lambda r: scatter_row(t, slot, r).start())

        @pl.when(t == n_used - 1)
        def _():
            for_rows(lambda r: scatter_row(t, slot, r).wait())

            @pl.when(t >= 1)
            def _():
                for_rows(lambda r: scatter_row(t - 1, 1 - slot, r).wait())

            n_rows = y_hbm.shape[0] - 2 * tm * Y_ROWS
            ybuf[0] = jnp.zeros(ybuf.shape[1:], F32)
            for half in range(2):
                fill = pltpu.make_async_copy(
                    ybuf.at[0], y_hbm.at[pl.ds(n_rows + half * tm * Y_ROWS, tm * Y_ROWS), :], ssem.at[half])
                fill.start()
                fill.wait()


def _moe_experts(hg, route, wgu, wd):
    n = hg.shape[0] // HG_ROWS
    tm = TM_MOE
    tile_group, src, dst, n_used = route
    n_tiles = tile_group.shape[0]
    grid_spec = pltpu.PrefetchScalarGridSpec(
        num_scalar_prefetch=4,
        grid=(n_tiles,),
        in_specs=[
            pl.BlockSpec(memory_space=pl.ANY),
            pl.BlockSpec((EXPERTS_PER_GROUP, D_MODEL, 2 * D_EXPERT), lambda t, tg, s, d, nu: (tg[t], 0, 0)),
            pl.BlockSpec((EXPERTS_PER_GROUP, D_EXPERT, D_MODEL), lambda t, tg, s, d, nu: (tg[t], 0, 0)),
        ],
        out_specs=pl.BlockSpec(memory_space=pl.ANY),
        scratch_shapes=[
            pltpu.VMEM((2, tm * HG_ROWS, LANES), F32),
            pltpu.VMEM((2, tm * Y_ROWS, LANES), F32),
            pltpu.SemaphoreType.DMA((2,)),
            pltpu.SemaphoreType.DMA((2,)),
        ],
    )
    return pl.pallas_call(
        _moe_kernel,
        grid_spec=grid_spec,
        out_shape=jax.ShapeDtypeStruct(((n + 2 * tm) * Y_ROWS, LANES), F32),
        compiler_params=_cparams(("arbitrary",), 58),
        name="moe_experts",
    )(tile_group, src, dst, n_used, hg, wgu, wd)


def _moe_out_kernel(x1_ref, y_ref, g2_ref, lg_ref, lb_ref, o_ref):
    o_ref[...] = _moe_residual_norm(x1_ref, y_ref, g2_ref, lg_ref, lb_ref)


def _moe_out(x1, y, g2, ln_g, ln_b, *, latent, seq_len):
    t = x1.shape[0]
    tm = TM_POST
    row = lambda i: (i, 0)
    return pl.pallas_call(
        _moe_out_kernel,
        grid=(t // tm,),
        in_specs=[pl.BlockSpec((tm, D_MODEL), row), pl.BlockSpec((tm * Y_ROWS, LANES), row),
                  pl.BlockSpec((None, 1, D_MODEL), _mod_row_map(latent, seq_len // tm)),
                  _resident((1, D_MODEL)), _resident((1, D_MODEL))],
        out_specs=pl.BlockSpec((tm, D_MODEL), row),
        out_shape=jax.ShapeDtypeStruct((t, D_MODEL), F32),
        compiler_params=_cparams(("arbitrary",), 32),
        name="moe_out",
    )(x1, y, g2, ln_g, ln_b)


def _moe(hg, gates, wgu, wd):
    return _moe_experts(hg, _route(gates[:, GROUP_LANE].astype(jnp.int32), TM_MOE), wgu, wd)


def _head_mean_matrix():
    i = jnp.arange(GQ_Q_WIDTH) // HEAD_DIM
    return jnp.where(i[:, None] == i[None, :], 1.0 / HEAD_DIM, 0.0).astype(BF16)


def _rope_tables(n_tokens):
    t = jnp.arange(n_tokens, dtype=jnp.int32)
    axis_dim = HEAD_DIM // 2
    inv_freq = ROPE_THETA ** (-jnp.arange(0, axis_dim, 2, dtype=F32) / axis_dim)
    ar = (t // GRID_W).astype(F32)[:, None] * inv_freq
    ac = (t % GRID_W).astype(F32)[:, None] * inv_freq
    c = jnp.concatenate([jnp.cos(ar), jnp.cos(ar), jnp.cos(ac), jnp.cos(ac)], axis=1)
    s = jnp.concatenate([-jnp.sin(ar), jnp.sin(ar), -jnp.sin(ac), jnp.sin(ac)], axis=1)
    return jnp.tile(c, (1, LANES // HEAD_DIM)), jnp.tile(s, (1, LANES // HEAD_DIM))


def _na_bias_kernel(rows, t_ref, o_ref):
    masked = jnp.full((GRID_W, GRID_W), NEG_INF, F32)
    for pat, first_qrow in enumerate((0, NA_KH // 2, rows - NA_QROWS)):
        krow0 = min(max(first_qrow - NA_KH // 2, 0), rows - NA_WIN_ROWS)
        for kj in range(NA_WIN_ROWS):
            blocks = []
            for qi in range(NA_QROWS):
                qrow, krow = first_qrow + qi, krow0 + kj
                win0 = min(max(qrow - NA_KH // 2, 0), rows - NA_KH)
                in_window = win0 <= krow < win0 + NA_KH
                blocks.append(t_ref[krow - qrow + NA_KH - 1] if in_window else masked)
            o_ref[pat, kj * GRID_W:(kj + 1) * GRID_W, :] = jnp.concatenate(blocks, axis=1)


def _na_bias_table(rpb, rows):
    col = np.arange(GRID_W)
    col_idx = np.clip(col[:, None] - col[None, :], -(NA_KW - 1), NA_KW - 1) + NA_KW - 1
    onehot = (col_idx[None] == np.arange(2 * NA_KW - 1)[:, None, None]).astype(np.float32)
    col_start = np.clip(col - NA_KW // 2, 0, GRID_W - NA_KW)
    col_ok = (col[:, None] >= col_start[None, :]) & (col[:, None] < col_start[None, :] + NA_KW)
    t = jnp.einsum("lhic,ckq->lhikq", rpb, onehot, precision=lax.Precision.HIGHEST)
    t = jnp.where(col_ok, t, NEG_INF)
    depth, heads, n_rel = t.shape[:3]
    nk, nq = NA_WIN_ROWS * GRID_W, NA_QROWS * GRID_W
    return pl.pallas_call(
        functools.partial(_na_bias_kernel, rows),
        grid=(depth, heads),
        in_specs=[pl.BlockSpec((None, None, n_rel, GRID_W, GRID_W), lambda l, h: (l, h, 0, 0, 0))],
        out_specs=pl.BlockSpec((None, 3, None, nk, nq), lambda l, h: (l, 0, h, 0, 0)),
        out_shape=jax.ShapeDtypeStruct((depth, 3, heads, nk, nq), F32),
        compiler_params=_cparams(("arbitrary", "arbitrary"), 32),
        name="na_bias_table",
    )(t)


def _stream_layer(x, mods, l, wts, *, latent, n_seq, seq_len, cache=None, rope=None):
    sh1, sc1, g1, sh2, sc2, g2 = mods
    pieces = _proj(x, 1.0 + sc1, sh1, wts["w_in"][l], wts["gmat"], wts["qn"][l], wts["kn"][l], rope,
                   latent=latent, seq_len=seq_len)
    if isinstance(x, tuple):
        x, pieces = pieces[0], pieces[1:]
    qa, ka, va, u, qc, kc, vc, gates = pieces
    if latent:
        nat_k, nat_v, gqa_k, gqa_v = cache
        swap = lambda a: jnp.swapaxes(a, 1, 2)
        out_a = _na_latent(qa, ka, swap(va.reshape(n_seq, seq_len, NA_WIDTH)), nat_k, swap(nat_v),
                           wts["na_bias"][l], n_seq=n_seq, seq_len=seq_len)
        k_all = jnp.concatenate([gqa_k, kc.reshape(n_seq, seq_len, GQ_KV_WIDTH)], axis=1)
        v_all = jnp.concatenate([gqa_v, vc.reshape(n_seq, seq_len, GQ_KV_WIDTH)], axis=1)
        out_c = _gqa_latent(qc, k_all, v_all, n_seq=n_seq, seq_len=seq_len)
    else:
        out_a, out_c = _ctx_attention(qa, ka, va, qc, kc, vc, seq_len=seq_len)
    out_b = _conformer_conv(u, wts["conv_w"][l], wts["conv_b"][l], wts["conv_ln_g"][l], wts["conv_ln_b"][l],
                            n_seq=n_seq, seq_len=seq_len)
    x1, hg, moe_gates = _post(out_a, out_b, out_c, gates, x, g1, wts["ln1_g"][l], wts["ln1_b"][l], 1.0 + sc2, sh2,
                         wts["w_br_a"][l], wts["w_br_b"][l], wts["w_br_c"][l], wts["w_out"][l],
                         wts["w_router"], wts["b_router"], latent=latent, seq_len=seq_len)
    y = _moe(hg, moe_gates, wts["w_gate_up"][l], wts["w_down"][l])
    return (x1, y, g2, wts["ln2_g"][l], wts["ln2_b"][l]), (ka, va, kc, vc)


def kernel(x_prompt, x_sample, cache_nat_k, cache_nat_v, cache_gqa_k, cache_gqa_v, c, c_ctx, w_mod, b_mod, w_in, nat_rpb, conv_w, conv_b, conv_ln_g, conv_ln_b, q_norm_g, k_norm_g, w_br_a, w_br_b, w_br_c, w_out, ln1_g, ln1_b, ln2_g, ln2_b, w_router, b_router, w_gate_up, w_down):
    batch, seq, _ = x_prompt.shape
    dec_batch, dec_seq, _ = x_sample.shape
    past = cache_nat_k.shape[2]
    assert dec_batch == 2 and (batch * seq) % TM_MOE == 0 and dec_seq % TM_MOE == 0 and seq % TL_CONV == 0
    assert TM_MOE % TM_PROJ == 0 and TM_MOE % TM_POST == 0

    cond8 = jnp.zeros((8, D_MODEL), F32).at[0].set(c_ctx).at[1:1 + dec_batch].set(c)
    mods_all = _modulation(cond8, w_mod, b_mod)

    pad = LANES - N_EXPERTS
    vecd = lambda a: a.reshape(DEPTH, 1, -1)
    wts = dict(
        w_in=w_in.astype(BF16),
        gmat=_head_mean_matrix(),
        qn=jnp.tile(q_norm_g, (1, GQ_HEADS)).reshape(DEPTH, 1, GQ_Q_WIDTH),
        kn=jnp.tile(k_norm_g, (1, GQ_KV_HEADS)).reshape(DEPTH, 1, GQ_KV_WIDTH),
        na_bias=_na_bias_table(nat_rpb, dec_seq // GRID_W),
        conv_w=conv_w, conv_b=conv_b, conv_ln_g=conv_ln_g, conv_ln_b=conv_ln_b,
        ln1_g=vecd(ln1_g), ln1_b=vecd(ln1_b), ln2_g=vecd(ln2_g), ln2_b=vecd(ln2_b),
        w_br_a=w_br_a.astype(BF16), w_br_b=w_br_b.astype(BF16), w_br_c=w_br_c.astype(BF16),
        w_out=w_out.astype(BF16),
        w_router=jnp.pad(w_router, ((0, 0), (0, pad))),
        b_router=jnp.pad(b_router, (0, pad), constant_values=NEG_INF).reshape(1, LANES),
        w_gate_up=w_gate_up.astype(BF16), w_down=w_down.astype(BF16),
    )
    rope = _rope_tables(dec_seq)

    def layer_mods(l):
        m = mods_all[l, :1 + dec_batch].reshape(1 + dec_batch, 6, 1, D_MODEL)
        return [m[:, i] for i in range(6)]

    xp = x_prompt.reshape(batch * seq, D_MODEL)
    xs = x_sample.reshape(dec_batch * dec_seq, D_MODEL)
    new_cache = []
    for l in range(DEPTH):
        mods = layer_mods(l)
        xp, ctx_kv = _stream_layer(xp, mods, l, wts, latent=False, n_seq=batch, seq_len=seq)
        new_cache.append(ctx_kv)
        cache = (cache_nat_k[:, l].reshape(dec_batch, past, NA_WIDTH).astype(BF16),
                 cache_nat_v[:, l].reshape(dec_batch, past, NA_WIDTH).astype(BF16),
                 cache_gqa_k[:, l].reshape(dec_batch, past, GQ_KV_WIDTH).astype(BF16),
                 cache_gqa_v[:, l].reshape(dec_batch, past, GQ_KV_WIDTH).astype(BF16))
        xs, _ = _stream_layer(xs, mods, l, wts, latent=True, n_seq=dec_batch, seq_len=dec_seq, cache=cache, rope=rope)
    xp = _moe_out(*xp, latent=False, seq_len=seq)
    xs = _moe_out(*xs, latent=True, seq_len=dec_seq)

    def stack(i, heads):
        return jnp.stack([kv[i].reshape(batch, seq, heads, HEAD_DIM) for kv in new_cache], axis=1)

    return (xp.reshape(batch, seq, D_MODEL), xs.reshape(dec_batch, dec_seq, D_MODEL),
            stack(0, NA_HEADS), stack(1, NA_HEADS), stack(2, GQ_KV_HEADS), stack(3, GQ_KV_HEADS))
```

```python
import functools

import numpy as np
import jax
import jax.numpy as jnp
from jax import lax
from jax.experimental import pallas as pl
from jax.experimental.pallas import tpu as pltpu

F32 = jnp.float32
BF16 = jnp.bfloat16

D_MODEL = 1024
DEPTH = 2
GRID_W = 64
HEAD_DIM = 64
NA_HEADS = 8
NA_WIDTH = NA_HEADS * HEAD_DIM
NA_KH = 8
NA_KW = 16
NA_QROWS = 4
NA_WIN_ROWS = NA_QROWS + NA_KH
CONV_CH = 512
CONV_K = 31
GQ_HEADS = 8
GQ_KV_HEADS = 2
GQ_Q_WIDTH = GQ_HEADS * HEAD_DIM
GQ_KV_WIDTH = GQ_KV_HEADS * HEAD_DIM
ROPE_THETA = 10000.0
N_EXPERTS = 16
EXPERTS_PER_GROUP = 4
N_GROUPS = N_EXPERTS // EXPERTS_PER_GROUP
GROUP_LANE = N_EXPERTS
Y_ROWS = D_MODEL // 128
HG_ROWS = 2 * Y_ROWS
HG_USED = Y_ROWS + 1
D_EXPERT = 512
ALPHA = (2 * DEPTH) ** 0.25
LN_EPS = 1e-6
RMS_EPS = 1e-6
NEG_INF = -1e30
QK_SCALE = HEAD_DIM ** -0.5

_OFF = (0, 512, 1024, 1536, 2560, 3072, 3200, 3328, 6400)
IN_COLS = _OFF[-1]

LANES = 128
V7X_VMEM_BYTES = 64 * 2 ** 20
HALO = 16

TM_PROJ = 256
TM_POST = 256
TM_MOE = 512
TL_CONV = 256
TQ_GQA = 256


def _cparams(sem, vmem_mib):
    assert vmem_mib * 2 ** 20 < V7X_VMEM_BYTES
    return pltpu.CompilerParams(dimension_semantics=sem, vmem_limit_bytes=vmem_mib * 2 ** 20)


def _resident(shape):
    nd = len(shape)
    return pl.BlockSpec(shape, lambda *_: (0,) * nd, pipeline_mode=pl.Buffered(1))


def _dot(a, b):
    return jnp.dot(a, b, preferred_element_type=F32)


def _dot_nt(a, b):
    return lax.dot_general(a, b, (((1,), (1,)), ((), ())), preferred_element_type=F32)


def _split_bf16(a):
    hi = a.astype(BF16)
    lo = (a - hi.astype(F32)).astype(BF16)
    return hi, lo


def _dot3(a, b):
    a_hi, a_lo = _split_bf16(a)
    b_hi, b_lo = _split_bf16(b)
    return _dot(a_hi, b_hi) + _dot(a_lo, b_hi) + _dot(a_hi, b_lo)


def _sigmoid(x):
    return 1.0 / (1.0 + jnp.exp(-x))


def _layer_norm(x, g, b):
    mu = jnp.mean(x, axis=-1, keepdims=True)
    xc = x - mu
    var = jnp.mean(xc * xc, axis=-1, keepdims=True)
    return xc * lax.rsqrt(var + LN_EPS) * g + b


def _mod_kernel(cond_ref, w_ref, b_ref, o_ref):
    c = cond_ref[...]
    o_ref[...] = _dot3(c * _sigmoid(c), w_ref[...]) + b_ref[...]


def _modulation(cond8, w_mod, b_mod):
    tn = 1536
    n = w_mod.shape[-1]
    return pl.pallas_call(
        _mod_kernel,
        grid=(DEPTH, n // tn),
        in_specs=[
            pl.BlockSpec((8, D_MODEL), lambda l, j: (0, 0)),
            pl.BlockSpec((None, D_MODEL, tn), lambda l, j: (l, 0, j)),
            pl.BlockSpec((None, 1, tn), lambda l, j: (l, 0, j)),
        ],
        out_specs=pl.BlockSpec((None, 8, tn), lambda l, j: (l, 0, j)),
        out_shape=jax.ShapeDtypeStruct((DEPTH, 8, n), F32),
        compiler_params=_cparams(("arbitrary", "arbitrary"), 40),
        name="modulation",
    )(cond8, w_mod, b_mod.reshape(DEPTH, 1, n))


def _head_rms(x, gmat, gain):
    hi, lo = _split_bf16(x * x)
    ms = _dot(hi, gmat) + _dot(lo, gmat)
    return x * lax.rsqrt(ms + RMS_EPS) * gain


def _rope(x, c, s):
    w = x.shape[1]
    reps = w // LANES
    if reps > 1:
        c = jnp.concatenate([c] * reps, axis=1)
        s = jnp.concatenate([s] * reps, axis=1)
    lane = lax.broadcasted_iota(jnp.int32, x.shape, 1)
    partner = jnp.where((lane % 32) < 16, pltpu.roll(x, w - 16, 1), pltpu.roll(x, 16, 1))
    return x * c + partner * s


def _moe_residual_norm(x1_ref, y_ref, g2_ref, lg_ref, lb_ref):
    tm = x1_ref.shape[0]
    y = jnp.concatenate([y_ref[pl.ds(j, tm, stride=Y_ROWS), :] for j in range(Y_ROWS)], axis=1)
    return _layer_norm(ALPHA * x1_ref[...] + g2_ref[...] * y, lg_ref[...], lb_ref[...])


def _proj_kernel(latent, fused, *refs):
    refs = list(refs)
    n_in = (5 if fused else 1) + 6 + (2 if latent else 0)
    ins, outs = refs[:n_in], refs[n_in:]
    x = _moe_residual_norm(*ins[:5]) if fused else ins[0][...]
    sc_ref, sh_ref, w_ref, gm_ref, qn_ref, kn_ref = ins[n_in - 6 - (2 if latent else 0):][:6]
    if latent:
        c_ref, s_ref = ins[-2:]
    if fused:
        outs.pop(0)[...] = x
    qa_ref, ka_ref, va_ref, u_ref, qc_ref, kc_ref, vc_ref, gt_ref = outs[:8]
    h = (x * sc_ref[...] + sh_ref[...]).astype(BF16)

    def mm(i):
        return _dot(h, w_ref[:, _OFF[i]:_OFF[i + 1]])

    qa_ref[...] = (mm(0) * QK_SCALE).astype(qa_ref.dtype)
    ka, va = mm(1), mm(2)
    u_ref[...] = mm(3).astype(u_ref.dtype)
    qc = _head_rms(mm(4), gm_ref[...], qn_ref[...])
    kc = _head_rms(mm(5), gm_ref[:GQ_KV_WIDTH, :GQ_KV_WIDTH], kn_ref[...])
    if latent:
        qc = _rope(qc, c_ref[...], s_ref[...])
        kc = _rope(kc, c_ref[...], s_ref[...])
    qc_ref[...] = (qc * QK_SCALE).astype(qc_ref.dtype)
    vc = mm(6)
    gt_ref[...] = mm(7).astype(gt_ref.dtype)
    for ref, val in ((ka_ref, ka), (va_ref, va), (kc_ref, kc), (vc_ref, vc)):
        ref[...] = val.astype(ref.dtype)
    for ref, val in zip(outs[8:], (ka, va, kc, vc)):
        ref[...] = val.T


def _mod_row_map(latent, tiles_per_seq):
    if latent:
        return lambda i: (1 + i // tiles_per_seq, 0, 0)
    return lambda i: (0, 0, 0)


def _proj(x, scale, shift, w_bf, gmat, qn, kn, rope, *, latent, seq_len):
    fused = isinstance(x, tuple)
    t = (x[0] if fused else x).shape[0]
    tm = TM_PROJ
    row = lambda i: (i, 0)
    mod_map = _mod_row_map(latent, seq_len // tm)
    if fused:
        in_specs = [pl.BlockSpec((tm, D_MODEL), row), pl.BlockSpec((tm * Y_ROWS, LANES), row),
                    pl.BlockSpec((None, 1, D_MODEL), mod_map), _resident((1, D_MODEL)), _resident((1, D_MODEL))]
        args = list(x)
    else:
        in_specs = [pl.BlockSpec((tm, D_MODEL), row)]
        args = [x]
    in_specs += [
        pl.BlockSpec((None, 1, D_MODEL), mod_map),
        pl.BlockSpec((None, 1, D_MODEL), mod_map),
        _resident((D_MODEL, IN_COLS)),
        _resident((GQ_Q_WIDTH, GQ_Q_WIDTH)),
        _resident((1, GQ_Q_WIDTH)),
        _resident((1, GQ_KV_WIDTH)),
    ]
    args += [scale, shift, w_bf, gmat, qn, kn]
    if latent:
        per = seq_len // tm
        in_specs += [pl.BlockSpec((tm, LANES), lambda i: (i % per, 0))] * 2
        args += list(rope)
    widths = (NA_WIDTH, NA_WIDTH, NA_WIDTH, 2 * CONV_CH, GQ_Q_WIDTH, GQ_KV_WIDTH, GQ_KV_WIDTH, 3 * D_MODEL)
    dtypes = (BF16,) * len(widths)
    if fused:
        widths = (D_MODEL,) + widths
        dtypes = (F32,) + dtypes
    out_specs = [pl.BlockSpec((tm, w), row) for w in widths]
    out_shape = [jax.ShapeDtypeStruct((t, w), dt) for w, dt in zip(widths, dtypes)]
    if not latent:
        per = seq_len // tm
        for w in (NA_WIDTH, NA_WIDTH, GQ_KV_WIDTH, GQ_KV_WIDTH):
            out_specs.append(pl.BlockSpec((None, w, tm), lambda i: (i // per, 0, i % per)))
            out_shape.append(jax.ShapeDtypeStruct((t // seq_len, w, seq_len), F32))
    return pl.pallas_call(
        functools.partial(_proj_kernel, latent, fused),
        grid=(t // tm,),
        in_specs=in_specs,
        out_specs=out_specs,
        out_shape=out_shape,
        compiler_params=_cparams(("arbitrary",), 48),
        name="proj_lat" if latent else "proj_ctx",
    )(*args)


def _pair(ref_or_val, j):
    return ref_or_val[:, j * LANES:(j + 1) * LANES]


def _keep_half(x, half, move_to=None):
    lane = lax.broadcasted_iota(jnp.int32, x.shape, 1)
    keep = (lane >= HEAD_DIM) if half else (lane < HEAD_DIM)
    if move_to is None or move_to == half:
        return jnp.where(keep, x, jnp.zeros_like(x))
    xf = jnp.where(keep, x.astype(F32), 0.0)
    return pltpu.roll(xf, HEAD_DIM, 1).astype(x.dtype)


def _attend_t(qp, pieces):
    scores = []
    for k, _, bias in pieces:
        s = _dot_nt(k, qp)
        scores.append(s if bias is None else s + bias)
    m = None
    for s in scores:
        mi = jnp.max(s, axis=0, keepdims=True)
        m = mi if m is None else jnp.maximum(m, mi)
    l = 0.0
    o = 0.0
    for s, (_, vt, _) in zip(scores, pieces):
        p = jnp.exp(s - m)
        l = l + jnp.sum(p, axis=0, keepdims=True)
        o = o + _dot(vt, p.astype(BF16))
    return o / l


def _select_rows(o_even, o_odd):
    return jnp.concatenate([o_even[:HEAD_DIM], o_odd[HEAD_DIM:]], axis=0)


def _ctx_attn_kernel(qa_ref, ka_ref, vat_ref, qc_ref, kc_ref, vct_ref, oa_ref, oc_ref):
    ka = ka_ref[...]
    vat = vat_ref[...].astype(BF16)
    outs = []
    for j in range(NA_HEADS // 2):
        k = _pair(ka, j)
        vt = vat[j * LANES:(j + 1) * LANES]
        q = _pair(qa_ref, j)
        outs.append(_select_rows(_attend_t(_keep_half(q, 0), [(k, vt, None)]),
                                 _attend_t(_keep_half(q, 1), [(k, vt, None)])))
    oa_ref[...] = jnp.concatenate(outs, axis=0).T.astype(oa_ref.dtype)
    kc = kc_ref[...]
    vct = vct_ref[...].astype(BF16)
    outs = []
    for h in range(GQ_HEADS):
        g = h // (GQ_HEADS // GQ_KV_HEADS)
        o = _attend_t(_keep_half(_pair(qc_ref, h // 2), h % 2, move_to=g), [(kc, vct, None)])
        outs.append(o[g * HEAD_DIM:(g + 1) * HEAD_DIM])
    oc_ref[...] = jnp.concatenate(outs, axis=0).T.astype(oc_ref.dtype)


def _ctx_attention(qa, ka, vat, qc, kc, vct, *, seq_len):
    t = qa.shape[0]
    row = lambda i: (i, 0)
    spec = lambda w: pl.BlockSpec((seq_len, w), row)
    tspec = lambda w: pl.BlockSpec((None, w, seq_len), lambda i: (i, 0, 0))
    return pl.pallas_call(
        _ctx_attn_kernel,
        grid=(t // seq_len,),
        in_specs=[spec(NA_WIDTH), spec(NA_WIDTH), tspec(NA_WIDTH), spec(GQ_Q_WIDTH), spec(GQ_KV_WIDTH),
                  tspec(GQ_KV_WIDTH)],
        out_specs=[spec(NA_WIDTH), spec(GQ_Q_WIDTH)],
        out_shape=[jax.ShapeDtypeStruct((t, NA_WIDTH), BF16), jax.ShapeDtypeStruct((t, GQ_Q_WIDTH), BF16)],
        compiler_params=_cparams(("arbitrary",), 32),
        name="attn_ctx",
    )(qa, ka, vat, qc, kc, vct)


def _na_key_row_start(blk, rows):
    return jnp.clip(blk * NA_QROWS - NA_KH // 2, 0, rows - NA_WIN_ROWS)


def _na_lat_kernel(rows, q_ref, k_ref, vt_ref, kc_ref, vct_ref, bias_ref, o_ref):
    base = pl.multiple_of(_na_key_row_start(pl.program_id(1), rows) * GRID_W, 2 * LANES)
    nk = NA_WIN_ROWS * GRID_W
    kwin = k_ref[pl.ds(base, nk), :]
    outs = []
    for j in range(NA_HEADS // 2):
        k = _pair(kwin, j)
        vt = vt_ref[j * LANES:(j + 1) * LANES, pl.ds(base, nk)]
        kc = _pair(kc_ref, j)
        vct = vct_ref[j * LANES:(j + 1) * LANES, :]
        q = _pair(q_ref, j)
        halves = [_attend_t(_keep_half(q, half), [(k, vt, bias_ref[2 * j + half]), (kc, vct, None)])
                  for half in range(2)]
        outs.append(_select_rows(*halves))
    o_ref[...] = jnp.concatenate(outs, axis=0).T.astype(o_ref.dtype)


def _na_latent(q, k, vt, k_ctx, vt_ctx, bias, *, n_seq, seq_len):
    rows = seq_len // GRID_W
    nblk = rows // NA_QROWS
    tq = NA_QROWS * GRID_W
    pattern = lambda b, i: (jnp.where(i == 0, 0, jnp.where(i == nblk - 1, 2, 1)), 0, 0, 0)
    return pl.pallas_call(
        functools.partial(_na_lat_kernel, rows),
        grid=(n_seq, nblk),
        in_specs=[
            pl.BlockSpec((tq, NA_WIDTH), lambda b, i: (b * nblk + i, 0)),
            pl.BlockSpec((seq_len, NA_WIDTH), lambda b, i: (b, 0)),
            pl.BlockSpec((None, NA_WIDTH, seq_len), lambda b, i: (b, 0, 0)),
            pl.BlockSpec((None,) + k_ctx.shape[1:], lambda b, i: (b, 0, 0)),
            pl.BlockSpec((None,) + vt_ctx.shape[1:], lambda b, i: (b, 0, 0)),
            pl.BlockSpec((None,) + bias.shape[1:], pattern),
        ],
        out_specs=pl.BlockSpec((tq, NA_WIDTH), lambda b, i: (b * nblk + i, 0)),
        out_shape=jax.ShapeDtypeStruct(q.shape, BF16),
        compiler_params=_cparams(("arbitrary", "arbitrary"), 52),
        name="attn_na_lat",
    )(q, k, vt, k_ctx, vt_ctx, bias)


def _gqa_lat_kernel(q_ref, k_ref, v_ref, o_ref):
    k = k_ref[...]
    for h in range(GQ_HEADS):
        g = h // (GQ_HEADS // GQ_KV_HEADS)
        s = _dot_nt(_keep_half(_pair(q_ref, h // 2), h % 2, move_to=g), k)
        p = jnp.exp(s - jnp.max(s, axis=-1, keepdims=True)).astype(BF16)
        o = _dot(p, v_ref[g])
        o_ref[:, h * HEAD_DIM:(h + 1) * HEAD_DIM] = (o[:, :HEAD_DIM] / o[:, HEAD_DIM:HEAD_DIM + 1]).astype(o_ref.dtype)


def _gqa_values_with_ones(v_all):
    b, s, _ = v_all.shape
    tail = jnp.concatenate([jnp.ones((b, s, 1), v_all.dtype), jnp.zeros((b, s, HEAD_DIM - 1), v_all.dtype)], axis=2)
    return jnp.stack([jnp.concatenate([v_all[:, :, g * HEAD_DIM:(g + 1) * HEAD_DIM], tail], axis=2)
                      for g in range(GQ_KV_HEADS)], axis=1)


def _gqa_latent(q, k_all, v_all, *, n_seq, seq_len):
    per = seq_len // TQ_GQA
    s_all = k_all.shape[1]
    return pl.pallas_call(
        _gqa_lat_kernel,
        grid=(n_seq, per),
        in_specs=[
            pl.BlockSpec((TQ_GQA, GQ_Q_WIDTH), lambda b, t: (b * per + t, 0)),
            pl.BlockSpec((None, s_all, GQ_KV_WIDTH), lambda b, t: (b, 0, 0)),
            pl.BlockSpec((None, GQ_KV_HEADS, s_all, 2 * HEAD_DIM), lambda b, t: (b, 0, 0, 0)),
        ],
        out_specs=pl.BlockSpec((TQ_GQA, GQ_Q_WIDTH), lambda b, t: (b * per + t, 0)),
        out_shape=jax.ShapeDtypeStruct(q.shape, BF16),
        compiler_params=_cparams(("arbitrary", "arbitrary"), 48),
        name="attn_gqa_lat",
    )(q, k_all, _gqa_values_with_ones(v_all))


def _conv_kernel(u_ref, up_ref, un_ref, w_ref, b_ref, g_ref, beta_ref, o_ref, shifted_ref):
    t = pl.program_id(1)
    nt = pl.num_programs(1)
    tl = u_ref.shape[0]

    def glu(v):
        return v[:, :CONV_CH].astype(F32) * _sigmoid(v[:, CONV_CH:].astype(F32))

    slab = jnp.concatenate([jnp.where(t > 0, glu(up_ref[...]), 0.0), glu(u_ref[...]),
                            jnp.where(t < nt - 1, glu(un_ref[...]), 0.0)], axis=0)
    first = HALO - CONV_K // 2
    span = shifted_ref.shape[1]
    assert (first + CONV_K - 1) // 8 * 8 + tl == span and span + 7 <= slab.shape[0]
    acc = jnp.zeros((tl, CONV_CH), F32)
    for shift in range(8):
        shifted_ref[shift % 2] = slab[shift:shift + span]
        for k in range(CONV_K):
            if (first + k) % 8 == shift:
                base = first + k - shift
                acc = acc + shifted_ref[shift % 2, base:base + tl, :] * w_ref[k:k + 1, :]
    y = _layer_norm(acc + b_ref[...], g_ref[...], beta_ref[...])
    o_ref[...] = (y * _sigmoid(y)).astype(o_ref.dtype)


def _conformer_conv(u, conv_w, conv_b, ln_g, ln_b, *, n_seq, seq_len):
    tl = TL_CONV
    nt = seq_len // tl
    hb = tl // HALO
    last_hb = seq_len // HALO - 1
    u3 = u.reshape(n_seq, seq_len, 2 * CONV_CH)
    vec = lambda a: a.reshape(1, CONV_CH)
    out = pl.pallas_call(
        _conv_kernel,
        grid=(n_seq, nt),
        in_specs=[
            pl.BlockSpec((None, tl, 2 * CONV_CH), lambda s, t: (s, t, 0)),
            pl.BlockSpec((None, HALO, 2 * CONV_CH), lambda s, t: (s, jnp.maximum(t * hb - 1, 0), 0)),
            pl.BlockSpec((None, HALO, 2 * CONV_CH), lambda s, t: (s, jnp.minimum((t + 1) * hb, last_hb), 0)),
            _resident((CONV_K, CONV_CH)),
            _resident((1, CONV_CH)),
            _resident((1, CONV_CH)),
            _resident((1, CONV_CH)),
        ],
        out_specs=pl.BlockSpec((None, tl, CONV_CH), lambda s, t: (s, t, 0)),
        out_shape=jax.ShapeDtypeStruct((n_seq, seq_len, CONV_CH), BF16),
        scratch_shapes=[pltpu.VMEM((2, (HALO + CONV_K // 2) // 8 * 8 + tl, CONV_CH), F32)],
        compiler_params=_cparams(("arbitrary", "arbitrary"), 32),
        name="conformer_conv",
    )(u3, u3, u3, conv_w, vec(conv_b), vec(ln_g), vec(ln_b))
    return out.reshape(n_seq * seq_len, CONV_CH)


def _router_gates(logits):
    lane = lax.broadcasted_iota(jnp.int32, logits.shape, 1)
    m = jnp.max(logits, axis=-1, keepdims=True)
    e = jnp.exp(logits - m)
    probs = e / jnp.sum(e, axis=-1, keepdims=True)
    p1 = jnp.max(probs, axis=-1, keepdims=True)
    i1 = jnp.min(jnp.where(probs == p1, lane, LANES), axis=-1, keepdims=True)
    in_group = (lane // EXPERTS_PER_GROUP) == (i1 // EXPERTS_PER_GROUP)
    cand = jnp.where(in_group, jnp.where(lane == i1, -1.0, probs), -1.0)
    p2 = jnp.max(cand, axis=-1, keepdims=True)
    i2 = jnp.min(jnp.where(cand == p2, lane, LANES), axis=-1, keepdims=True)
    den = p1 + p2
    gates = jnp.where(lane == i1, p1 / den, 0.0) + jnp.where(lane == i2, p2 / den, 0.0)
    group = (i1 // EXPERTS_PER_GROUP).astype(F32)
    return jnp.where(lane == GROUP_LANE, group, gates)


def _post_kernel(oa_ref, ob_ref, oc_ref, gt_ref, x_ref, g1_ref, lg_ref, lb_ref, sc2_ref, sh2_ref,
                 wa_ref, wb_ref, wc_ref, wo_ref, wr_ref, br_ref, x1_ref, hg_ref, gate_ref):
    sg = _sigmoid(gt_ref[...].astype(F32))
    merged = (sg[:, :D_MODEL] * _dot(oa_ref[...], wa_ref[...])
              + sg[:, D_MODEL:2 * D_MODEL] * _dot(ob_ref[...], wb_ref[...])
              + sg[:, 2 * D_MODEL:] * _dot(oc_ref[...], wc_ref[...]))
    y = _dot(merged.astype(BF16), wo_ref[...])
    x1 = _layer_norm(ALPHA * x_ref[...] + g1_ref[...] * y, lg_ref[...], lb_ref[...])
    x1_ref[...] = x1
    h2 = x1 * sc2_ref[...] + sh2_ref[...]
    tm = x1.shape[0]
    gates = _router_gates(_dot3(h2, wr_ref[...]) + br_ref[...])
    gate_ref[...] = gates
    block = jnp.concatenate([h2, gates, jnp.zeros((tm, (HG_ROWS - HG_USED) * LANES), F32)], axis=1)
    hg_ref[...] = pltpu.einshape("t(jl)->(tj)l", block, j=HG_ROWS)


def _post(oa, ob, oc, gates, x, g1, ln_g, ln_b, scale2, shift2, wa, wb, wc, wo, wr, br, *, latent, seq_len):
    t = x.shape[0]
    tm = TM_POST
    row = lambda i: (i, 0)
    mod_map = _mod_row_map(latent, seq_len // tm)
    mod = pl.BlockSpec((None, 1, D_MODEL), mod_map)
    return pl.pallas_call(
        _post_kernel,
        grid=(t // tm,),
        in_specs=[
            pl.BlockSpec((tm, NA_WIDTH), row), pl.BlockSpec((tm, CONV_CH), row), pl.BlockSpec((tm, GQ_Q_WIDTH), row),
            pl.BlockSpec((tm, 3 * D_MODEL), row), pl.BlockSpec((tm, D_MODEL), row),
            mod, _resident((1, D_MODEL)), _resident((1, D_MODEL)), mod, mod,
            _resident(wa.shape), _resident(wb.shape), _resident(wc.shape), _resident(wo.shape),
            _resident(wr.shape), _resident(br.shape),
        ],
        out_specs=[pl.BlockSpec((tm, D_MODEL), row), pl.BlockSpec((tm * HG_ROWS, LANES), row),
                   pl.BlockSpec((tm, LANES), row)],
        out_shape=[jax.ShapeDtypeStruct((t, D_MODEL), F32), jax.ShapeDtypeStruct((t * HG_ROWS, LANES), F32),
                   jax.ShapeDtypeStruct((t, LANES), F32)],
        compiler_params=_cparams(("arbitrary",), 40),
        name="post_lat" if latent else "post_ctx",
    )(oa, ob, oc, gates, x, g1, ln_g, ln_b, scale2, shift2, wa, wb, wc, wo, wr, br)


def _route(grp, tm):
    n = grp.shape[0]
    n_tiles = n // tm + N_GROUPS
    groups = jnp.arange(N_GROUPS, dtype=jnp.int32)
    onehot = (grp[:, None] == groups[None, :]).astype(jnp.int32)
    csum = jnp.cumsum(onehot, axis=0)
    rank = jnp.sum((csum - onehot) * onehot, axis=1)
    counts = csum[-1]
    tiles = (counts + tm - 1) // tm
    tile_end = jnp.cumsum(tiles)
    tile_start = tile_end - tiles
    slot = jnp.sum(onehot * tile_start[None, :], axis=1) * tm + rank
    tile_ids = jnp.arange(n_tiles, dtype=jnp.int32)
    tile_group = jnp.minimum(jnp.sum((tile_ids[:, None] >= tile_end[None, :]).astype(jnp.int32), axis=1), N_GROUPS - 1)
    src = jnp.zeros((n_tiles * tm,), jnp.int32).at[slot].set(jnp.arange(n, dtype=jnp.int32))
    r = jnp.arange(n_tiles * tm, dtype=jnp.int32)
    slot_tile = r // tm
    in_group = (jnp.repeat(tile_group, tm)[:, None] == groups[None, :]).astype(jnp.int32)
    slot_rank = r - jnp.sum(in_group * tile_start[None, :], axis=1) * tm
    valid = (slot_tile < tile_end[-1]) & (slot_rank < jnp.sum(in_group * counts[None, :], axis=1))
    dst = jnp.where(valid, src, n + (slot_tile % 2) * tm + r % tm)
    return tile_group.astype(jnp.int32), src, dst, tile_end[-1:].astype(jnp.int32)


def _moe_kernel(tg_ref, src_ref, dst_ref, nused_ref, h_hbm, wgu_ref, wd_ref, y_hbm, hbuf, ybuf, gsem, ssem):
    t = pl.program_id(0)
    n_used = nused_ref[0]
    slot = t % 2
    tm = ybuf.shape[1] // Y_ROWS

    def gather_row(tile, s, r):
        tok = src_ref[tile * tm + r]
        return pltpu.make_async_copy(h_hbm.at[pl.ds(pl.multiple_of(tok * HG_ROWS, HG_ROWS), HG_USED), :],
                                     hbuf.at[s, pl.ds(pl.multiple_of(r * HG_ROWS, HG_ROWS), HG_USED), :], gsem.at[s])

    def scatter_row(tile, s, r):
        row = dst_ref[tile * tm + r]
        return pltpu.make_async_copy(ybuf.at[s, pl.ds(pl.multiple_of(r * Y_ROWS, Y_ROWS), Y_ROWS), :],
                                     y_hbm.at[pl.ds(pl.multiple_of(row * Y_ROWS, Y_ROWS), Y_ROWS), :], ssem.at[s])

    def for_rows(fn):
        def body(r, carry):
            fn(r)
            return carry
        lax.fori_loop(0, tm, body, 0, unroll=8)

    @pl.when(t == 0)
    def _():
        for_rows(lambda r: gather_row(0, 0, r).start())

    @pl.when(t + 1 < n_used)
    def _():
        for_rows(lambda r: gather_row(t + 1, 1 - slot, r).start())

    @pl.when(t < n_used)
    def _():
        for_rows(lambda r: gather_row(t, slot, r).wait())

        @pl.when(t >= 2)
        def _():
            for_rows(lambda r: scatter_row(t - 2, slot, r).wait())

        first = tg_ref[t] * EXPERTS_PER_GROUP
        token_rows = lambda ref, j, per: ref[slot, pl.ds(j, tm, stride=per), :]
        h = jnp.concatenate([token_rows(hbuf, j, HG_ROWS) for j in range(Y_ROWS)], axis=1).astype(BF16)
        gates = token_rows(hbuf, Y_ROWS, HG_ROWS)
        lane = lax.broadcasted_iota(jnp.int32, gates.shape, 1)
        acc = jnp.zeros((tm, D_MODEL), F32)
        for e in range(EXPERTS_PER_GROUP):
            gu = _dot(h, wgu_ref[e])
            a = gu[:, :D_EXPERT]
            ge = jnp.sum(jnp.where(lane == first + e, gates, 0.0), axis=-1, keepdims=True)
            hid = a * _sigmoid(a) * gu[:, D_EXPERT:] * ge
            acc = acc + _dot(hid.astype(BF16), wd_ref[e])
        for j in range(Y_ROWS):
            ybuf[slot, pl.ds(j, tm, stride=Y_ROWS), :] = acc[:, j * LANES:(j + 1) * LANES]
        for_rows(lambda r: scatter_row(t, slot, r).start())

        @pl.when(t == n_used - 1)
        def _():
            for_rows(lambda r: scatter_row(t, slot, r).wait())

            @pl.when(t >= 1)
            def _():
                for_rows(lambda r: scatter_row(t - 1, 1 - slot, r).wait())

            n_rows = y_hbm.shape[0] - 2 * tm * Y_ROWS
            ybuf[0] = jnp.zeros(ybuf.shape[1:], F32)
            for half in range(2):
                fill = pltpu.make_async_copy(
                    ybuf.at[0], y_hbm.at[pl.ds(n_rows + half * tm * Y_ROWS, tm * Y_ROWS), :], ssem.at[half])
                fill.start()
                fill.wait()


def _moe_experts(hg, route, wgu, wd):
    n = hg.shape[0] // HG_ROWS
    tm = TM_MOE
    tile_group, src, dst, n_used = route
    n_tiles = tile_group.shape[0]
    grid_spec = pltpu.PrefetchScalarGridSpec(
        num_scalar_prefetch=4,
        grid=(n_tiles,),
        in_specs=[
            pl.BlockSpec(memory_space=pl.ANY),
            pl.BlockSpec((EXPERTS_PER_GROUP, D_MODEL, 2 * D_EXPERT), lambda t, tg, s, d, nu: (tg[t], 0, 0)),
            pl.BlockSpec((EXPERTS_PER_GROUP, D_EXPERT, D_MODEL), lambda t, tg, s, d, nu: (tg[t], 0, 0)),
        ],
        out_specs=pl.BlockSpec(memory_space=pl.ANY),
        scratch_shapes=[
            pltpu.VMEM((2, tm * HG_ROWS, LANES), F32),
            pltpu.VMEM((2, tm * Y_ROWS, LANES), F32),
            pltpu.SemaphoreType.DMA((2,)),
            pltpu.SemaphoreType.DMA((2,)),
        ],
    )
    return pl.pallas_call(
        _moe_kernel,
        grid_spec=grid_spec,
        out_shape=jax.ShapeDtypeStruct(((n + 2 * tm) * Y_ROWS, LANES), F32),
        compiler_params=_cparams(("arbitrary",), 58),
        name="moe_experts",
    )(tile_group, src, dst, n_used, hg, wgu, wd)


def _moe_out_kernel(x1_ref, y_ref, g2_ref, lg_ref, lb_ref, o_ref):
    o_ref[...] = _moe_residual_norm(x1_ref, y_ref, g2_ref, lg_ref, lb_ref)


def _moe_out(x1, y, g2, ln_g, ln_b, *, latent, seq_len):
    t = x1.shape[0]
    tm = TM_POST
    row = lambda i: (i, 0)
    return pl.pallas_call(
        _moe_out_kernel,
        grid=(t // tm,),
        in_specs=[pl.BlockSpec((tm, D_MODEL), row), pl.BlockSpec((tm * Y_ROWS, LANES), row),
                  pl.BlockSpec((None, 1, D_MODEL), _mod_row_map(latent, seq_len // tm)),
                  _resident((1, D_MODEL)), _resident((1, D_MODEL))],
        out_specs=pl.BlockSpec((tm, D_MODEL), row),
        out_shape=jax.ShapeDtypeStruct((t, D_MODEL), F32),
        compiler_params=_cparams(("arbitrary",), 32),
        name="moe_out",
    )(x1, y, g2, ln_g, ln_b)


def _moe(hg, gates, wgu, wd):
    return _moe_experts(hg, _route(gates[:, GROUP_LANE].astype(jnp.int32), TM_MOE), wgu, wd)


def _head_mean_matrix():
    i = jnp.arange(GQ_Q_WIDTH) // HEAD_DIM
    return jnp.where(i[:, None] == i[None, :], 1.0 / HEAD_DIM, 0.0).astype(BF16)


def _rope_tables(n_tokens):
    t = jnp.arange(n_tokens, dtype=jnp.int32)
    axis_dim = HEAD_DIM // 2
    inv_freq = ROPE_THETA ** (-jnp.arange(0, axis_dim, 2, dtype=F32) / axis_dim)
    ar = (t // GRID_W).astype(F32)[:, None] * inv_freq
    ac = (t % GRID_W).astype(F32)[:, None] * inv_freq
    c = jnp.concatenate([jnp.cos(ar), jnp.cos(ar), jnp.cos(ac), jnp.cos(ac)], axis=1)
    s = jnp.concatenate([-jnp.sin(ar), jnp.sin(ar), -jnp.sin(ac), jnp.sin(ac)], axis=1)
    return jnp.tile(c, (1, LANES // HEAD_DIM)), jnp.tile(s, (1, LANES // HEAD_DIM))


def _na_bias_kernel(rows, t_ref, o_ref):
    masked = jnp.full((GRID_W, GRID_W), NEG_INF, F32)
    for pat, first_qrow in enumerate((0, NA_KH // 2, rows - NA_QROWS)):
        krow0 = min(max(first_qrow - NA_KH // 2, 0), rows - NA_WIN_ROWS)
        for kj in range(NA_WIN_ROWS):
            blocks = []
            for qi in range(NA_QROWS):
                qrow, krow = first_qrow + qi, krow0 + kj
                win0 = min(max(qrow - NA_KH // 2, 0), rows - NA_KH)
                in_window = win0 <= krow < win0 + NA_KH
                blocks.append(t_ref[krow - qrow + NA_KH - 1] if in_window else masked)
            o_ref[pat, kj * GRID_W:(kj + 1) * GRID_W, :] = jnp.concatenate(blocks, axis=1)


def _na_bias_table(rpb, rows):
    col = np.arange(GRID_W)
    col_idx = np.clip(col[:, None] - col[None, :], -(NA_KW - 1), NA_KW - 1) + NA_KW - 1
    onehot = (col_idx[None] == np.arange(2 * NA_KW - 1)[:, None, None]).astype(np.float32)
    col_start = np.clip(col - NA_KW // 2, 0, GRID_W - NA_KW)
    col_ok = (col[:, None] >= col_start[None, :]) & (col[:, None] < col_start[None, :] + NA_KW)
    t = jnp.einsum("lhic,ckq->lhikq", rpb, onehot, precision=lax.Precision.HIGHEST)
    t = jnp.where(col_ok, t, NEG_INF)
    depth, heads, n_rel = t.shape[:3]
    nk, nq = NA_WIN_ROWS * GRID_W, NA_QROWS * GRID_W
    return pl.pallas_call(
        functools.partial(_na_bias_kernel, rows),
        grid=(depth, heads),
        in_specs=[pl.BlockSpec((None, None, n_rel, GRID_W, GRID_W), lambda l, h: (l, h, 0, 0, 0))],
        out_specs=pl.BlockSpec((None, 3, None, nk, nq), lambda l, h: (l, 0, h, 0, 0)),
        out_shape=jax.ShapeDtypeStruct((depth, 3, heads, nk, nq), F32),
        compiler_params=_cparams(("arbitrary", "arbitrary"), 32),
        name="na_bias_table",
    )(t)


def _stream_layer(x, mods, l, wts, *, latent, n_seq, seq_len, cache=None, rope=None):
    sh1, sc1, g1, sh2, sc2, g2 = mods
    pieces = _proj(x, 1.0 + sc1, sh1, wts["w_in"][l], wts["gmat"], wts["qn"][l], wts["kn"][l], rope,
                   latent=latent, seq_len=seq_len)
    if isinstance(x, tuple):
        x, pieces = pieces[0], pieces[1:]
    qa, ka, va, u, qc, kc, vc, gates = pieces[:8]
    new_cache_t = pieces[8:]
    if latent:
        nat_k, nat_v, gqa_k, gqa_v = cache
        swap = lambda a: jnp.swapaxes(a, 1, 2)
        out_a = _na_latent(qa, ka, swap(va.reshape(n_seq, seq_len, NA_WIDTH)), nat_k, swap(nat_v),
                           wts["na_bias"][l], n_seq=n_seq, seq_len=seq_len)
        k_all = jnp.concatenate([gqa_k, kc.reshape(n_seq, seq_len, GQ_KV_WIDTH)], axis=1)
        v_all = jnp.concatenate([gqa_v, vc.reshape(n_seq, seq_len, GQ_KV_WIDTH)], axis=1)
        out_c = _gqa_latent(qc, k_all, v_all, n_seq=n_seq, seq_len=seq_len)
    else:
        out_a, out_c = _ctx_attention(qa, ka, new_cache_t[1], qc, kc, new_cache_t[3], seq_len=seq_len)
    out_b = _conformer_conv(u, wts["conv_w"][l], wts["conv_b"][l], wts["conv_ln_g"][l], wts["conv_ln_b"][l],
                            n_seq=n_seq, seq_len=seq_len)
    x1, hg, moe_gates = _post(out_a, out_b, out_c, gates, x, g1, wts["ln1_g"][l], wts["ln1_b"][l], 1.0 + sc2, sh2,
                         wts["w_br_a"][l], wts["w_br_b"][l], wts["w_br_c"][l], wts["w_out"][l],
                         wts["w_router"], wts["b_router"], latent=latent, seq_len=seq_len)
    y = _moe(hg, moe_gates, wts["w_gate_up"][l], wts["w_down"][l])
    return (x1, y, g2, wts["ln2_g"][l], wts["ln2_b"][l]), new_cache_t


def kernel(x_prompt, x_sample, cache_nat_k, cache_nat_v, cache_gqa_k, cache_gqa_v, c, c_ctx, w_mod, b_mod, w_in, nat_rpb, conv_w, conv_b, conv_ln_g, conv_ln_b, q_norm_g, k_norm_g, w_br_a, w_br_b, w_br_c, w_out, ln1_g, ln1_b, ln2_g, ln2_b, w_router, b_router, w_gate_up, w_down):
    batch, seq, _ = x_prompt.shape
    dec_batch, dec_seq, _ = x_sample.shape
    past = cache_nat_k.shape[2]
    assert dec_batch == 2 and (batch * seq) % TM_MOE == 0 and dec_seq % TM_MOE == 0 and seq % TL_CONV == 0
    assert TM_MOE % TM_PROJ == 0 and TM_MOE % TM_POST == 0

    cond8 = jnp.zeros((8, D_MODEL), F32).at[0].set(c_ctx).at[1:1 + dec_batch].set(c)
    mods_all = _modulation(cond8, w_mod, b_mod)

    pad = LANES - N_EXPERTS
    vecd = lambda a: a.reshape(DEPTH, 1, -1)
    per_layer_bf16 = lambda w: [w[l].astype(BF16) for l in range(DEPTH)]
    rows = dec_seq // GRID_W
    wts = dict(
        w_in=per_layer_bf16(w_in),
        gmat=_head_mean_matrix(),
        qn=jnp.tile(q_norm_g, (1, GQ_HEADS)).reshape(DEPTH, 1, GQ_Q_WIDTH),
        kn=jnp.tile(k_norm_g, (1, GQ_KV_HEADS)).reshape(DEPTH, 1, GQ_KV_WIDTH),
        na_bias=[_na_bias_table(nat_rpb[l:l + 1], rows).reshape(3, NA_HEADS, NA_WIN_ROWS * GRID_W, NA_QROWS * GRID_W)
                 for l in range(DEPTH)],
        conv_w=conv_w, conv_b=conv_b, conv_ln_g=conv_ln_g, conv_ln_b=conv_ln_b,
        ln1_g=vecd(ln1_g), ln1_b=vecd(ln1_b), ln2_g=vecd(ln2_g), ln2_b=vecd(ln2_b),
        w_br_a=per_layer_bf16(w_br_a), w_br_b=per_layer_bf16(w_br_b), w_br_c=per_layer_bf16(w_br_c),
        w_out=per_layer_bf16(w_out),
        w_router=jnp.pad(w_router, ((0, 0), (0, pad))),
        b_router=jnp.pad(b_router, (0, pad), constant_values=NEG_INF).reshape(1, LANES),
        w_gate_up=per_layer_bf16(w_gate_up), w_down=per_layer_bf16(w_down),
    )
    rope = _rope_tables(dec_seq)

    def layer_mods(l):
        m = mods_all[l, :1 + dec_batch].reshape(1 + dec_batch, 6, 1, D_MODEL)
        return [m[:, i] for i in range(6)]

    xp = x_prompt.reshape(batch * seq, D_MODEL)
    xs = x_sample.reshape(dec_batch * dec_seq, D_MODEL)
    new_cache = []
    for l in range(DEPTH):
        mods = layer_mods(l)
        xp, ctx_kv = _stream_layer(xp, mods, l, wts, latent=False, n_seq=batch, seq_len=seq)
        new_cache.append(ctx_kv)
        cache = (cache_nat_k[:, l].reshape(dec_batch, past, NA_WIDTH).astype(BF16),
                 cache_nat_v[:, l].reshape(dec_batch, past, NA_WIDTH).astype(BF16),
                 cache_gqa_k[:, l].reshape(dec_batch, past, GQ_KV_WIDTH).astype(BF16),
                 cache_gqa_v[:, l].reshape(dec_batch, past, GQ_KV_WIDTH).astype(BF16))
        xs, _ = _stream_layer(xs, mods, l, wts, latent=True, n_seq=dec_batch, seq_len=dec_seq, cache=cache, rope=rope)
    xp = _moe_out(*xp, latent=False, seq_len=seq)
    xs = _moe_out(*xs, latent=True, seq_len=dec_seq)

    def stack(i, heads):
        per_layer = jnp.stack([kv[i].reshape(batch, heads, HEAD_DIM, seq) for kv in new_cache], axis=1)
        return per_layer.transpose(0, 1, 4, 2, 3)

    return (xp.reshape(batch, seq, D_MODEL), xs.reshape(dec_batch, dec_seq, D_MODEL),
            stack(0, NA_HEADS), stack(1, NA_HEADS), stack(2, GQ_KV_HEADS), stack(3, GQ_KV_HEADS))
```

```python
import functools

import numpy as np
import jax
import jax.numpy as jnp
from jax import lax
from jax.experimental import pallas as pl
from jax.experimental.pallas import tpu as pltpu

F32 = jnp.float32
BF16 = jnp.bfloat16

D_MODEL = 1024
DEPTH = 2
GRID_W = 64
HEAD_DIM = 64
NA_HEADS = 8
NA_WIDTH = NA_HEADS * HEAD_DIM
NA_KH = 8
NA_KW = 16
NA_QROWS = 4
NA_WIN_ROWS = NA_QROWS + NA_KH
CONV_CH = 512
CONV_K = 31
GQ_HEADS = 8
GQ_KV_HEADS = 2
GQ_Q_WIDTH = GQ_HEADS * HEAD_DIM
GQ_KV_WIDTH = GQ_KV_HEADS * HEAD_DIM
ROPE_THETA = 10000.0
N_EXPERTS = 16
EXPERTS_PER_GROUP = 4
N_GROUPS = N_EXPERTS // EXPERTS_PER_GROUP
GROUP_LANE = N_EXPERTS
Y_ROWS = D_MODEL // 128
HG_ROWS = 2 * Y_ROWS
HG_USED = Y_ROWS + 1
D_EXPERT = 512
ALPHA = (2 * DEPTH) ** 0.25
LN_EPS = 1e-6
RMS_EPS = 1e-6
NEG_INF = -1e30
QK_SCALE = HEAD_DIM ** -0.5

_OFF = (0, 512, 1024, 1536, 2560, 3072, 3200, 3328, 6400)
IN_COLS = _OFF[-1]

LANES = 128
V7X_VMEM_BYTES = 64 * 2 ** 20
HALO = 16

TM_PROJ = 256
TM_POST = 256
TM_MOE = 512
TL_CONV = 256
TQ_GQA = 256


def _cparams(sem, vmem_mib):
    assert vmem_mib * 2 ** 20 < V7X_VMEM_BYTES
    return pltpu.CompilerParams(dimension_semantics=sem, vmem_limit_bytes=vmem_mib * 2 ** 20)


def _resident(shape):
    nd = len(shape)
    return pl.BlockSpec(shape, lambda *_: (0,) * nd, pipeline_mode=pl.Buffered(1))


def _dot(a, b):
    return jnp.dot(a, b, preferred_element_type=F32)


def _dot_nt(a, b):
    return lax.dot_general(a, b, (((1,), (1,)), ((), ())), preferred_element_type=F32)


def _split_bf16(a):
    hi = a.astype(BF16)
    lo = (a - hi.astype(F32)).astype(BF16)
    return hi, lo


def _dot3(a, b):
    a_hi, a_lo = _split_bf16(a)
    b_hi, b_lo = _split_bf16(b)
    return _dot(a_hi, b_hi) + _dot(a_lo, b_hi) + _dot(a_hi, b_lo)


def _sigmoid(x):
    return 1.0 / (1.0 + jnp.exp(-x))


def _layer_norm(x, g, b):
    mu = jnp.mean(x, axis=-1, keepdims=True)
    xc = x - mu
    var = jnp.mean(xc * xc, axis=-1, keepdims=True)
    return xc * lax.rsqrt(var + LN_EPS) * g + b


def _mod_kernel(cond_ref, w_ref, b_ref, o_ref):
    c = cond_ref[...]
    o_ref[...] = _dot3(c * _sigmoid(c), w_ref[...]) + b_ref[...]


def _modulation(cond8, w_mod, b_mod):
    tn = 1536
    n = w_mod.shape[-1]
    return pl.pallas_call(
        _mod_kernel,
        grid=(DEPTH, n // tn),
        in_specs=[
            pl.BlockSpec((8, D_MODEL), lambda l, j: (0, 0)),
            pl.BlockSpec((None, D_MODEL, tn), lambda l, j: (l, 0, j)),
            pl.BlockSpec((None, 1, tn), lambda l, j: (l, 0, j)),
        ],
        out_specs=pl.BlockSpec((None, 8, tn), lambda l, j: (l, 0, j)),
        out_shape=jax.ShapeDtypeStruct((DEPTH, 8, n), F32),
        compiler_params=_cparams(("arbitrary", "arbitrary"), 40),
        name="modulation",
    )(cond8, w_mod, b_mod.reshape(DEPTH, 1, n))


def _head_rms(x, gmat, gain):
    hi, lo = _split_bf16(x * x)
    ms = _dot(hi, gmat) + _dot(lo, gmat)
    return x * lax.rsqrt(ms + RMS_EPS) * gain


def _rope(x, c, s):
    w = x.shape[1]
    reps = w // LANES
    if reps > 1:
        c = jnp.concatenate([c] * reps, axis=1)
        s = jnp.concatenate([s] * reps, axis=1)
    lane = lax.broadcasted_iota(jnp.int32, x.shape, 1)
    partner = jnp.where((lane % 32) < 16, pltpu.roll(x, w - 16, 1), pltpu.roll(x, 16, 1))
    return x * c + partner * s


def _moe_residual_norm(x1_ref, y_ref, g2_ref, lg_ref, lb_ref):
    tm = x1_ref.shape[0]
    y = jnp.concatenate([y_ref[pl.ds(j, tm, stride=Y_ROWS), :] for j in range(Y_ROWS)], axis=1)
    return _layer_norm(ALPHA * x1_ref[...] + g2_ref[...] * y, lg_ref[...], lb_ref[...])


def _proj_kernel(latent, fused, *refs):
    refs = list(refs)
    n_in = (5 if fused else 1) + 6 + (2 if latent else 0)
    ins, outs = refs[:n_in], refs[n_in:]
    x = _moe_residual_norm(*ins[:5]) if fused else ins[0][...]
    sc_ref, sh_ref, w_ref, gm_ref, qn_ref, kn_ref = ins[n_in - 6 - (2 if latent else 0):][:6]
    if latent:
        c_ref, s_ref = ins[-2:]
    if fused:
        outs.pop(0)[...] = x
    qa_ref, ka_ref, va_ref, u_ref, qc_ref, kc_ref, vc_ref, gt_ref = outs[:8]
    h = (x * sc_ref[...] + sh_ref[...]).astype(BF16)

    def mm(i):
        return _dot(h, w_ref[:, _OFF[i]:_OFF[i + 1]])

    qa_ref[...] = (mm(0) * QK_SCALE).astype(qa_ref.dtype)
    ka, va = mm(1), mm(2)
    u_ref[...] = mm(3).astype(u_ref.dtype)
    qc = _head_rms(mm(4), gm_ref[...], qn_ref[...])
    kc = _head_rms(mm(5), gm_ref[:GQ_KV_WIDTH, :GQ_KV_WIDTH], kn_ref[...])
    if latent:
        qc = _rope(qc, c_ref[...], s_ref[...])
        kc = _rope(kc, c_ref[...], s_ref[...])
    qc_ref[...] = (qc * QK_SCALE).astype(qc_ref.dtype)
    vc = mm(6)
    gt_ref[...] = mm(7).astype(gt_ref.dtype)
    for ref, val in ((ka_ref, ka), (va_ref, va), (kc_ref, kc), (vc_ref, vc)):
        ref[...] = val.astype(ref.dtype)
    for ref, val in zip(outs[8:], (ka, va, kc, vc)):
        ref[...] = val.T


def _mod_row_map(latent, tiles_per_seq):
    if latent:
        return lambda i: (1 + i // tiles_per_seq, 0, 0)
    return lambda i: (0, 0, 0)


def _proj(x, scale, shift, w_bf, gmat, qn, kn, rope, *, latent, seq_len, layer):
    fused = isinstance(x, tuple)
    t = (x[0] if fused else x).shape[0]
    tm = TM_PROJ
    row = lambda i: (i, 0)
    mod_map = _mod_row_map(latent, seq_len // tm)
    if fused:
        in_specs = [pl.BlockSpec((tm, D_MODEL), row), pl.BlockSpec((tm * Y_ROWS, LANES), row),
                    pl.BlockSpec((None, 1, D_MODEL), mod_map), _resident((1, D_MODEL)), _resident((1, D_MODEL))]
        args = list(x)
    else:
        in_specs = [pl.BlockSpec((tm, D_MODEL), row)]
        args = [x]
    in_specs += [
        pl.BlockSpec((None, 1, D_MODEL), mod_map),
        pl.BlockSpec((None, 1, D_MODEL), mod_map),
        pl.BlockSpec((None, D_MODEL, IN_COLS), lambda i: (layer, 0, 0), pipeline_mode=pl.Buffered(1)),
        _resident((GQ_Q_WIDTH, GQ_Q_WIDTH)),
        _resident((1, GQ_Q_WIDTH)),
        _resident((1, GQ_KV_WIDTH)),
    ]
    args += [scale, shift, w_bf, gmat, qn, kn]
    if latent:
        per = seq_len // tm
        in_specs += [pl.BlockSpec((tm, LANES), lambda i: (i % per, 0))] * 2
        args += list(rope)
    widths = (NA_WIDTH, NA_WIDTH, NA_WIDTH, 2 * CONV_CH, GQ_Q_WIDTH, GQ_KV_WIDTH, GQ_KV_WIDTH, 3 * D_MODEL)
    dtypes = (BF16,) * len(widths)
    if fused:
        widths = (D_MODEL,) + widths
        dtypes = (F32,) + dtypes
    out_specs = [pl.BlockSpec((tm, w), row) for w in widths]
    out_shape = [jax.ShapeDtypeStruct((t, w), dt) for w, dt in zip(widths, dtypes)]
    if not latent:
        per = seq_len // tm
        for w in (NA_WIDTH, NA_WIDTH, GQ_KV_WIDTH, GQ_KV_WIDTH):
            out_specs.append(pl.BlockSpec((None, w, tm), lambda i: (i // per, 0, i % per)))
            out_shape.append(jax.ShapeDtypeStruct((t // seq_len, w, seq_len), F32))
    return pl.pallas_call(
        functools.partial(_proj_kernel, latent, fused),
        grid=(t // tm,),
        in_specs=in_specs,
        out_specs=out_specs,
        out_shape=out_shape,
        compiler_params=_cparams(("arbitrary",), 48),
        name="proj_lat" if latent else "proj_ctx",
    )(*args)


def _pair(ref_or_val, j):
    return ref_or_val[:, j * LANES:(j + 1) * LANES]


def _keep_half(x, half, move_to=None):
    lane = lax.broadcasted_iota(jnp.int32, x.shape, 1)
    keep = (lane >= HEAD_DIM) if half else (lane < HEAD_DIM)
    if move_to is None or move_to == half:
        return jnp.where(keep, x, jnp.zeros_like(x))
    xf = jnp.where(keep, x.astype(F32), 0.0)
    return pltpu.roll(xf, HEAD_DIM, 1).astype(x.dtype)


def _attend_t(qp, pieces):
    scores = []
    for k, _, bias in pieces:
        s = _dot_nt(k, qp)
        scores.append(s if bias is None else s + bias)
    m = None
    for s in scores:
        mi = jnp.max(s, axis=0, keepdims=True)
        m = mi if m is None else jnp.maximum(m, mi)
    l = 0.0
    o = 0.0
    for s, (_, vt, _) in zip(scores, pieces):
        p = jnp.exp(s - m)
        l = l + jnp.sum(p, axis=0, keepdims=True)
        o = o + _dot(vt, p.astype(BF16))
    return o / l


def _select_rows(o_even, o_odd):
    return jnp.concatenate([o_even[:HEAD_DIM], o_odd[HEAD_DIM:]], axis=0)


def _ctx_attn_kernel(qa_ref, ka_ref, vat_ref, qc_ref, kc_ref, vct_ref, oa_ref, oc_ref):
    ka = ka_ref[...]
    vat = vat_ref[...].astype(BF16)
    outs = []
    for j in range(NA_HEADS // 2):
        k = _pair(ka, j)
        vt = vat[j * LANES:(j + 1) * LANES]
        q = _pair(qa_ref, j)
        outs.append(_select_rows(_attend_t(_keep_half(q, 0), [(k, vt, None)]),
                                 _attend_t(_keep_half(q, 1), [(k, vt, None)])))
    oa_ref[...] = jnp.concatenate(outs, axis=0).T.astype(oa_ref.dtype)
    kc = kc_ref[...]
    vct = vct_ref[...].astype(BF16)
    outs = []
    for h in range(GQ_HEADS):
        g = h // (GQ_HEADS // GQ_KV_HEADS)
        o = _attend_t(_keep_half(_pair(qc_ref, h // 2), h % 2, move_to=g), [(kc, vct, None)])
        outs.append(o[g * HEAD_DIM:(g + 1) * HEAD_DIM])
    oc_ref[...] = jnp.concatenate(outs, axis=0).T.astype(oc_ref.dtype)


def _ctx_attention(qa, ka, vat, qc, kc, vct, *, seq_len):
    t = qa.shape[0]
    row = lambda i: (i, 0)
    spec = lambda w: pl.BlockSpec((seq_len, w), row)
    tspec = lambda w: pl.BlockSpec((None, w, seq_len), lambda i: (i, 0, 0))
    return pl.pallas_call(
        _ctx_attn_kernel,
        grid=(t // seq_len,),
        in_specs=[spec(NA_WIDTH), spec(NA_WIDTH), tspec(NA_WIDTH), spec(GQ_Q_WIDTH), spec(GQ_KV_WIDTH),
                  tspec(GQ_KV_WIDTH)],
        out_specs=[spec(NA_WIDTH), spec(GQ_Q_WIDTH)],
        out_shape=[jax.ShapeDtypeStruct((t, NA_WIDTH), BF16), jax.ShapeDtypeStruct((t, GQ_Q_WIDTH), BF16)],
        compiler_params=_cparams(("arbitrary",), 32),
        name="attn_ctx",
    )(qa, ka, vat, qc, kc, vct)


def _na_key_row_start(blk, rows):
    return jnp.clip(blk * NA_QROWS - NA_KH // 2, 0, rows - NA_WIN_ROWS)


def _na_lat_kernel(rows, q_ref, k_ref, vt_ref, kc_ref, vct_ref, bias_ref, o_ref):
    base = pl.multiple_of(_na_key_row_start(pl.program_id(1), rows) * GRID_W, 2 * LANES)
    nk = NA_WIN_ROWS * GRID_W
    kwin = k_ref[pl.ds(base, nk), :]
    outs = []
    for j in range(NA_HEADS // 2):
        k = _pair(kwin, j)
        vt = vt_ref[j * LANES:(j + 1) * LANES, pl.ds(base, nk)]
        kc = _pair(kc_ref, j)
        vct = vct_ref[j * LANES:(j + 1) * LANES, :]
        q = _pair(q_ref, j)
        halves = [_attend_t(_keep_half(q, half), [(k, vt, bias_ref[2 * j + half]), (kc, vct, None)])
                  for half in range(2)]
        outs.append(_select_rows(*halves))
    o_ref[...] = jnp.concatenate(outs, axis=0).T.astype(o_ref.dtype)


def _na_latent(q, k, vt, k_ctx, vt_ctx, bias, *, n_seq, seq_len):
    rows = seq_len // GRID_W
    nblk = rows // NA_QROWS
    tq = NA_QROWS * GRID_W
    pattern = lambda b, i: (jnp.where(i == 0, 0, jnp.where(i == nblk - 1, 2, 1)), 0, 0, 0)
    return pl.pallas_call(
        functools.partial(_na_lat_kernel, rows),
        grid=(n_seq, nblk),
        in_specs=[
            pl.BlockSpec((tq, NA_WIDTH), lambda b, i: (b * nblk + i, 0)),
            pl.BlockSpec((seq_len, NA_WIDTH), lambda b, i: (b, 0)),
            pl.BlockSpec((None, NA_WIDTH, seq_len), lambda b, i: (b, 0, 0)),
            pl.BlockSpec((None,) + k_ctx.shape[1:], lambda b, i: (b, 0, 0)),
            pl.BlockSpec((None,) + vt_ctx.shape[1:], lambda b, i: (b, 0, 0)),
            pl.BlockSpec((None,) + bias.shape[1:], pattern),
        ],
        out_specs=pl.BlockSpec((tq, NA_WIDTH), lambda b, i: (b * nblk + i, 0)),
        out_shape=jax.ShapeDtypeStruct(q.shape, BF16),
        compiler_params=_cparams(("arbitrary", "arbitrary"), 52),
        name="attn_na_lat",
    )(q, k, vt, k_ctx, vt_ctx, bias)


def _gqa_lat_kernel(q_ref, k_ref, v_ref, o_ref):
    k = k_ref[...]
    for h in range(GQ_HEADS):
        g = h // (GQ_HEADS // GQ_KV_HEADS)
        s = _dot_nt(_keep_half(_pair(q_ref, h // 2), h % 2, move_to=g), k)
        p = jnp.exp(s - jnp.max(s, axis=-1, keepdims=True)).astype(BF16)
        o = _dot(p, v_ref[g])
        o_ref[:, h * HEAD_DIM:(h + 1) * HEAD_DIM] = (o[:, :HEAD_DIM] / o[:, HEAD_DIM:HEAD_DIM + 1]).astype(o_ref.dtype)


def _gqa_values_with_ones(v_all):
    b, s, _ = v_all.shape
    tail = jnp.concatenate([jnp.ones((b, s, 1), v_all.dtype), jnp.zeros((b, s, HEAD_DIM - 1), v_all.dtype)], axis=2)
    return jnp.stack([jnp.concatenate([v_all[:, :, g * HEAD_DIM:(g + 1) * HEAD_DIM], tail], axis=2)
                      for g in range(GQ_KV_HEADS)], axis=1)


def _gqa_latent(q, k_all, v_all, *, n_seq, seq_len):
    per = seq_len // TQ_GQA
    s_all = k_all.shape[1]
    return pl.pallas_call(
        _gqa_lat_kernel,
        grid=(n_seq, per),
        in_specs=[
            pl.BlockSpec((TQ_GQA, GQ_Q_WIDTH), lambda b, t: (b * per + t, 0)),
            pl.BlockSpec((None, s_all, GQ_KV_WIDTH), lambda b, t: (b, 0, 0)),
            pl.BlockSpec((None, GQ_KV_HEADS, s_all, 2 * HEAD_DIM), lambda b, t: (b, 0, 0, 0)),
        ],
        out_specs=pl.BlockSpec((TQ_GQA, GQ_Q_WIDTH), lambda b, t: (b * per + t, 0)),
        out_shape=jax.ShapeDtypeStruct(q.shape, BF16),
        compiler_params=_cparams(("arbitrary", "arbitrary"), 48),
        name="attn_gqa_lat",
    )(q, k_all, _gqa_values_with_ones(v_all))


def _conv_kernel(u_ref, up_ref, un_ref, w_ref, b_ref, g_ref, beta_ref, o_ref, shifted_ref):
    t = pl.program_id(1)
    nt = pl.num_programs(1)
    tl = u_ref.shape[0]

    def glu(v):
        return v[:, :CONV_CH].astype(F32) * _sigmoid(v[:, CONV_CH:].astype(F32))

    slab = jnp.concatenate([jnp.where(t > 0, glu(up_ref[...]), 0.0), glu(u_ref[...]),
                            jnp.where(t < nt - 1, glu(un_ref[...]), 0.0)], axis=0)
    first = HALO - CONV_K // 2
    span = shifted_ref.shape[1]
    assert (first + CONV_K - 1) // 8 * 8 + tl == span and span + 7 <= slab.shape[0]
    acc = jnp.zeros((tl, CONV_CH), F32)
    for shift in range(8):
        shifted_ref[shift % 2] = slab[shift:shift + span]
        for k in range(CONV_K):
            if (first + k) % 8 == shift:
                base = first + k - shift
                acc = acc + shifted_ref[shift % 2, base:base + tl, :] * w_ref[k:k + 1, :]
    y = _layer_norm(acc + b_ref[...], g_ref[...], beta_ref[...])
    o_ref[...] = (y * _sigmoid(y)).astype(o_ref.dtype)


def _conformer_conv(u, conv_w, conv_b, ln_g, ln_b, *, n_seq, seq_len):
    tl = TL_CONV
    nt = seq_len // tl
    hb = tl // HALO
    last_hb = seq_len // HALO - 1
    u3 = u.reshape(n_seq, seq_len, 2 * CONV_CH)
    vec = lambda a: a.reshape(1, CONV_CH)
    out = pl.pallas_call(
        _conv_kernel,
        grid=(n_seq, nt),
        in_specs=[
            pl.BlockSpec((None, tl, 2 * CONV_CH), lambda s, t: (s, t, 0)),
            pl.BlockSpec((None, HALO, 2 * CONV_CH), lambda s, t: (s, jnp.maximum(t * hb - 1, 0), 0)),
            pl.BlockSpec((None, HALO, 2 * CONV_CH), lambda s, t: (s, jnp.minimum((t + 1) * hb, last_hb), 0)),
            _resident((CONV_K, CONV_CH)),
            _resident((1, CONV_CH)),
            _resident((1, CONV_CH)),
            _resident((1, CONV_CH)),
        ],
        out_specs=pl.BlockSpec((None, tl, CONV_CH), lambda s, t: (s, t, 0)),
        out_shape=jax.ShapeDtypeStruct((n_seq, seq_len, CONV_CH), BF16),
        scratch_shapes=[pltpu.VMEM((2, (HALO + CONV_K // 2) // 8 * 8 + tl, CONV_CH), F32)],
        compiler_params=_cparams(("arbitrary", "arbitrary"), 32),
        name="conformer_conv",
    )(u3, u3, u3, conv_w, vec(conv_b), vec(ln_g), vec(ln_b))
    return out.reshape(n_seq * seq_len, CONV_CH)


def _router_gates(logits):
    lane = lax.broadcasted_iota(jnp.int32, logits.shape, 1)
    m = jnp.max(logits, axis=-1, keepdims=True)
    e = jnp.exp(logits - m)
    probs = e / jnp.sum(e, axis=-1, keepdims=True)
    p1 = jnp.max(probs, axis=-1, keepdims=True)
    i1 = jnp.min(jnp.where(probs == p1, lane, LANES), axis=-1, keepdims=True)
    in_group = (lane // EXPERTS_PER_GROUP) == (i1 // EXPERTS_PER_GROUP)
    cand = jnp.where(in_group, jnp.where(lane == i1, -1.0, probs), -1.0)
    p2 = jnp.max(cand, axis=-1, keepdims=True)
    i2 = jnp.min(jnp.where(cand == p2, lane, LANES), axis=-1, keepdims=True)
    den = p1 + p2
    gates = jnp.where(lane == i1, p1 / den, 0.0) + jnp.where(lane == i2, p2 / den, 0.0)
    group = (i1 // EXPERTS_PER_GROUP).astype(F32)
    return jnp.where(lane == GROUP_LANE, group, gates)


def _post_kernel(oa_ref, ob_ref, oc_ref, gt_ref, x_ref, g1_ref, lg_ref, lb_ref, sc2_ref, sh2_ref,
                 wa_ref, wb_ref, wc_ref, wo_ref, wr_ref, br_ref, x1_ref, hg_ref, gate_ref):
    sg = _sigmoid(gt_ref[...].astype(F32))
    merged = (sg[:, :D_MODEL] * _dot(oa_ref[...], wa_ref[...])
              + sg[:, D_MODEL:2 * D_MODEL] * _dot(ob_ref[...], wb_ref[...])
              + sg[:, 2 * D_MODEL:] * _dot(oc_ref[...], wc_ref[...]))
    y = _dot(merged.astype(BF16), wo_ref[...])
    x1 = _layer_norm(ALPHA * x_ref[...] + g1_ref[...] * y, lg_ref[...], lb_ref[...])
    x1_ref[...] = x1
    h2 = x1 * sc2_ref[...] + sh2_ref[...]
    tm = x1.shape[0]
    gates = _router_gates(_dot3(h2, wr_ref[...]) + br_ref[...])
    gate_ref[...] = gates
    block = jnp.concatenate([h2, gates, jnp.zeros((tm, (HG_ROWS - HG_USED) * LANES), F32)], axis=1)
    hg_ref[...] = pltpu.einshape("t(jl)->(tj)l", block, j=HG_ROWS)


def _post(oa, ob, oc, gates, x, g1, ln_g, ln_b, scale2, shift2, wa, wb, wc, wo, wr, br, *, latent, seq_len):
    t = x.shape[0]
    tm = TM_POST
    row = lambda i: (i, 0)
    mod_map = _mod_row_map(latent, seq_len // tm)
    mod = pl.BlockSpec((None, 1, D_MODEL), mod_map)
    return pl.pallas_call(
        _post_kernel,
        grid=(t // tm,),
        in_specs=[
            pl.BlockSpec((tm, NA_WIDTH), row), pl.BlockSpec((tm, CONV_CH), row), pl.BlockSpec((tm, GQ_Q_WIDTH), row),
            pl.BlockSpec((tm, 3 * D_MODEL), row), pl.BlockSpec((tm, D_MODEL), row),
            mod, _resident((1, D_MODEL)), _resident((1, D_MODEL)), mod, mod,
            _resident(wa.shape), _resident(wb.shape), _resident(wc.shape), _resident(wo.shape),
            _resident(wr.shape), _resident(br.shape),
        ],
        out_specs=[pl.BlockSpec((tm, D_MODEL), row), pl.BlockSpec((tm * HG_ROWS, LANES), row),
                   pl.BlockSpec((tm, LANES), row)],
        out_shape=[jax.ShapeDtypeStruct((t, D_MODEL), F32), jax.ShapeDtypeStruct((t * HG_ROWS, LANES), F32),
                   jax.ShapeDtypeStruct((t, LANES), F32)],
        compiler_params=_cparams(("arbitrary",), 40),
        name="post_lat" if latent else "post_ctx",
    )(oa, ob, oc, gates, x, g1, ln_g, ln_b, scale2, shift2, wa, wb, wc, wo, wr, br)


def _route(grp, tm):
    n = grp.shape[0]
    n_tiles = n // tm + N_GROUPS
    groups = jnp.arange(N_GROUPS, dtype=jnp.int32)
    onehot = (grp[:, None] == groups[None, :]).astype(jnp.int32)
    csum = jnp.cumsum(onehot, axis=0)
    rank = jnp.sum((csum - onehot) * onehot, axis=1)
    counts = csum[-1]
    tiles = (counts + tm - 1) // tm
    tile_end = jnp.cumsum(tiles)
    tile_start = tile_end - tiles
    slot = jnp.sum(onehot * tile_start[None, :], axis=1) * tm + rank
    tile_ids = jnp.arange(n_tiles, dtype=jnp.int32)
    tile_group = jnp.minimum(jnp.sum((tile_ids[:, None] >= tile_end[None, :]).astype(jnp.int32), axis=1), N_GROUPS - 1)
    src = jnp.zeros((n_tiles * tm,), jnp.int32).at[slot].set(jnp.arange(n, dtype=jnp.int32))
    r = jnp.arange(n_tiles * tm, dtype=jnp.int32)
    slot_tile = r // tm
    in_group = (jnp.repeat(tile_group, tm)[:, None] == groups[None, :]).astype(jnp.int32)
    slot_rank = r - jnp.sum(in_group * tile_start[None, :], axis=1) * tm
    valid = (slot_tile < tile_end[-1]) & (slot_rank < jnp.sum(in_group * counts[None, :], axis=1))
    dst = jnp.where(valid, src, n + (slot_tile % 2) * tm + r % tm)
    return tile_group.astype(jnp.int32), src, dst, tile_end[-1:].astype(jnp.int32)


def _moe_kernel(tg_ref, src_ref, dst_ref, nused_ref, h_hbm, wgu_ref, wd_ref, y_hbm, hbuf, ybuf, gsem, ssem):
    t = pl.program_id(0)
    n_used = nused_ref[0]
    slot = t % 2
    tm = ybuf.shape[1] // Y_ROWS

    def gather_row(tile, s, r):
        tok = src_ref[tile * tm + r]
        return pltpu.make_async_copy(h_hbm.at[pl.ds(pl.multiple_of(tok * HG_ROWS, HG_ROWS), HG_USED), :],
                                     hbuf.at[s, pl.ds(pl.multiple_of(r * HG_ROWS, HG_ROWS), HG_USED), :], gsem.at[s])

    def scatter_row(tile, s, r):
        row = dst_ref[tile * tm + r]
        return pltpu.make_async_copy(ybuf.at[s, pl.ds(pl.multiple_of(r * Y_ROWS, Y_ROWS), Y_ROWS), :],
                                     y_hbm.at[pl.ds(pl.multiple_of(row * Y_ROWS, Y_ROWS), Y_ROWS), :], ssem.at[s])

    def for_rows(fn):
        def body(r, carry):
            fn(r)
            return carry
        lax.fori_loop(0, tm, body, 0, unroll=8)

    @pl.when(t == 0)
    def _():
        for_rows(lambda r: gather_row(0, 0, r).start())

    @pl.when(t + 1 < n_used)
    def _():
        for_rows(lambda r: gather_row(t + 1, 1 - slot, r).start())

    @pl.when(t < n_used)
    def _():
        for_rows(lambda r: gather_row(t, slot, r).wait())

        @pl.when(t >= 2)
        def _():
            for_rows(lambda r: scatter_row(t - 2, slot, r).wait())

        first = tg_ref[t] * EXPERTS_PER_GROUP
        token_rows = lambda ref, j, per: ref[slot, pl.ds(j, tm, stride=per), :]
        h = jnp.concatenate([token_rows(hbuf, j, HG_ROWS) for j in range(Y_ROWS)], axis=1).astype(BF16)
        gates = token_rows(hbuf, Y_ROWS, HG_ROWS)
        lane = lax.broadcasted_iota(jnp.int32, gates.shape, 1)
        acc = jnp.zeros((tm, D_MODEL), F32)
        for e in range(EXPERTS_PER_GROUP):
            gu = _dot(h, wgu_ref[e])
            a = gu[:, :D_EXPERT]
            ge = jnp.sum(jnp.where(lane == first + e, gates, 0.0), axis=-1, keepdims=True)
            hid = a * _sigmoid(a) * gu[:, D_EXPERT:] * ge
            acc = acc + _dot(hid.astype(BF16), wd_ref[e])
        for j in range(Y_ROWS):
            ybuf[slot, pl.ds(j, tm, stride=Y_ROWS), :] = acc[:, j * LANES:(j + 1) * LANES]
        for_rows(lambda r: scatter_row(t, slot, r).start())

        @pl.when(t == n_used - 1)
        def _():
            for_rows(lambda r: scatter_row(t, slot, r).wait())

            @pl.when(t >= 1)
            def _():
                for_rows(lambda r: scatter_row(t - 1, 1 - slot, r).wait())

            n_rows = y_hbm.shape[0] - 2 * tm * Y_ROWS
            ybuf[0] = jnp.zeros(ybuf.shape[1:], F32)
            for half in range(2):
                fill = pltpu.make_async_copy(
                    ybuf.at[0], y_hbm.at[pl.ds(n_rows + half * tm * Y_ROWS, tm * Y_ROWS), :], ssem.at[half])
                fill.start()
                fill.wait()


def _moe_experts(hg, route, wgu, wd, layer):
    n = hg.shape[0] // HG_ROWS
    tm = TM_MOE
    tile_group, src, dst, n_used = route
    n_tiles = tile_group.shape[0]
    grid_spec = pltpu.PrefetchScalarGridSpec(
        num_scalar_prefetch=4,
        grid=(n_tiles,),
        in_specs=[
            pl.BlockSpec(memory_space=pl.ANY),
            pl.BlockSpec((None, EXPERTS_PER_GROUP, D_MODEL, 2 * D_EXPERT), lambda t, tg, s, d, nu: (layer, tg[t], 0, 0)),
            pl.BlockSpec((None, EXPERTS_PER_GROUP, D_EXPERT, D_MODEL), lambda t, tg, s, d, nu: (layer, tg[t], 0, 0)),
        ],
        out_specs=pl.BlockSpec(memory_space=pl.ANY),
        scratch_shapes=[
            pltpu.VMEM((2, tm * HG_ROWS, LANES), F32),
            pltpu.VMEM((2, tm * Y_ROWS, LANES), F32),
            pltpu.SemaphoreType.DMA((2,)),
            pltpu.SemaphoreType.DMA((2,)),
        ],
    )
    return pl.pallas_call(
        _moe_kernel,
        grid_spec=grid_spec,
        out_shape=jax.ShapeDtypeStruct(((n + 2 * tm) * Y_ROWS, LANES), F32),
        compiler_params=_cparams(("arbitrary",), 58),
        name="moe_experts",
    )(tile_group, src, dst, n_used, hg, wgu, wd)


def _moe_out_kernel(x1_ref, y_ref, g2_ref, lg_ref, lb_ref, o_ref):
    o_ref[...] = _moe_residual_norm(x1_ref, y_ref, g2_ref, lg_ref, lb_ref)


def _moe_out(x1, y, g2, ln_g, ln_b, *, latent, seq_len):
    t = x1.shape[0]
    tm = TM_POST
    row = lambda i: (i, 0)
    return pl.pallas_call(
        _moe_out_kernel,
        grid=(t // tm,),
        in_specs=[pl.BlockSpec((tm, D_MODEL), row), pl.BlockSpec((tm * Y_ROWS, LANES), row),
                  pl.BlockSpec((None, 1, D_MODEL), _mod_row_map(latent, seq_len // tm)),
                  _resident((1, D_MODEL)), _resident((1, D_MODEL))],
        out_specs=pl.BlockSpec((tm, D_MODEL), row),
        out_shape=jax.ShapeDtypeStruct((t, D_MODEL), F32),
        compiler_params=_cparams(("arbitrary",), 32),
        name="moe_out",
    )(x1, y, g2, ln_g, ln_b)


def _moe(hg, gates, wgu, wd, layer):
    return _moe_experts(hg, _route(gates[:, GROUP_LANE].astype(jnp.int32), TM_MOE), wgu, wd, layer)


def _head_mean_matrix():
    i = jnp.arange(GQ_Q_WIDTH) // HEAD_DIM
    return jnp.where(i[:, None] == i[None, :], 1.0 / HEAD_DIM, 0.0).astype(BF16)


def _rope_tables(n_tokens):
    t = jnp.arange(n_tokens, dtype=jnp.int32)
    axis_dim = HEAD_DIM // 2
    inv_freq = ROPE_THETA ** (-jnp.arange(0, axis_dim, 2, dtype=F32) / axis_dim)
    ar = (t // GRID_W).astype(F32)[:, None] * inv_freq
    ac = (t % GRID_W).astype(F32)[:, None] * inv_freq
    c = jnp.concatenate([jnp.cos(ar), jnp.cos(ar), jnp.cos(ac), jnp.cos(ac)], axis=1)
    s = jnp.concatenate([-jnp.sin(ar), jnp.sin(ar), -jnp.sin(ac), jnp.sin(ac)], axis=1)
    return jnp.tile(c, (1, LANES // HEAD_DIM)), jnp.tile(s, (1, LANES // HEAD_DIM))


def _na_bias_kernel(rows, t_ref, o_ref):
    masked = jnp.full((GRID_W, GRID_W), NEG_INF, F32)
    for pat, first_qrow in enumerate((0, NA_KH // 2, rows - NA_QROWS)):
        krow0 = min(max(first_qrow - NA_KH // 2, 0), rows - NA_WIN_ROWS)
        for kj in range(NA_WIN_ROWS):
            blocks = []
            for qi in range(NA_QROWS):
                qrow, krow = first_qrow + qi, krow0 + kj
                win0 = min(max(qrow - NA_KH // 2, 0), rows - NA_KH)
                in_window = win0 <= krow < win0 + NA_KH
                blocks.append(t_ref[krow - qrow + NA_KH - 1] if in_window else masked)
            o_ref[pat, kj * GRID_W:(kj + 1) * GRID_W, :] = jnp.concatenate(blocks, axis=1)


def _na_bias_table(rpb, rows):
    col = np.arange(GRID_W)
    col_idx = np.clip(col[:, None] - col[None, :], -(NA_KW - 1), NA_KW - 1) + NA_KW - 1
    onehot = (col_idx[None] == np.arange(2 * NA_KW - 1)[:, None, None]).astype(np.float32)
    col_start = np.clip(col - NA_KW // 2, 0, GRID_W - NA_KW)
    col_ok = (col[:, None] >= col_start[None, :]) & (col[:, None] < col_start[None, :] + NA_KW)
    t = jnp.einsum("lhic,ckq->lhikq", rpb, onehot, precision=lax.Precision.HIGHEST)
    t = jnp.where(col_ok, t, NEG_INF)
    depth, heads, n_rel = t.shape[:3]
    nk, nq = NA_WIN_ROWS * GRID_W, NA_QROWS * GRID_W
    return pl.pallas_call(
        functools.partial(_na_bias_kernel, rows),
        grid=(depth, heads),
        in_specs=[pl.BlockSpec((None, None, n_rel, GRID_W, GRID_W), lambda l, h: (l, h, 0, 0, 0))],
        out_specs=pl.BlockSpec((None, 3, None, nk, nq), lambda l, h: (l, 0, h, 0, 0)),
        out_shape=jax.ShapeDtypeStruct((depth, 3, heads, nk, nq), F32),
        compiler_params=_cparams(("arbitrary", "arbitrary"), 32),
        name="na_bias_table",
    )(t)


def _stream_layer(x, mods, l, wts, *, latent, n_seq, seq_len, cache=None, rope=None):
    sh1, sc1, g1, sh2, sc2, g2 = mods
    pieces = _proj(x, 1.0 + sc1, sh1, wts["w_in"], wts["gmat"], wts["qn"][l], wts["kn"][l], rope,
                   latent=latent, seq_len=seq_len, layer=l)
    if isinstance(x, tuple):
        x, pieces = pieces[0], pieces[1:]
    qa, ka, va, u, qc, kc, vc, gates = pieces[:8]
    new_cache_t = pieces[8:]
    if latent:
        nat_k, nat_v, gqa_k, gqa_v = cache
        swap = lambda a: jnp.swapaxes(a, 1, 2)
        out_a = _na_latent(qa, ka, swap(va.reshape(n_seq, seq_len, NA_WIDTH)), nat_k, swap(nat_v),
                           wts["na_bias"][l], n_seq=n_seq, seq_len=seq_len)
        k_all = jnp.concatenate([gqa_k, kc.reshape(n_seq, seq_len, GQ_KV_WIDTH)], axis=1)
        v_all = jnp.concatenate([gqa_v, vc.reshape(n_seq, seq_len, GQ_KV_WIDTH)], axis=1)
        out_c = _gqa_latent(qc, k_all, v_all, n_seq=n_seq, seq_len=seq_len)
    else:
        out_a, out_c = _ctx_attention(qa, ka, new_cache_t[1], qc, kc, new_cache_t[3], seq_len=seq_len)
    out_b = _conformer_conv(u, wts["conv_w"][l], wts["conv_b"][l], wts["conv_ln_g"][l], wts["conv_ln_b"][l],
                            n_seq=n_seq, seq_len=seq_len)
    x1, hg, moe_gates = _post(out_a, out_b, out_c, gates, x, g1, wts["ln1_g"][l], wts["ln1_b"][l], 1.0 + sc2, sh2,
                         wts["w_br_a"][l], wts["w_br_b"][l], wts["w_br_c"][l], wts["w_out"][l],
                         wts["w_router"], wts["b_router"], latent=latent, seq_len=seq_len)
    y = _moe(hg, moe_gates, wts["w_gate_up"], wts["w_down"], l)
    return (x1, y, g2, wts["ln2_g"][l], wts["ln2_b"][l]), new_cache_t


def kernel(x_prompt, x_sample, cache_nat_k, cache_nat_v, cache_gqa_k, cache_gqa_v, c, c_ctx, w_mod, b_mod, w_in, nat_rpb, conv_w, conv_b, conv_ln_g, conv_ln_b, q_norm_g, k_norm_g, w_br_a, w_br_b, w_br_c, w_out, ln1_g, ln1_b, ln2_g, ln2_b, w_router, b_router, w_gate_up, w_down):
    batch, seq, _ = x_prompt.shape
    dec_batch, dec_seq, _ = x_sample.shape
    past = cache_nat_k.shape[2]
    assert dec_batch == 2 and (batch * seq) % TM_MOE == 0 and dec_seq % TM_MOE == 0 and seq % TL_CONV == 0
    assert TM_MOE % TM_PROJ == 0 and TM_MOE % TM_POST == 0

    cond8 = jnp.zeros((8, D_MODEL), F32).at[0].set(c_ctx).at[1:1 + dec_batch].set(c)
    mods_all = _modulation(cond8, w_mod, b_mod)

    pad = LANES - N_EXPERTS
    vecd = lambda a: a.reshape(DEPTH, 1, -1)
    per_layer_bf16 = lambda w: [w[l].astype(BF16) for l in range(DEPTH)]
    rows = dec_seq // GRID_W
    wts = dict(
        w_in=w_in.astype(BF16),
        gmat=_head_mean_matrix(),
        qn=jnp.tile(q_norm_g, (1, GQ_HEADS)).reshape(DEPTH, 1, GQ_Q_WIDTH),
        kn=jnp.tile(k_norm_g, (1, GQ_KV_HEADS)).reshape(DEPTH, 1, GQ_KV_WIDTH),
        na_bias=[_na_bias_table(nat_rpb[l:l + 1], rows).reshape(3, NA_HEADS, NA_WIN_ROWS * GRID_W, NA_QROWS * GRID_W)
                 for l in range(DEPTH)],
        conv_w=conv_w, conv_b=conv_b, conv_ln_g=conv_ln_g, conv_ln_b=conv_ln_b,
        ln1_g=vecd(ln1_g), ln1_b=vecd(ln1_b), ln2_g=vecd(ln2_g), ln2_b=vecd(ln2_b),
        w_br_a=per_layer_bf16(w_br_a), w_br_b=per_layer_bf16(w_br_b), w_br_c=per_layer_bf16(w_br_c),
        w_out=per_layer_bf16(w_out),
        w_router=jnp.pad(w_router, ((0, 0), (0, pad))),
        b_router=jnp.pad(b_router, (0, pad), constant_values=NEG_INF).reshape(1, LANES),
        w_gate_up=w_gate_up.astype(BF16), w_down=w_down.astype(BF16),
    )
    rope = _rope_tables(dec_seq)

    def layer_mods(l):
        m = mods_all[l, :1 + dec_batch].reshape(1 + dec_batch, 6, 1, D_MODEL)
        return [m[:, i] for i in range(6)]

    xp = x_prompt.reshape(batch * seq, D_MODEL)
    xs = x_sample.reshape(dec_batch * dec_seq, D_MODEL)
    new_cache = []
    for l in range(DEPTH):
        mods = layer_mods(l)
        xp, ctx_kv = _stream_layer(xp, mods, l, wts, latent=False, n_seq=batch, seq_len=seq)
        new_cache.append(ctx_kv)
        cache = (cache_nat_k[:, l].reshape(dec_batch, past, NA_WIDTH).astype(BF16),
                 cache_nat_v[:, l].reshape(dec_batch, past, NA_WIDTH).astype(BF16),
                 cache_gqa_k[:, l].reshape(dec_batch, past, GQ_KV_WIDTH).astype(BF16),
                 cache_gqa_v[:, l].reshape(dec_batch, past, GQ_KV_WIDTH).astype(BF16))
        xs, _ = _stream_layer(xs, mods, l, wts, latent=True, n_seq=dec_batch, seq_len=dec_seq, cache=cache, rope=rope)
    xp = _moe_out(*xp, latent=False, seq_len=seq)
    xs = _moe_out(*xs, latent=True, seq_len=dec_seq)

    def stack(i, heads):
        per_layer = jnp.stack([kv[i].reshape(batch, heads, HEAD_DIM, seq) for kv in new_cache], axis=1)
        return per_layer.transpose(0, 1, 4, 2, 3)

    return (xp.reshape(batch, seq, D_MODEL), xs.reshape(dec_batch, dec_seq, D_MODEL),
            stack(0, NA_HEADS), stack(1, NA_HEADS), stack(2, GQ_KV_HEADS), stack(3, GQ_KV_HEADS))
```

```python
import functools

import numpy as np
import jax
import jax.numpy as jnp
from jax import lax
from jax.experimental import pallas as pl
from jax.experimental.pallas import tpu as pltpu

F32 = jnp.float32
BF16 = jnp.bfloat16

D_MODEL = 1024
DEPTH = 2
GRID_W = 64
HEAD_DIM = 64
NA_HEADS = 8
NA_WIDTH = NA_HEADS * HEAD_DIM
NA_KH = 8
NA_KW = 16
NA_QROWS = 4
NA_WIN_ROWS = NA_QROWS + NA_KH
CONV_CH = 512
CONV_K = 31
GQ_HEADS = 8
GQ_KV_HEADS = 2
GQ_Q_WIDTH = GQ_HEADS * HEAD_DIM
GQ_KV_WIDTH = GQ_KV_HEADS * HEAD_DIM
ROPE_THETA = 10000.0
N_EXPERTS = 16
EXPERTS_PER_GROUP = 4
N_GROUPS = N_EXPERTS // EXPERTS_PER_GROUP
GROUP_LANE = N_EXPERTS
D_EXPERT = 512
ALPHA = (2 * DEPTH) ** 0.25
LN_EPS = 1e-6
RMS_EPS = 1e-6
NEG_INF = -1e30
QK_SCALE = HEAD_DIM ** -0.5

LANES = 128
Y_ROWS = D_MODEL // LANES
HG_ROWS = 2 * Y_ROWS
HG_USED = Y_ROWS + 1

IN_SIZES = (NA_WIDTH, NA_WIDTH, NA_WIDTH, 2 * CONV_CH, GQ_Q_WIDTH, GQ_KV_WIDTH, GQ_KV_WIDTH, 3 * D_MODEL)
_OFF = tuple(int(o) for o in np.cumsum((0,) + IN_SIZES))
IN_COLS = _OFF[-1]
ROPE_AXIS = HEAD_DIM // 2
V7X_VMEM_BYTES = 64 * 2 ** 20
HALO = 16

TM_PROJ = 256
TM_POST = 256
TM_MOE = 512
TL_CONV = 256
TQ_GQA = 256


def _cparams(sem, vmem_mib):
    assert vmem_mib * 2 ** 20 < V7X_VMEM_BYTES
    return pltpu.CompilerParams(dimension_semantics=sem, vmem_limit_bytes=vmem_mib * 2 ** 20)


def _resident(shape):
    nd = len(shape)
    return pl.BlockSpec(shape, lambda *_: (0,) * nd, pipeline_mode=pl.Buffered(1))


def _dot(a, b):
    return jnp.dot(a, b, preferred_element_type=F32)


def _dot_nt(a, b):
    return lax.dot_general(a, b, (((1,), (1,)), ((), ())), preferred_element_type=F32)


def _split_bf16(a):
    hi = a.astype(BF16)
    lo = (a - hi.astype(F32)).astype(BF16)
    return hi, lo


def _dot3(a, b):
    a_hi, a_lo = _split_bf16(a)
    b_hi, b_lo = _split_bf16(b)
    return _dot(a_hi, b_hi) + _dot(a_lo, b_hi) + _dot(a_hi, b_lo)


def _sigmoid(x):
    return 1.0 / (1.0 + jnp.exp(-x))


def _layer_norm(x, g, b):
    mu = jnp.mean(x, axis=-1, keepdims=True)
    xc = x - mu
    var = jnp.mean(xc * xc, axis=-1, keepdims=True)
    return xc * lax.rsqrt(var + LN_EPS) * g + b


def _mod_kernel(cond_ref, w_ref, b_ref, o_ref):
    c = cond_ref[...]
    o_ref[...] = _dot3(c * _sigmoid(c), w_ref[...]) + b_ref[...]


def _modulation(cond8, w_mod, b_mod):
    tn = 1536
    n = w_mod.shape[-1]
    return pl.pallas_call(
        _mod_kernel,
        grid=(DEPTH, n // tn),
        in_specs=[
            pl.BlockSpec((8, D_MODEL), lambda l, j: (0, 0)),
            pl.BlockSpec((None, D_MODEL, tn), lambda l, j: (l, 0, j)),
            pl.BlockSpec((None, 1, tn), lambda l, j: (l, 0, j)),
        ],
        out_specs=pl.BlockSpec((None, 8, tn), lambda l, j: (l, 0, j)),
        out_shape=jax.ShapeDtypeStruct((DEPTH, 8, n), F32),
        compiler_params=_cparams(("arbitrary", "arbitrary"), 40),
        name="modulation",
    )(cond8, w_mod, b_mod.reshape(DEPTH, 1, n))


def _head_rms(x, gmat, gain):
    hi, lo = _split_bf16(x * x)
    ms = _dot(hi, gmat) + _dot(lo, gmat)
    return x * lax.rsqrt(ms + RMS_EPS) * gain


def _rope(x, c, s):
    w = x.shape[1]
    reps = w // LANES
    if reps > 1:
        c = jnp.concatenate([c] * reps, axis=1)
        s = jnp.concatenate([s] * reps, axis=1)
    lane = lax.broadcasted_iota(jnp.int32, x.shape, 1)
    half = ROPE_AXIS // 2
    partner = jnp.where((lane % ROPE_AXIS) < half, pltpu.roll(x, w - half, 1), pltpu.roll(x, half, 1))
    return x * c + partner * s


def _moe_residual_norm(x1_ref, y_ref, g2_ref, lg_ref, lb_ref):
    tm = x1_ref.shape[0]
    y = jnp.concatenate([y_ref[pl.ds(j, tm, stride=Y_ROWS), :] for j in range(Y_ROWS)], axis=1)
    return _layer_norm(ALPHA * x1_ref[...] + g2_ref[...] * y, lg_ref[...], lb_ref[...])


def _proj_kernel(latent, fused, *refs):
    refs = list(refs)
    n_in = (5 if fused else 1) + 6 + (2 if latent else 0)
    ins, outs = refs[:n_in], refs[n_in:]
    x = _moe_residual_norm(*ins[:5]) if fused else ins[0][...]
    sc_ref, sh_ref, w_ref, gm_ref, qn_ref, kn_ref = ins[n_in - 6 - (2 if latent else 0):][:6]
    if latent:
        c_ref, s_ref = ins[-2:]
    if fused:
        outs.pop(0)[...] = x
    qa_ref, ka_ref, va_ref, u_ref, qc_ref, kc_ref, vc_ref, gt_ref = outs[:8]
    h = (x * sc_ref[...] + sh_ref[...]).astype(BF16)

    def mm(i):
        return _dot(h, w_ref[:, _OFF[i]:_OFF[i + 1]])

    qa_ref[...] = (mm(0) * QK_SCALE).astype(qa_ref.dtype)
    ka, va = mm(1), mm(2)
    u_ref[...] = mm(3).astype(u_ref.dtype)
    qc = _head_rms(mm(4), gm_ref[...], qn_ref[...])
    kc = _head_rms(mm(5), gm_ref[:GQ_KV_WIDTH, :GQ_KV_WIDTH], kn_ref[...])
    if latent:
        qc = _rope(qc, c_ref[...], s_ref[...])
        kc = _rope(kc, c_ref[...], s_ref[...])
    qc_ref[...] = (qc * QK_SCALE).astype(qc_ref.dtype)
    vc = mm(6)
    gt_ref[...] = mm(7).astype(gt_ref.dtype)
    for ref, val in ((ka_ref, ka), (va_ref, va), (kc_ref, kc), (vc_ref, vc)):
        ref[...] = val.astype(ref.dtype)
    for ref, val in zip(outs[8:], (ka, va, kc, vc)):
        ref[...] = val.T


def _mod_row_map(latent, tiles_per_seq):
    if latent:
        return lambda i: (1 + i // tiles_per_seq, 0, 0)
    return lambda i: (0, 0, 0)


def _proj(x, scale, shift, w_bf, gmat, qn, kn, rope, *, latent, seq_len, layer):
    fused = isinstance(x, tuple)
    t = (x[0] if fused else x).shape[0]
    tm = TM_PROJ
    row = lambda i: (i, 0)
    mod_map = _mod_row_map(latent, seq_len // tm)
    if fused:
        in_specs = [pl.BlockSpec((tm, D_MODEL), row), pl.BlockSpec((tm * Y_ROWS, LANES), row),
                    pl.BlockSpec((None, 1, D_MODEL), mod_map), _resident((1, D_MODEL)), _resident((1, D_MODEL))]
        args = list(x)
    else:
        in_specs = [pl.BlockSpec((tm, D_MODEL), row)]
        args = [x]
    in_specs += [
        pl.BlockSpec((None, 1, D_MODEL), mod_map),
        pl.BlockSpec((None, 1, D_MODEL), mod_map),
        pl.BlockSpec((None, D_MODEL, IN_COLS), lambda i: (layer, 0, 0), pipeline_mode=pl.Buffered(1)),
        _resident((GQ_Q_WIDTH, GQ_Q_WIDTH)),
        _resident((1, GQ_Q_WIDTH)),
        _resident((1, GQ_KV_WIDTH)),
    ]
    args += [scale, shift, w_bf, gmat, qn, kn]
    if latent:
        per = seq_len // tm
        in_specs += [pl.BlockSpec((tm, LANES), lambda i: (i % per, 0))] * 2
        args += list(rope)
    widths = IN_SIZES
    dtypes = (BF16,) * len(widths)
    if fused:
        widths = (D_MODEL,) + widths
        dtypes = (F32,) + dtypes
    out_specs = [pl.BlockSpec((tm, w), row) for w in widths]
    out_shape = [jax.ShapeDtypeStruct((t, w), dt) for w, dt in zip(widths, dtypes)]
    if not latent:
        per = seq_len // tm
        for w in (NA_WIDTH, NA_WIDTH, GQ_KV_WIDTH, GQ_KV_WIDTH):
            out_specs.append(pl.BlockSpec((None, w, tm), lambda i: (i // per, 0, i % per)))
            out_shape.append(jax.ShapeDtypeStruct((t // seq_len, w, seq_len), F32))
    return pl.pallas_call(
        functools.partial(_proj_kernel, latent, fused),
        grid=(t // tm,),
        in_specs=in_specs,
        out_specs=out_specs,
        out_shape=out_shape,
        compiler_params=_cparams(("arbitrary",), 48),
        name="proj_lat" if latent else "proj_ctx",
    )(*args)


def _pair(ref_or_val, j):
    return ref_or_val[:, j * LANES:(j + 1) * LANES]


def _keep_half(x, half, move_to=None):
    lane = lax.broadcasted_iota(jnp.int32, x.shape, 1)
    keep = (lane >= HEAD_DIM) if half else (lane < HEAD_DIM)
    if move_to is None or move_to == half:
        return jnp.where(keep, x, jnp.zeros_like(x))
    xf = jnp.where(keep, x.astype(F32), 0.0)
    return pltpu.roll(xf, HEAD_DIM, 1).astype(x.dtype)


def _attend_t(qp, pieces):
    scores = []
    for k, _, bias in pieces:
        s = _dot_nt(k, qp)
        scores.append(s if bias is None else s + bias)
    m = None
    for s in scores:
        mi = jnp.max(s, axis=0, keepdims=True)
        m = mi if m is None else jnp.maximum(m, mi)
    l = 0.0
    o = 0.0
    for s, (_, vt, _) in zip(scores, pieces):
        p = jnp.exp(s - m)
        l = l + jnp.sum(p, axis=0, keepdims=True)
        o = o + _dot(vt, p.astype(BF16))
    return o / l


def _select_rows(o_even, o_odd):
    return jnp.concatenate([o_even[:HEAD_DIM], o_odd[HEAD_DIM:]], axis=0)


def _ctx_attn_kernel(qa_ref, ka_ref, vat_ref, qc_ref, kc_ref, vct_ref, oa_ref, oc_ref):
    ka = ka_ref[...]
    vat = vat_ref[...].astype(BF16)
    outs = []
    for j in range(NA_HEADS // 2):
        k = _pair(ka, j)
        vt = vat[j * LANES:(j + 1) * LANES]
        q = _pair(qa_ref, j)
        outs.append(_select_rows(_attend_t(_keep_half(q, 0), [(k, vt, None)]),
                                 _attend_t(_keep_half(q, 1), [(k, vt, None)])))
    oa_ref[...] = jnp.concatenate(outs, axis=0).T.astype(oa_ref.dtype)
    kc = kc_ref[...]
    vct = vct_ref[...].astype(BF16)
    outs = []
    for h in range(GQ_HEADS):
        g = h // (GQ_HEADS // GQ_KV_HEADS)
        o = _attend_t(_keep_half(_pair(qc_ref, h // 2), h % 2, move_to=g), [(kc, vct, None)])
        outs.append(o[g * HEAD_DIM:(g + 1) * HEAD_DIM])
    oc_ref[...] = jnp.concatenate(outs, axis=0).T.astype(oc_ref.dtype)


def _ctx_attention(qa, ka, vat, qc, kc, vct, *, seq_len):
    t = qa.shape[0]
    row = lambda i: (i, 0)
    spec = lambda w: pl.BlockSpec((seq_len, w), row)
    tspec = lambda w: pl.BlockSpec((None, w, seq_len), lambda i: (i, 0, 0))
    return pl.pallas_call(
        _ctx_attn_kernel,
        grid=(t // seq_len,),
        in_specs=[spec(NA_WIDTH), spec(NA_WIDTH), tspec(NA_WIDTH), spec(GQ_Q_WIDTH), spec(GQ_KV_WIDTH),
                  tspec(GQ_KV_WIDTH)],
        out_specs=[spec(NA_WIDTH), spec(GQ_Q_WIDTH)],
        out_shape=[jax.ShapeDtypeStruct((t, NA_WIDTH), BF16), jax.ShapeDtypeStruct((t, GQ_Q_WIDTH), BF16)],
        compiler_params=_cparams(("arbitrary",), 32),
        name="attn_ctx",
    )(qa, ka, vat, qc, kc, vct)


def _na_key_row_start(blk, rows):
    return jnp.clip(blk * NA_QROWS - NA_KH // 2, 0, rows - NA_WIN_ROWS)


def _na_lat_kernel(rows, q_ref, k_ref, vt_ref, kc_ref, vct_ref, bias_ref, o_ref):
    base = pl.multiple_of(_na_key_row_start(pl.program_id(1), rows) * GRID_W, 2 * LANES)
    nk = NA_WIN_ROWS * GRID_W
    kwin = k_ref[pl.ds(base, nk), :]
    outs = []
    for j in range(NA_HEADS // 2):
        k = _pair(kwin, j)
        vt = vt_ref[j * LANES:(j + 1) * LANES, pl.ds(base, nk)]
        kc = _pair(kc_ref, j)
        vct = vct_ref[j * LANES:(j + 1) * LANES, :]
        q = _pair(q_ref, j)
        halves = [_attend_t(_keep_half(q, half), [(k, vt, bias_ref[2 * j + half]), (kc, vct, None)])
                  for half in range(2)]
        outs.append(_select_rows(*halves))
    o_ref[...] = jnp.concatenate(outs, axis=0).T.astype(o_ref.dtype)


def _na_latent(q, k, vt, k_ctx, vt_ctx, bias, *, n_seq, seq_len):
    rows = seq_len // GRID_W
    nblk = rows // NA_QROWS
    tq = NA_QROWS * GRID_W
    pattern = lambda b, i: (jnp.where(i == 0, 0, jnp.where(i == nblk - 1, 2, 1)), 0, 0, 0)
    return pl.pallas_call(
        functools.partial(_na_lat_kernel, rows),
        grid=(n_seq, nblk),
        in_specs=[
            pl.BlockSpec((tq, NA_WIDTH), lambda b, i: (b * nblk + i, 0)),
            pl.BlockSpec((seq_len, NA_WIDTH), lambda b, i: (b, 0)),
            pl.BlockSpec((None, NA_WIDTH, seq_len), lambda b, i: (b, 0, 0)),
            pl.BlockSpec((None,) + k_ctx.shape[1:], lambda b, i: (b, 0, 0)),
            pl.BlockSpec((None,) + vt_ctx.shape[1:], lambda b, i: (b, 0, 0)),
            pl.BlockSpec((None,) + bias.shape[1:], pattern),
        ],
        out_specs=pl.BlockSpec((tq, NA_WIDTH), lambda b, i: (b * nblk + i, 0)),
        out_shape=jax.ShapeDtypeStruct(q.shape, BF16),
        compiler_params=_cparams(("arbitrary", "arbitrary"), 52),
        name="attn_na_lat",
    )(q, k, vt, k_ctx, vt_ctx, bias)


def _gqa_lat_kernel(q_ref, k_ref, v_ref, o_ref):
    k = k_ref[...]
    for h in range(GQ_HEADS):
        g = h // (GQ_HEADS // GQ_KV_HEADS)
        s = _dot_nt(_keep_half(_pair(q_ref, h // 2), h % 2, move_to=g), k)
        p = jnp.exp(s - jnp.max(s, axis=-1, keepdims=True)).astype(BF16)
        o = _dot(p, v_ref[g])
        o_ref[:, h * HEAD_DIM:(h + 1) * HEAD_DIM] = (o[:, :HEAD_DIM] / o[:, HEAD_DIM:HEAD_DIM + 1]).astype(o_ref.dtype)


def _gqa_values_with_ones(v_all):
    b, s, _ = v_all.shape
    tail = jnp.concatenate([jnp.ones((b, s, 1), v_all.dtype), jnp.zeros((b, s, HEAD_DIM - 1), v_all.dtype)], axis=2)
    return jnp.stack([jnp.concatenate([v_all[:, :, g * HEAD_DIM:(g + 1) * HEAD_DIM], tail], axis=2)
                      for g in range(GQ_KV_HEADS)], axis=1)


def _gqa_latent(q, k_all, v_all, *, n_seq, seq_len):
    per = seq_len // TQ_GQA
    s_all = k_all.shape[1]
    return pl.pallas_call(
        _gqa_lat_kernel,
        grid=(n_seq, per),
        in_specs=[
            pl.BlockSpec((TQ_GQA, GQ_Q_WIDTH), lambda b, t: (b * per + t, 0)),
            pl.BlockSpec((None, s_all, GQ_KV_WIDTH), lambda b, t: (b, 0, 0)),
            pl.BlockSpec((None, GQ_KV_HEADS, s_all, 2 * HEAD_DIM), lambda b, t: (b, 0, 0, 0)),
        ],
        out_specs=pl.BlockSpec((TQ_GQA, GQ_Q_WIDTH), lambda b, t: (b * per + t, 0)),
        out_shape=jax.ShapeDtypeStruct(q.shape, BF16),
        compiler_params=_cparams(("arbitrary", "arbitrary"), 48),
        name="attn_gqa_lat",
    )(q, k_all, _gqa_values_with_ones(v_all))


def _conv_kernel(u_ref, up_ref, un_ref, w_ref, b_ref, g_ref, beta_ref, o_ref, shifted_ref):
    t = pl.program_id(1)
    nt = pl.num_programs(1)
    tl = u_ref.shape[0]

    def glu(v):
        return v[:, :CONV_CH].astype(F32) * _sigmoid(v[:, CONV_CH:].astype(F32))

    slab = jnp.concatenate([jnp.where(t > 0, glu(up_ref[...]), 0.0), glu(u_ref[...]),
                            jnp.where(t < nt - 1, glu(un_ref[...]), 0.0)], axis=0)
    first = HALO - CONV_K // 2
    span = shifted_ref.shape[1]
    assert (first + CONV_K - 1) // 8 * 8 + tl == span and span + 7 <= slab.shape[0]
    acc = jnp.zeros((tl, CONV_CH), F32)
    for shift in range(8):
        shifted_ref[shift % 2] = slab[shift:shift + span]
        for k in range(CONV_K):
            if (first + k) % 8 == shift:
                base = first + k - shift
                acc = acc + shifted_ref[shift % 2, base:base + tl, :] * w_ref[k:k + 1, :]
    y = _layer_norm(acc + b_ref[...], g_ref[...], beta_ref[...])
    o_ref[...] = (y * _sigmoid(y)).astype(o_ref.dtype)


def _conformer_conv(u, conv_w, conv_b, ln_g, ln_b, *, n_seq, seq_len):
    tl = TL_CONV
    nt = seq_len // tl
    hb = tl // HALO
    last_hb = seq_len // HALO - 1
    u3 = u.reshape(n_seq, seq_len, 2 * CONV_CH)
    vec = lambda a: a.reshape(1, CONV_CH)
    out = pl.pallas_call(
        _conv_kernel,
        grid=(n_seq, nt),
        in_specs=[
            pl.BlockSpec((None, tl, 2 * CONV_CH), lambda s, t: (s, t, 0)),
            pl.BlockSpec((None, HALO, 2 * CONV_CH), lambda s, t: (s, jnp.maximum(t * hb - 1, 0), 0)),
            pl.BlockSpec((None, HALO, 2 * CONV_CH), lambda s, t: (s, jnp.minimum((t + 1) * hb, last_hb), 0)),
            _resident((CONV_K, CONV_CH)),
            _resident((1, CONV_CH)),
            _resident((1, CONV_CH)),
            _resident((1, CONV_CH)),
        ],
        out_specs=pl.BlockSpec((None, tl, CONV_CH), lambda s, t: (s, t, 0)),
        out_shape=jax.ShapeDtypeStruct((n_seq, seq_len, CONV_CH), BF16),
        scratch_shapes=[pltpu.VMEM((2, (HALO + CONV_K // 2) // 8 * 8 + tl, CONV_CH), F32)],
        compiler_params=_cparams(("arbitrary", "arbitrary"), 32),
        name="conformer_conv",
    )(u3, u3, u3, conv_w, vec(conv_b), vec(ln_g), vec(ln_b))
    return out.reshape(n_seq * seq_len, CONV_CH)


def _router_gates(logits):
    lane = lax.broadcasted_iota(jnp.int32, logits.shape, 1)
    m = jnp.max(logits, axis=-1, keepdims=True)
    e = jnp.exp(logits - m)
    probs = e / jnp.sum(e, axis=-1, keepdims=True)
    p1 = jnp.max(probs, axis=-1, keepdims=True)
    i1 = jnp.min(jnp.where(probs == p1, lane, LANES), axis=-1, keepdims=True)
    in_group = (lane // EXPERTS_PER_GROUP) == (i1 // EXPERTS_PER_GROUP)
    cand = jnp.where(in_group, jnp.where(lane == i1, -1.0, probs), -1.0)
    p2 = jnp.max(cand, axis=-1, keepdims=True)
    i2 = jnp.min(jnp.where(cand == p2, lane, LANES), axis=-1, keepdims=True)
    den = p1 + p2
    gates = jnp.where(lane == i1, p1 / den, 0.0) + jnp.where(lane == i2, p2 / den, 0.0)
    group = (i1 // EXPERTS_PER_GROUP).astype(F32)
    return jnp.where(lane == GROUP_LANE, group, gates)


def _post_kernel(oa_ref, ob_ref, oc_ref, gt_ref, x_ref, g1_ref, lg_ref, lb_ref, sc2_ref, sh2_ref,
                 wa_ref, wb_ref, wc_ref, wo_ref, wr_ref, br_ref, x1_ref, hg_ref, gate_ref):
    sg = _sigmoid(gt_ref[...].astype(F32))
    merged = (sg[:, :D_MODEL] * _dot(oa_ref[...], wa_ref[...])
              + sg[:, D_MODEL:2 * D_MODEL] * _dot(ob_ref[...], wb_ref[...])
              + sg[:, 2 * D_MODEL:] * _dot(oc_ref[...], wc_ref[...]))
    y = _dot(merged.astype(BF16), wo_ref[...])
    x1 = _layer_norm(ALPHA * x_ref[...] + g1_ref[...] * y, lg_ref[...], lb_ref[...])
    x1_ref[...] = x1
    h2 = x1 * sc2_ref[...] + sh2_ref[...]
    tm = x1.shape[0]
    gates = _router_gates(_dot3(h2, wr_ref[...]) + br_ref[...])
    gate_ref[...] = gates
    block = jnp.concatenate([h2, gates, jnp.zeros((tm, (HG_ROWS - HG_USED) * LANES), F32)], axis=1)
    hg_ref[...] = pltpu.einshape("t(jl)->(tj)l", block, j=HG_ROWS)


def _post(oa, ob, oc, gates, x, g1, ln_g, ln_b, scale2, shift2, wa, wb, wc, wo, wr, br, *, latent, seq_len):
    t = x.shape[0]
    tm = TM_POST
    row = lambda i: (i, 0)
    mod_map = _mod_row_map(latent, seq_len // tm)
    mod = pl.BlockSpec((None, 1, D_MODEL), mod_map)
    return pl.pallas_call(
        _post_kernel,
        grid=(t // tm,),
        in_specs=[
            pl.BlockSpec((tm, NA_WIDTH), row), pl.BlockSpec((tm, CONV_CH), row), pl.BlockSpec((tm, GQ_Q_WIDTH), row),
            pl.BlockSpec((tm, 3 * D_MODEL), row), pl.BlockSpec((tm, D_MODEL), row),
            mod, _resident((1, D_MODEL)), _resident((1, D_MODEL)), mod, mod,
            _resident(wa.shape), _resident(wb.shape), _resident(wc.shape), _resident(wo.shape),
            _resident(wr.shape), _resident(br.shape),
        ],
        out_specs=[pl.BlockSpec((tm, D_MODEL), row), pl.BlockSpec((tm * HG_ROWS, LANES), row),
                   pl.BlockSpec((tm, LANES), row)],
        out_shape=[jax.ShapeDtypeStruct((t, D_MODEL), F32), jax.ShapeDtypeStruct((t * HG_ROWS, LANES), F32),
                   jax.ShapeDtypeStruct((t, LANES), F32)],
        compiler_params=_cparams(("arbitrary",), 40),
        name="post_lat" if latent else "post_ctx",
    )(oa, ob, oc, gates, x, g1, ln_g, ln_b, scale2, shift2, wa, wb, wc, wo, wr, br)


def _route(grp, tm):
    n = grp.shape[0]
    n_tiles = n // tm + N_GROUPS
    groups = jnp.arange(N_GROUPS, dtype=jnp.int32)
    onehot = (grp[:, None] == groups[None, :]).astype(jnp.int32)
    csum = jnp.cumsum(onehot, axis=0)
    rank = jnp.sum((csum - onehot) * onehot, axis=1)
    counts = csum[-1]
    tiles = (counts + tm - 1) // tm
    tile_end = jnp.cumsum(tiles)
    tile_start = tile_end - tiles
    slot = jnp.sum(onehot * tile_start[None, :], axis=1) * tm + rank
    tile_ids = jnp.arange(n_tiles, dtype=jnp.int32)
    tile_group = jnp.minimum(jnp.sum((tile_ids[:, None] >= tile_end[None, :]).astype(jnp.int32), axis=1), N_GROUPS - 1)
    src = jnp.zeros((n_tiles * tm,), jnp.int32).at[slot].set(jnp.arange(n, dtype=jnp.int32))
    r = jnp.arange(n_tiles * tm, dtype=jnp.int32)
    slot_tile = r // tm
    in_group = (jnp.repeat(tile_group, tm)[:, None] == groups[None, :]).astype(jnp.int32)
    slot_rank = r - jnp.sum(in_group * tile_start[None, :], axis=1) * tm
    valid = (slot_tile < tile_end[-1]) & (slot_rank < jnp.sum(in_group * counts[None, :], axis=1))
    dst = jnp.where(valid, src, n + (slot_tile % 2) * tm + r % tm)
    return tile_group.astype(jnp.int32), src, dst, tile_end[-1:].astype(jnp.int32)


def _moe_kernel(tg_ref, src_ref, dst_ref, nused_ref, h_hbm, wgu_ref, wd_ref, y_hbm, hbuf, ybuf, gsem, ssem):
    t = pl.program_id(0)
    n_used = nused_ref[0]
    slot = t % 2
    tm = ybuf.shape[1] // Y_ROWS

    def gather_row(tile, s, r):
        tok = src_ref[tile * tm + r]
        return pltpu.make_async_copy(h_hbm.at[pl.ds(pl.multiple_of(tok * HG_ROWS, HG_ROWS), HG_USED), :],
                                     hbuf.at[s, pl.ds(pl.multiple_of(r * HG_ROWS, HG_ROWS), HG_USED), :], gsem.at[s])

    def scatter_row(tile, s, r):
        row = dst_ref[tile * tm + r]
        return pltpu.make_async_copy(ybuf.at[s, pl.ds(pl.multiple_of(r * Y_ROWS, Y_ROWS), Y_ROWS), :],
                                     y_hbm.at[pl.ds(pl.multiple_of(row * Y_ROWS, Y_ROWS), Y_ROWS), :], ssem.at[s])

    def for_rows(fn):
        def body(r, carry):
            fn(r)
            return carry
        lax.fori_loop(0, tm, body, 0, unroll=8)

    @pl.when(t == 0)
    def _():
        for_rows(lambda r: gather_row(0, 0, r).start())

    @pl.when(t + 1 < n_used)
    def _():
        for_rows(lambda r: gather_row(t + 1, 1 - slot, r).start())

    @pl.when(t < n_used)
    def _():
        for_rows(lambda r: gather_row(t, slot, r).wait())

        @pl.when(t >= 2)
        def _():
            for_rows(lambda r: scatter_row(t - 2, slot, r).wait())

        first = tg_ref[t] * EXPERTS_PER_GROUP
        token_rows = lambda ref, j, per: ref[slot, pl.ds(j, tm, stride=per), :]
        h = jnp.concatenate([token_rows(hbuf, j, HG_ROWS) for j in range(Y_ROWS)], axis=1).astype(BF16)
        gates = token_rows(hbuf, Y_ROWS, HG_ROWS)
        lane = lax.broadcasted_iota(jnp.int32, gates.shape, 1)
        acc = jnp.zeros((tm, D_MODEL), F32)
        for e in range(EXPERTS_PER_GROUP):
            gu = _dot(h, wgu_ref[e])
            a = gu[:, :D_EXPERT]
            ge = jnp.sum(jnp.where(lane == first + e, gates, 0.0), axis=-1, keepdims=True)
            hid = a * _sigmoid(a) * gu[:, D_EXPERT:] * ge
            acc = acc + _dot(hid.astype(BF16), wd_ref[e])
        for j in range(Y_ROWS):
            ybuf[slot, pl.ds(j, tm, stride=Y_ROWS), :] = acc[:, j * LANES:(j + 1) * LANES]
        for_rows(lambda r: scatter_row(t, slot, r).start())

        @pl.when(t == n_used - 1)
        def _():
            for_rows(lambda r: scatter_row(t, slot, r).wait())

            @pl.when(t >= 1)
            def _():
                for_rows(lambda r: scatter_row(t - 1, 1 - slot, r).wait())

            n_rows = y_hbm.shape[0] - 2 * tm * Y_ROWS
            ybuf[0] = jnp.zeros(ybuf.shape[1:], F32)
            for half in range(2):
                fill = pltpu.make_async_copy(
                    ybuf.at[0], y_hbm.at[pl.ds(n_rows + half * tm * Y_ROWS, tm * Y_ROWS), :], ssem.at[half])
                fill.start()
                fill.wait()


def _moe_experts(hg, route, wgu, wd, layer):
    n = hg.shape[0] // HG_ROWS
    tm = TM_MOE
    tile_group, src, dst, n_used = route
    n_tiles = tile_group.shape[0]
    grid_spec = pltpu.PrefetchScalarGridSpec(
        num_scalar_prefetch=4,
        grid=(n_tiles,),
        in_specs=[
            pl.BlockSpec(memory_space=pl.ANY),
            pl.BlockSpec((None, EXPERTS_PER_GROUP, D_MODEL, 2 * D_EXPERT), lambda t, tg, s, d, nu: (layer, tg[t], 0, 0)),
            pl.BlockSpec((None, EXPERTS_PER_GROUP, D_EXPERT, D_MODEL), lambda t, tg, s, d, nu: (layer, tg[t], 0, 0)),
        ],
        out_specs=pl.BlockSpec(memory_space=pl.ANY),
        scratch_shapes=[
            pltpu.VMEM((2, tm * HG_ROWS, LANES), F32),
            pltpu.VMEM((2, tm * Y_ROWS, LANES), F32),
            pltpu.SemaphoreType.DMA((2,)),
            pltpu.SemaphoreType.DMA((2,)),
        ],
    )
    return pl.pallas_call(
        _moe_kernel,
        grid_spec=grid_spec,
        out_shape=jax.ShapeDtypeStruct(((n + 2 * tm) * Y_ROWS, LANES), F32),
        compiler_params=_cparams(("arbitrary",), 58),
        name="moe_experts",
    )(tile_group, src, dst, n_used, hg, wgu, wd)


def _moe_out_kernel(x1_ref, y_ref, g2_ref, lg_ref, lb_ref, o_ref):
    o_ref[...] = _moe_residual_norm(x1_ref, y_ref, g2_ref, lg_ref, lb_ref)


def _moe_out(x1, y, g2, ln_g, ln_b, *, latent, seq_len):
    t = x1.shape[0]
    tm = TM_POST
    row = lambda i: (i, 0)
    return pl.pallas_call(
        _moe_out_kernel,
        grid=(t // tm,),
        in_specs=[pl.BlockSpec((tm, D_MODEL), row), pl.BlockSpec((tm * Y_ROWS, LANES), row),
                  pl.BlockSpec((None, 1, D_MODEL), _mod_row_map(latent, seq_len // tm)),
                  _resident((1, D_MODEL)), _resident((1, D_MODEL))],
        out_specs=pl.BlockSpec((tm, D_MODEL), row),
        out_shape=jax.ShapeDtypeStruct((t, D_MODEL), F32),
        compiler_params=_cparams(("arbitrary",), 32),
        name="moe_out",
    )(x1, y, g2, ln_g, ln_b)


def _moe(hg, gates, wgu, wd, layer):
    return _moe_experts(hg, _route(gates[:, GROUP_LANE].astype(jnp.int32), TM_MOE), wgu, wd, layer)


def _head_mean_matrix():
    i = jnp.arange(GQ_Q_WIDTH) // HEAD_DIM
    return jnp.where(i[:, None] == i[None, :], 1.0 / HEAD_DIM, 0.0).astype(BF16)


def _rope_tables(n_tokens):
    t = jnp.arange(n_tokens, dtype=jnp.int32)
    inv_freq = ROPE_THETA ** (-jnp.arange(0, ROPE_AXIS, 2, dtype=F32) / ROPE_AXIS)
    ar = (t // GRID_W).astype(F32)[:, None] * inv_freq
    ac = (t % GRID_W).astype(F32)[:, None] * inv_freq
    c = jnp.concatenate([jnp.cos(ar), jnp.cos(ar), jnp.cos(ac), jnp.cos(ac)], axis=1)
    s = jnp.concatenate([-jnp.sin(ar), jnp.sin(ar), -jnp.sin(ac), jnp.sin(ac)], axis=1)
    return jnp.tile(c, (1, LANES // HEAD_DIM)), jnp.tile(s, (1, LANES // HEAD_DIM))


def _na_bias_kernel(rows, t_ref, o_ref):
    masked = jnp.full((GRID_W, GRID_W), NEG_INF, F32)
    for pat, first_qrow in enumerate((0, NA_KH // 2, rows - NA_QROWS)):
        krow0 = min(max(first_qrow - NA_KH // 2, 0), rows - NA_WIN_ROWS)
        for kj in range(NA_WIN_ROWS):
            blocks = []
            for qi in range(NA_QROWS):
                qrow, krow = first_qrow + qi, krow0 + kj
                win0 = min(max(qrow - NA_KH // 2, 0), rows - NA_KH)
                in_window = win0 <= krow < win0 + NA_KH
                blocks.append(t_ref[krow - qrow + NA_KH - 1] if in_window else masked)
            o_ref[pat, kj * GRID_W:(kj + 1) * GRID_W, :] = jnp.concatenate(blocks, axis=1)


def _na_bias_table(rpb, rows):
    col = np.arange(GRID_W)
    col_idx = np.clip(col[:, None] - col[None, :], -(NA_KW - 1), NA_KW - 1) + NA_KW - 1
    onehot = (col_idx[None] == np.arange(2 * NA_KW - 1)[:, None, None]).astype(np.float32)
    col_start = np.clip(col - NA_KW // 2, 0, GRID_W - NA_KW)
    col_ok = (col[:, None] >= col_start[None, :]) & (col[:, None] < col_start[None, :] + NA_KW)
    t = jnp.einsum("lhic,ckq->lhikq", rpb, onehot, precision=lax.Precision.HIGHEST)
    t = jnp.where(col_ok, t, NEG_INF)
    depth, heads, n_rel = t.shape[:3]
    nk, nq = NA_WIN_ROWS * GRID_W, NA_QROWS * GRID_W
    return pl.pallas_call(
        functools.partial(_na_bias_kernel, rows),
        grid=(depth, heads),
        in_specs=[pl.BlockSpec((None, None, n_rel, GRID_W, GRID_W), lambda l, h: (l, h, 0, 0, 0))],
        out_specs=pl.BlockSpec((None, 3, None, nk, nq), lambda l, h: (l, 0, h, 0, 0)),
        out_shape=jax.ShapeDtypeStruct((depth, 3, heads, nk, nq), F32),
        compiler_params=_cparams(("arbitrary", "arbitrary"), 32),
        name="na_bias_table",
    )(t)


def _stream_layer(x, mods, l, wts, *, latent, n_seq, seq_len, cache=None, rope=None):
    sh1, sc1, g1, sh2, sc2, g2 = mods
    pieces = _proj(x, 1.0 + sc1, sh1, wts["w_in"], wts["gmat"], wts["qn"][l], wts["kn"][l], rope,
                   latent=latent, seq_len=seq_len, layer=l)
    if isinstance(x, tuple):
        x, pieces = pieces[0], pieces[1:]
    qa, ka, va, u, qc, kc, vc, gates = pieces[:8]
    new_cache_t = pieces[8:]
    if latent:
        nat_k, nat_v, gqa_k, gqa_v = cache
        swap = lambda a: jnp.swapaxes(a, 1, 2)
        out_a = _na_latent(qa, ka, swap(va.reshape(n_seq, seq_len, NA_WIDTH)), nat_k, swap(nat_v),
                           wts["na_bias"][l], n_seq=n_seq, seq_len=seq_len)
        k_all = jnp.concatenate([gqa_k, kc.reshape(n_seq, seq_len, GQ_KV_WIDTH)], axis=1)
        v_all = jnp.concatenate([gqa_v, vc.reshape(n_seq, seq_len, GQ_KV_WIDTH)], axis=1)
        out_c = _gqa_latent(qc, k_all, v_all, n_seq=n_seq, seq_len=seq_len)
    else:
        out_a, out_c = _ctx_attention(qa, ka, new_cache_t[1], qc, kc, new_cache_t[3], seq_len=seq_len)
    out_b = _conformer_conv(u, wts["conv_w"][l], wts["conv_b"][l], wts["conv_ln_g"][l], wts["conv_ln_b"][l],
                            n_seq=n_seq, seq_len=seq_len)
    x1, hg, moe_gates = _post(out_a, out_b, out_c, gates, x, g1, wts["ln1_g"][l], wts["ln1_b"][l], 1.0 + sc2, sh2,
                         wts["w_br_a"][l], wts["w_br_b"][l], wts["w_br_c"][l], wts["w_out"][l],
                         wts["w_router"], wts["b_router"], latent=latent, seq_len=seq_len)
    y = _moe(hg, moe_gates, wts["w_gate_up"], wts["w_down"], l)
    return (x1, y, g2, wts["ln2_g"][l], wts["ln2_b"][l]), new_cache_t


def kernel(x_prompt, x_sample, cache_nat_k, cache_nat_v, cache_gqa_k, cache_gqa_v, c, c_ctx, w_mod, b_mod, w_in, nat_rpb, conv_w, conv_b, conv_ln_g, conv_ln_b, q_norm_g, k_norm_g, w_br_a, w_br_b, w_br_c, w_out, ln1_g, ln1_b, ln2_g, ln2_b, w_router, b_router, w_gate_up, w_down):
    batch, seq, _ = x_prompt.shape
    dec_batch, dec_seq, _ = x_sample.shape
    past = cache_nat_k.shape[2]
    assert dec_batch == 2 and (batch * seq) % TM_MOE == 0 and dec_seq % TM_MOE == 0 and seq % TL_CONV == 0
    assert TM_MOE % TM_PROJ == 0 and TM_MOE % TM_POST == 0

    cond8 = jnp.zeros((8, D_MODEL), F32).at[0].set(c_ctx).at[1:1 + dec_batch].set(c)
    mods_all = _modulation(cond8, w_mod, b_mod)

    pad = LANES - N_EXPERTS
    vecd = lambda a: a.reshape(DEPTH, 1, -1)
    per_layer_bf16 = lambda w: [w[l].astype(BF16) for l in range(DEPTH)]
    rows = dec_seq // GRID_W
    wts = dict(
        w_in=w_in.astype(BF16),
        gmat=_head_mean_matrix(),
        qn=jnp.tile(q_norm_g, (1, GQ_HEADS)).reshape(DEPTH, 1, GQ_Q_WIDTH),
        kn=jnp.tile(k_norm_g, (1, GQ_KV_HEADS)).reshape(DEPTH, 1, GQ_KV_WIDTH),
        na_bias=[_na_bias_table(nat_rpb[l:l + 1], rows).reshape(3, NA_HEADS, NA_WIN_ROWS * GRID_W, NA_QROWS * GRID_W)
                 for l in range(DEPTH)],
        conv_w=conv_w, conv_b=conv_b, conv_ln_g=conv_ln_g, conv_ln_b=conv_ln_b,
        ln1_g=vecd(ln1_g), ln1_b=vecd(ln1_b), ln2_g=vecd(ln2_g), ln2_b=vecd(ln2_b),
        w_br_a=per_layer_bf16(w_br_a), w_br_b=per_layer_bf16(w_br_b), w_br_c=per_layer_bf16(w_br_c),
        w_out=per_layer_bf16(w_out),
        w_router=jnp.pad(w_router, ((0, 0), (0, pad))),
        b_router=jnp.pad(b_router, (0, pad), constant_values=NEG_INF).reshape(1, LANES),
        w_gate_up=w_gate_up.astype(BF16), w_down=w_down.astype(BF16),
    )
    rope = _rope_tables(dec_seq)

    def layer_mods(l):
        m = mods_all[l, :1 + dec_batch].reshape(1 + dec_batch, 6, 1, D_MODEL)
        return [m[:, i] for i in range(6)]

    xp = x_prompt.reshape(batch * seq, D_MODEL)
    xs = x_sample.reshape(dec_batch * dec_seq, D_MODEL)
    new_cache = []
    for l in range(DEPTH):
        mods = layer_mods(l)
        xp, ctx_kv = _stream_layer(xp, mods, l, wts, latent=False, n_seq=batch, seq_len=seq)
        new_cache.append(ctx_kv)
        cache = (cache_nat_k[:, l].reshape(dec_batch, past, NA_WIDTH).astype(BF16),
                 cache_nat_v[:, l].reshape(dec_batch, past, NA_WIDTH).astype(BF16),
                 cache_gqa_k[:, l].reshape(dec_batch, past, GQ_KV_WIDTH).astype(BF16),
                 cache_gqa_v[:, l].reshape(dec_batch, past, GQ_KV_WIDTH).astype(BF16))
        xs, _ = _stream_layer(xs, mods, l, wts, latent=True, n_seq=dec_batch, seq_len=dec_seq, cache=cache, rope=rope)
    xp = _moe_out(*xp, latent=False, seq_len=seq)
    xs = _moe_out(*xs, latent=True, seq_len=dec_seq)

    def stack(i, heads):
        per_layer = jnp.stack([kv[i].reshape(batch, heads, HEAD_DIM, seq) for kv in new_cache], axis=1)
        return per_layer.transpose(0, 1, 4, 2, 3)

    return (xp.reshape(batch, seq, D_MODEL), xs.reshape(dec_batch, dec_seq, D_MODEL),
            stack(0, NA_HEADS), stack(1, NA_HEADS), stack(2, GQ_KV_HEADS), stack(3, GQ_KV_HEADS))
```

```python
import functools

import numpy as np
import jax
import jax.numpy as jnp
from jax import lax
from jax.experimental import pallas as pl
from jax.experimental.pallas import tpu as pltpu

F32 = jnp.float32
BF16 = jnp.bfloat16

D_MODEL = 1024
DEPTH = 2
GRID_W = 64
HEAD_DIM = 64
NA_HEADS = 8
NA_WIDTH = NA_HEADS * HEAD_DIM
NA_KH = 8
NA_KW = 16
NA_QROWS = 4
NA_WIN_ROWS = NA_QROWS + NA_KH
CONV_CH = 512
CONV_K = 31
GQ_HEADS = 8
GQ_KV_HEADS = 2
GQ_Q_WIDTH = GQ_HEADS * HEAD_DIM
GQ_KV_WIDTH = GQ_KV_HEADS * HEAD_DIM
ROPE_THETA = 10000.0
N_EXPERTS = 16
EXPERTS_PER_GROUP = 4
N_GROUPS = N_EXPERTS // EXPERTS_PER_GROUP
GROUP_LANE = N_EXPERTS
D_EXPERT = 512
ALPHA = (2 * DEPTH) ** 0.25
LN_EPS = 1e-6
RMS_EPS = 1e-6
NEG_INF = -1e30
QK_SCALE = HEAD_DIM ** -0.5

LANES = 128
Y_ROWS = D_MODEL // LANES
HG_ROWS = 2 * Y_ROWS
HG_USED = Y_ROWS + 1

IN_SIZES = (NA_WIDTH, NA_WIDTH, NA_WIDTH, 2 * CONV_CH, GQ_Q_WIDTH, GQ_KV_WIDTH, GQ_KV_WIDTH, 3 * D_MODEL)
_OFF = tuple(int(o) for o in np.cumsum((0,) + IN_SIZES))
IN_COLS = _OFF[-1]
ROPE_AXIS = HEAD_DIM // 2
V7X_VMEM_BYTES = 64 * 2 ** 20
HALO = 16

TM_PROJ = 256
TM_POST = 256
TM_MOE = 512
TL_CONV = 256
TQ_GQA = 256


def _cparams(sem, vmem_mib):
    assert vmem_mib * 2 ** 20 < V7X_VMEM_BYTES
    return pltpu.CompilerParams(dimension_semantics=sem, vmem_limit_bytes=vmem_mib * 2 ** 20)


def _resident(shape):
    nd = len(shape)
    return pl.BlockSpec(shape, lambda *_: (0,) * nd, pipeline_mode=pl.Buffered(1))


def _dot(a, b):
    return jnp.dot(a, b, preferred_element_type=F32)


def _dot_nt(a, b):
    return lax.dot_general(a, b, (((1,), (1,)), ((), ())), preferred_element_type=F32)


def _split_bf16(a):
    hi = a.astype(BF16)
    lo = (a - hi.astype(F32)).astype(BF16)
    return hi, lo


def _dot3(a, b):
    a_hi, a_lo = _split_bf16(a)
    b_hi, b_lo = _split_bf16(b)
    return _dot(a_hi, b_hi) + _dot(a_lo, b_hi) + _dot(a_hi, b_lo)


def _sigmoid(x):
    return 1.0 / (1.0 + jnp.exp(-x))


def _layer_norm(x, g, b):
    mu = jnp.mean(x, axis=-1, keepdims=True)
    xc = x - mu
    var = jnp.mean(xc * xc, axis=-1, keepdims=True)
    return xc * lax.rsqrt(var + LN_EPS) * g + b


def _mod_kernel(cond_ref, w_ref, b_ref, o_ref):
    c = cond_ref[...]
    o_ref[...] = _dot3(c * _sigmoid(c), w_ref[...]) + b_ref[...]


def _modulation(cond8, w_mod, b_mod):
    tn = 1536
    n = w_mod.shape[-1]
    return pl.pallas_call(
        _mod_kernel,
        grid=(DEPTH, n // tn),
        in_specs=[
            pl.BlockSpec((8, D_MODEL), lambda l, j: (0, 0)),
            pl.BlockSpec((None, D_MODEL, tn), lambda l, j: (l, 0, j)),
            pl.BlockSpec((None, 1, tn), lambda l, j: (l, 0, j)),
        ],
        out_specs=pl.BlockSpec((None, 8, tn), lambda l, j: (l, 0, j)),
        out_shape=jax.ShapeDtypeStruct((DEPTH, 8, n), F32),
        compiler_params=_cparams(("arbitrary", "arbitrary"), 40),
        name="modulation",
    )(cond8, w_mod, b_mod.reshape(DEPTH, 1, n))


def _head_rms(x, gmat, gain):
    hi, lo = _split_bf16(x * x)
    ms = _dot(hi, gmat) + _dot(lo, gmat)
    return x * lax.rsqrt(ms + RMS_EPS) * gain


def _rope(x, c, s):
    w = x.shape[1]
    reps = w // LANES
    if reps > 1:
        c = jnp.concatenate([c] * reps, axis=1)
        s = jnp.concatenate([s] * reps, axis=1)
    lane = lax.broadcasted_iota(jnp.int32, x.shape, 1)
    half = ROPE_AXIS // 2
    partner = jnp.where((lane % ROPE_AXIS) < half, pltpu.roll(x, w - half, 1), pltpu.roll(x, half, 1))
    return x * c + partner * s


def _moe_residual_norm(x1_ref, y_ref, g2_ref, lg_ref, lb_ref):
    tm = x1_ref.shape[0]
    y = jnp.concatenate([y_ref[pl.ds(j, tm, stride=Y_ROWS), :] for j in range(Y_ROWS)], axis=1)
    return _layer_norm(ALPHA * x1_ref[...] + g2_ref[...] * y, lg_ref[...], lb_ref[...])


def _proj_kernel(latent, fused, *refs):
    refs = list(refs)
    n_in = (5 if fused else 1) + 6 + (2 if latent else 0)
    ins, outs = refs[:n_in], refs[n_in:]
    x = _moe_residual_norm(*ins[:5]) if fused else ins[0][...]
    sc_ref, sh_ref, w_ref, gm_ref, qn_ref, kn_ref = ins[n_in - 6 - (2 if latent else 0):][:6]
    if latent:
        c_ref, s_ref = ins[-2:]
    if fused:
        outs.pop(0)[...] = x
    qa_ref, ka_ref, va_ref, u_ref, qc_ref, kc_ref, vc_ref, gt_ref = outs[:8]
    h = (x * sc_ref[...] + sh_ref[...]).astype(BF16)

    def mm(i):
        return _dot(h, w_ref[:, _OFF[i]:_OFF[i + 1]])

    qa_ref[...] = (mm(0) * QK_SCALE).astype(qa_ref.dtype)
    ka, va = mm(1), mm(2)
    u_ref[...] = mm(3).astype(u_ref.dtype)
    qc = _head_rms(mm(4), gm_ref[...], qn_ref[...])
    kc = _head_rms(mm(5), gm_ref[:GQ_KV_WIDTH, :GQ_KV_WIDTH], kn_ref[...])
    if latent:
        qc = _rope(qc, c_ref[...], s_ref[...])
        kc = _rope(kc, c_ref[...], s_ref[...])
    qc_ref[...] = (qc * QK_SCALE).astype(qc_ref.dtype)
    vc = mm(6)
    gt_ref[...] = mm(7).astype(gt_ref.dtype)
    for ref, val in ((ka_ref, ka), (va_ref, va), (kc_ref, kc), (vc_ref, vc)):
        ref[...] = val.astype(ref.dtype)
    for ref, val in zip(outs[8:], (ka, va, kc, vc)):
        ref[...] = val.T


def _mod_row_map(latent, tiles_per_seq):
    if latent:
        return lambda i: (1 + i // tiles_per_seq, 0, 0)
    return lambda i: (0, 0, 0)


def _proj(x, scale, shift, w_bf, gmat, qn, kn, rope, *, latent, seq_len, layer):
    fused = isinstance(x, tuple)
    t = (x[0] if fused else x).shape[0]
    tm = TM_PROJ
    row = lambda i: (i, 0)
    mod_map = _mod_row_map(latent, seq_len // tm)
    if fused:
        in_specs = [pl.BlockSpec((tm, D_MODEL), row), pl.BlockSpec((tm * Y_ROWS, LANES), row),
                    pl.BlockSpec((None, 1, D_MODEL), mod_map), _resident((1, D_MODEL)), _resident((1, D_MODEL))]
        args = list(x)
    else:
        in_specs = [pl.BlockSpec((tm, D_MODEL), row)]
        args = [x]
    in_specs += [
        pl.BlockSpec((None, 1, D_MODEL), mod_map),
        pl.BlockSpec((None, 1, D_MODEL), mod_map),
        pl.BlockSpec((None, D_MODEL, IN_COLS), lambda i: (layer, 0, 0), pipeline_mode=pl.Buffered(1)),
        _resident((GQ_Q_WIDTH, GQ_Q_WIDTH)),
        _resident((1, GQ_Q_WIDTH)),
        _resident((1, GQ_KV_WIDTH)),
    ]
    args += [scale, shift, w_bf, gmat, qn, kn]
    if latent:
        per = seq_len // tm
        in_specs += [pl.BlockSpec((tm, LANES), lambda i: (i % per, 0))] * 2
        args += list(rope)
    widths = IN_SIZES
    dtypes = (BF16,) * len(widths)
    if fused:
        widths = (D_MODEL,) + widths
        dtypes = (F32,) + dtypes
    out_specs = [pl.BlockSpec((tm, w), row) for w in widths]
    out_shape = [jax.ShapeDtypeStruct((t, w), dt) for w, dt in zip(widths, dtypes)]
    if not latent:
        per = seq_len // tm
        for w in (NA_WIDTH, NA_WIDTH, GQ_KV_WIDTH, GQ_KV_WIDTH):
            out_specs.append(pl.BlockSpec((None, w, tm), lambda i: (i // per, 0, i % per)))
            out_shape.append(jax.ShapeDtypeStruct((t // seq_len, w, seq_len), F32))
    return pl.pallas_call(
        functools.partial(_proj_kernel, latent, fused),
        grid=(t // tm,),
        in_specs=in_specs,
        out_specs=out_specs,
        out_shape=out_shape,
        compiler_params=_cparams(("arbitrary",), 48),
        name="proj_lat" if latent else "proj_ctx",
    )(*args)


def _pair(ref_or_val, j):
    return ref_or_val[:, j * LANES:(j + 1) * LANES]


def _keep_half(x, half, move_to=None):
    lane = lax.broadcasted_iota(jnp.int32, x.shape, 1)
    keep = (lane >= HEAD_DIM) if half else (lane < HEAD_DIM)
    if move_to is None or move_to == half:
        return jnp.where(keep, x, jnp.zeros_like(x))
    xf = jnp.where(keep, x.astype(F32), 0.0)
    return pltpu.roll(xf, HEAD_DIM, 1).astype(x.dtype)


def _attend_t(qp, pieces):
    scores = []
    for k, _, bias in pieces:
        s = _dot_nt(k, qp)
        scores.append(s if bias is None else s + bias)
    m = None
    for s in scores:
        mi = jnp.max(s, axis=0, keepdims=True)
        m = mi if m is None else jnp.maximum(m, mi)
    l = 0.0
    o = 0.0
    for s, (_, vt, _) in zip(scores, pieces):
        p = jnp.exp(s - m)
        l = l + jnp.sum(p, axis=0, keepdims=True)
        o = o + _dot(vt, p.astype(BF16))
    return o / l


def _select_rows(o_even, o_odd):
    return jnp.concatenate([o_even[:HEAD_DIM], o_odd[HEAD_DIM:]], axis=0)


def _ctx_attn_kernel(qa_ref, ka_ref, vat_ref, qc_ref, kc_ref, vct_ref, oa_ref, oc_ref):
    ka = ka_ref[...]
    vat = vat_ref[...].astype(BF16)
    outs = []
    for j in range(NA_HEADS // 2):
        k = _pair(ka, j)
        vt = vat[j * LANES:(j + 1) * LANES]
        q = _pair(qa_ref, j)
        outs.append(_select_rows(_attend_t(_keep_half(q, 0), [(k, vt, None)]),
                                 _attend_t(_keep_half(q, 1), [(k, vt, None)])))
    oa_ref[...] = jnp.concatenate(outs, axis=0).T.astype(oa_ref.dtype)
    kc = kc_ref[...]
    vct = vct_ref[...].astype(BF16)
    outs = []
    for h in range(GQ_HEADS):
        g = h // (GQ_HEADS // GQ_KV_HEADS)
        o = _attend_t(_keep_half(_pair(qc_ref, h // 2), h % 2, move_to=g), [(kc, vct, None)])
        outs.append(o[g * HEAD_DIM:(g + 1) * HEAD_DIM])
    oc_ref[...] = jnp.concatenate(outs, axis=0).T.astype(oc_ref.dtype)


def _ctx_attention(qa, ka, vat, qc, kc, vct, *, seq_len):
    t = qa.shape[0]
    row = lambda i: (i, 0)
    spec = lambda w: pl.BlockSpec((seq_len, w), row)
    tspec = lambda w: pl.BlockSpec((None, w, seq_len), lambda i: (i, 0, 0))
    return pl.pallas_call(
        _ctx_attn_kernel,
        grid=(t // seq_len,),
        in_specs=[spec(NA_WIDTH), spec(NA_WIDTH), tspec(NA_WIDTH), spec(GQ_Q_WIDTH), spec(GQ_KV_WIDTH),
                  tspec(GQ_KV_WIDTH)],
        out_specs=[spec(NA_WIDTH), spec(GQ_Q_WIDTH)],
        out_shape=[jax.ShapeDtypeStruct((t, NA_WIDTH), BF16), jax.ShapeDtypeStruct((t, GQ_Q_WIDTH), BF16)],
        compiler_params=_cparams(("arbitrary",), 32),
        name="attn_ctx",
    )(qa, ka, vat, qc, kc, vct)


def _na_key_row_start(blk, rows):
    return jnp.clip(blk * NA_QROWS - NA_KH // 2, 0, rows - NA_WIN_ROWS)


def _na_lat_kernel(rows, q_ref, k_ref, vt_ref, kc_ref, vct_ref, bias_ref, o_ref):
    base = pl.multiple_of(_na_key_row_start(pl.program_id(1), rows) * GRID_W, 2 * LANES)
    nk = NA_WIN_ROWS * GRID_W
    kwin = k_ref[pl.ds(base, nk), :]
    outs = []
    for j in range(NA_HEADS // 2):
        k = _pair(kwin, j)
        vt = vt_ref[j * LANES:(j + 1) * LANES, pl.ds(base, nk)]
        kc = _pair(kc_ref, j)
        vct = vct_ref[j * LANES:(j + 1) * LANES, :]
        q = _pair(q_ref, j)
        halves = [_attend_t(_keep_half(q, half), [(k, vt, bias_ref[2 * j + half]), (kc, vct, None)])
                  for half in range(2)]
        outs.append(_select_rows(*halves))
    o_ref[...] = jnp.concatenate(outs, axis=0).T.astype(o_ref.dtype)


def _na_latent(q, k, vt, k_ctx, vt_ctx, bias, *, n_seq, seq_len):
    rows = seq_len // GRID_W
    nblk = rows // NA_QROWS
    tq = NA_QROWS * GRID_W
    pattern = lambda b, i: (jnp.where(i == 0, 0, jnp.where(i == nblk - 1, 2, 1)), 0, 0, 0)
    return pl.pallas_call(
        functools.partial(_na_lat_kernel, rows),
        grid=(n_seq, nblk),
        in_specs=[
            pl.BlockSpec((tq, NA_WIDTH), lambda b, i: (b * nblk + i, 0)),
            pl.BlockSpec((seq_len, NA_WIDTH), lambda b, i: (b, 0)),
            pl.BlockSpec((None, NA_WIDTH, seq_len), lambda b, i: (b, 0, 0)),
            pl.BlockSpec((None,) + k_ctx.shape[1:], lambda b, i: (b, 0, 0)),
            pl.BlockSpec((None,) + vt_ctx.shape[1:], lambda b, i: (b, 0, 0)),
            pl.BlockSpec((None,) + bias.shape[1:], pattern),
        ],
        out_specs=pl.BlockSpec((tq, NA_WIDTH), lambda b, i: (b * nblk + i, 0)),
        out_shape=jax.ShapeDtypeStruct(q.shape, BF16),
        compiler_params=_cparams(("arbitrary", "arbitrary"), 52),
        name="attn_na_lat",
    )(q, k, vt, k_ctx, vt_ctx, bias)


def _gqa_lat_kernel(q_ref, k_ref, v_ref, o_ref):
    k = k_ref[...]
    for h in range(GQ_HEADS):
        g = h // (GQ_HEADS // GQ_KV_HEADS)
        s = _dot_nt(_keep_half(_pair(q_ref, h // 2), h % 2, move_to=g), k)
        p = jnp.exp(s - jnp.max(s, axis=-1, keepdims=True)).astype(BF16)
        o = _dot(p, v_ref[g])
        o_ref[:, h * HEAD_DIM:(h + 1) * HEAD_DIM] = (o[:, :HEAD_DIM] / o[:, HEAD_DIM:HEAD_DIM + 1]).astype(o_ref.dtype)


def _gqa_values_with_ones(v_all):
    b, s, _ = v_all.shape
    tail = jnp.concatenate([jnp.ones((b, s, 1), v_all.dtype), jnp.zeros((b, s, HEAD_DIM - 1), v_all.dtype)], axis=2)
    return jnp.stack([jnp.concatenate([v_all[:, :, g * HEAD_DIM:(g + 1) * HEAD_DIM], tail], axis=2)
                      for g in range(GQ_KV_HEADS)], axis=1)


def _gqa_latent(q, k_all, v_all, *, n_seq, seq_len):
    per = seq_len // TQ_GQA
    s_all = k_all.shape[1]
    return pl.pallas_call(
        _gqa_lat_kernel,
        grid=(n_seq, per),
        in_specs=[
            pl.BlockSpec((TQ_GQA, GQ_Q_WIDTH), lambda b, t: (b * per + t, 0)),
            pl.BlockSpec((None, s_all, GQ_KV_WIDTH), lambda b, t: (b, 0, 0)),
            pl.BlockSpec((None, GQ_KV_HEADS, s_all, 2 * HEAD_DIM), lambda b, t: (b, 0, 0, 0)),
        ],
        out_specs=pl.BlockSpec((TQ_GQA, GQ_Q_WIDTH), lambda b, t: (b * per + t, 0)),
        out_shape=jax.ShapeDtypeStruct(q.shape, BF16),
        compiler_params=_cparams(("arbitrary", "arbitrary"), 48),
        name="attn_gqa_lat",
    )(q, k_all, _gqa_values_with_ones(v_all))


def _conv_kernel(u_ref, up_ref, un_ref, w_ref, b_ref, g_ref, beta_ref, o_ref, shifted_ref):
    t = pl.program_id(1)
    nt = pl.num_programs(1)
    tl = u_ref.shape[0]

    def glu(v):
        return v[:, :CONV_CH].astype(F32) * _sigmoid(v[:, CONV_CH:].astype(F32))

    slab = jnp.concatenate([jnp.where(t > 0, glu(up_ref[...]), 0.0), glu(u_ref[...]),
                            jnp.where(t < nt - 1, glu(un_ref[...]), 0.0)], axis=0)
    first = HALO - CONV_K // 2
    span = shifted_ref.shape[1]
    assert (first + CONV_K - 1) // 8 * 8 + tl == span and span + 7 <= slab.shape[0]
    acc = jnp.zeros((tl, CONV_CH), F32)
    for shift in range(8):
        shifted_ref[shift % 2] = slab[shift:shift + span]
        for k in range(CONV_K):
            if (first + k) % 8 == shift:
                base = first + k - shift
                acc = acc + shifted_ref[shift % 2, base:base + tl, :] * w_ref[k:k + 1, :]
    y = _layer_norm(acc + b_ref[...], g_ref[...], beta_ref[...])
    o_ref[...] = (y * _sigmoid(y)).astype(o_ref.dtype)


def _conformer_conv(u, conv_w, conv_b, ln_g, ln_b, *, n_seq, seq_len):
    tl = TL_CONV
    nt = seq_len // tl
    hb = tl // HALO
    last_hb = seq_len // HALO - 1
    u3 = u.reshape(n_seq, seq_len, 2 * CONV_CH)
    vec = lambda a: a.reshape(1, CONV_CH)
    out = pl.pallas_call(
        _conv_kernel,
        grid=(n_seq, nt),
        in_specs=[
            pl.BlockSpec((None, tl, 2 * CONV_CH), lambda s, t: (s, t, 0)),
            pl.BlockSpec((None, HALO, 2 * CONV_CH), lambda s, t: (s, jnp.maximum(t * hb - 1, 0), 0)),
            pl.BlockSpec((None, HALO, 2 * CONV_CH), lambda s, t: (s, jnp.minimum((t + 1) * hb, last_hb), 0)),
            _resident((CONV_K, CONV_CH)),
            _resident((1, CONV_CH)),
            _resident((1, CONV_CH)),
            _resident((1, CONV_CH)),
        ],
        out_specs=pl.BlockSpec((None, tl, CONV_CH), lambda s, t: (s, t, 0)),
        out_shape=jax.ShapeDtypeStruct((n_seq, seq_len, CONV_CH), BF16),
        scratch_shapes=[pltpu.VMEM((2, (HALO + CONV_K // 2) // 8 * 8 + tl, CONV_CH), F32)],
        compiler_params=_cparams(("arbitrary", "arbitrary"), 32),
        name="conformer_conv",
    )(u3, u3, u3, conv_w, vec(conv_b), vec(ln_g), vec(ln_b))
    return out.reshape(n_seq * seq_len, CONV_CH)


def _router_gates(logits):
    lane = lax.broadcasted_iota(jnp.int32, logits.shape, 1)
    m = jnp.max(logits, axis=-1, keepdims=True)
    e = jnp.exp(logits - m)
    probs = e / jnp.sum(e, axis=-1, keepdims=True)
    p1 = jnp.max(probs, axis=-1, keepdims=True)
    i1 = jnp.min(jnp.where(probs == p1, lane, LANES), axis=-1, keepdims=True)
    in_group = (lane // EXPERTS_PER_GROUP) == (i1 // EXPERTS_PER_GROUP)
    cand = jnp.where(in_group, jnp.where(lane == i1, -1.0, probs), -1.0)
    p2 = jnp.max(cand, axis=-1, keepdims=True)
    i2 = jnp.min(jnp.where(cand == p2, lane, LANES), axis=-1, keepdims=True)
    den = p1 + p2
    gates = jnp.where(lane == i1, p1 / den, 0.0) + jnp.where(lane == i2, p2 / den, 0.0)
    group = (i1 // EXPERTS_PER_GROUP).astype(F32)
    return jnp.where(lane == GROUP_LANE, group, gates)


def _post_kernel(oa_ref, ob_ref, oc_ref, gt_ref, x_ref, g1_ref, lg_ref, lb_ref, sc2_ref, sh2_ref,
                 wa_ref, wb_ref, wc_ref, wo_ref, wr_ref, br_ref, x1_ref, hg_ref, gate_ref):
    sg = _sigmoid(gt_ref[...].astype(F32))
    merged = (sg[:, :D_MODEL] * _dot(oa_ref[...], wa_ref[...])
              + sg[:, D_MODEL:2 * D_MODEL] * _dot(ob_ref[...], wb_ref[...])
              + sg[:, 2 * D_MODEL:] * _dot(oc_ref[...], wc_ref[...]))
    y = _dot(merged.astype(BF16), wo_ref[...])
    x1 = _layer_norm(ALPHA * x_ref[...] + g1_ref[...] * y, lg_ref[...], lb_ref[...])
    x1_ref[...] = x1
    h2 = x1 * sc2_ref[...] + sh2_ref[...]
    tm = x1.shape[0]
    gates = _router_gates(_dot3(h2, wr_ref[...]) + br_ref[...])
    gate_ref[...] = gates
    block = jnp.concatenate([h2, gates, jnp.zeros((tm, (HG_ROWS - HG_USED) * LANES), F32)], axis=1)
    hg_ref[...] = pltpu.einshape("t(jl)->(tj)l", block, j=HG_ROWS)


def _post(oa, ob, oc, gates, x, g1, ln_g, ln_b, scale2, shift2, wa, wb, wc, wo, wr, br, *, latent, seq_len):
    t = x.shape[0]
    tm = TM_POST
    row = lambda i: (i, 0)
    mod_map = _mod_row_map(latent, seq_len // tm)
    mod = pl.BlockSpec((None, 1, D_MODEL), mod_map)
    return pl.pallas_call(
        _post_kernel,
        grid=(t // tm,),
        in_specs=[
            pl.BlockSpec((tm, NA_WIDTH), row), pl.BlockSpec((tm, CONV_CH), row), pl.BlockSpec((tm, GQ_Q_WIDTH), row),
            pl.BlockSpec((tm, 3 * D_MODEL), row), pl.BlockSpec((tm, D_MODEL), row),
            mod, _resident((1, D_MODEL)), _resident((1, D_MODEL)), mod, mod,
            _resident(wa.shape), _resident(wb.shape), _resident(wc.shape), _resident(wo.shape),
            _resident(wr.shape), _resident(br.shape),
        ],
        out_specs=[pl.BlockSpec((tm, D_MODEL), row), pl.BlockSpec((tm * HG_ROWS, LANES), row),
                   pl.BlockSpec((tm, LANES), row)],
        out_shape=[jax.ShapeDtypeStruct((t, D_MODEL), F32), jax.ShapeDtypeStruct((t * HG_ROWS, LANES), F32),
                   jax.ShapeDtypeStruct((t, LANES), F32)],
        compiler_params=_cparams(("arbitrary",), 40),
        name="post_lat" if latent else "post_ctx",
    )(oa, ob, oc, gates, x, g1, ln_g, ln_b, scale2, shift2, wa, wb, wc, wo, wr, br)


def _route(grp, tm):
    n = grp.shape[0]
    n_tiles = n // tm + N_GROUPS
    groups = jnp.arange(N_GROUPS, dtype=jnp.int32)
    onehot = (grp[:, None] == groups[None, :]).astype(jnp.int32)
    csum = jnp.cumsum(onehot, axis=0)
    rank = jnp.sum((csum - onehot) * onehot, axis=1)
    counts = csum[-1]
    tiles = (counts + tm - 1) // tm
    tile_end = jnp.cumsum(tiles)
    tile_start = tile_end - tiles
    slot = jnp.sum(onehot * tile_start[None, :], axis=1) * tm + rank
    tile_ids = jnp.arange(n_tiles, dtype=jnp.int32)
    tile_group = jnp.minimum(jnp.sum((tile_ids[:, None] >= tile_end[None, :]).astype(jnp.int32), axis=1), N_GROUPS - 1)
    src = jnp.zeros((n_tiles * tm,), jnp.int32).at[slot].set(jnp.arange(n, dtype=jnp.int32))
    r = jnp.arange(n_tiles * tm, dtype=jnp.int32)
    slot_tile = r // tm
    in_group = (jnp.repeat(tile_group, tm)[:, None] == groups[None, :]).astype(jnp.int32)
    slot_rank = r - jnp.sum(in_group * tile_start[None, :], axis=1) * tm
    valid = (slot_tile < tile_end[-1]) & (slot_rank < jnp.sum(in_group * counts[None, :], axis=1))
    dst = jnp.where(valid, src, n + (slot_tile % 2) * tm + r % tm)
    return tile_group.astype(jnp.int32), src, dst, tile_end[-1:].astype(jnp.int32)


def _moe_kernel(tg_ref, src_ref, dst_ref, nused_ref, h_hbm, wgu_ref, wd_ref, y_hbm, hbuf, ybuf, gsem, ssem):
    t = pl.program_id(0)
    n_used = nused_ref[0]
    slot = t % 2
    tm = ybuf.shape[1] // Y_ROWS

    def gather_row(tile, s, r):
        tok = src_ref[tile * tm + r]
        return pltpu.make_async_copy(h_hbm.at[pl.ds(pl.multiple_of(tok * HG_ROWS, HG_ROWS), HG_USED), :],
                                     hbuf.at[s, pl.ds(pl.multiple_of(r * HG_ROWS, HG_ROWS), HG_USED), :], gsem.at[s])

    def scatter_row(tile, s, r):
        row = dst_ref[tile * tm + r]
        return pltpu.make_async_copy(ybuf.at[s, pl.ds(pl.multiple_of(r * Y_ROWS, Y_ROWS), Y_ROWS), :],
                                     y_hbm.at[pl.ds(pl.multiple_of(row * Y_ROWS, Y_ROWS), Y_ROWS), :], ssem.at[s])

    def for_rows(fn):
        def body(r, carry):
            fn(r)
            return carry
        lax.fori_loop(0, tm, body, 0, unroll=8)

    @pl.when(t == 0)
    def _():
        for_rows(lambda r: gather_row(0, 0, r).start())

    @pl.when(t == n_used)
    def _():
        for_rows(lambda r: gather_row(t, slot, r).wait())

    @pl.when(t < n_used)
    def _():
        for_rows(lambda r: gather_row(t, slot, r).wait())

        @pl.when(t >= 2)
        def _():
            for_rows(lambda r: scatter_row(t - 2, slot, r).wait())

        first = tg_ref[t] * EXPERTS_PER_GROUP
        token_rows = lambda ref, j, per: ref[slot, pl.ds(j, tm, stride=per), :]
        h = jnp.concatenate([token_rows(hbuf, j, HG_ROWS) for j in range(Y_ROWS)], axis=1).astype(BF16)
        gates = token_rows(hbuf, Y_ROWS, HG_ROWS)
        lane = lax.broadcasted_iota(jnp.int32, gates.shape, 1)
        acc = jnp.zeros((tm, D_MODEL), F32)
        rows_per_expert = tm // EXPERTS_PER_GROUP
        for e in range(EXPERTS_PER_GROUP):
            for r in range(e * rows_per_expert, (e + 1) * rows_per_expert):
                gather_row(t + 1, 1 - slot, r).start()
            gu = _dot(h, wgu_ref[e])
            a = gu[:, :D_EXPERT]
            ge = jnp.sum(jnp.where(lane == first + e, gates, 0.0), axis=-1, keepdims=True)
            hid = a * _sigmoid(a) * gu[:, D_EXPERT:] * ge
            acc = acc + _dot(hid.astype(BF16), wd_ref[e])
        for j in range(Y_ROWS):
            ybuf[slot, pl.ds(j, tm, stride=Y_ROWS), :] = acc[:, j * LANES:(j + 1) * LANES]
        for_rows(lambda r: scatter_row(t, slot, r).start())

        @pl.when(t == n_used - 1)
        def _():
            for_rows(lambda r: scatter_row(t, slot, r).wait())

            @pl.when(t >= 1)
            def _():
                for_rows(lambda r: scatter_row(t - 1, 1 - slot, r).wait())

            n_rows = y_hbm.shape[0] - 2 * tm * Y_ROWS
            ybuf[0] = jnp.zeros(ybuf.shape[1:], F32)
            for half in range(2):
                fill = pltpu.make_async_copy(
                    ybuf.at[0], y_hbm.at[pl.ds(n_rows + half * tm * Y_ROWS, tm * Y_ROWS), :], ssem.at[half])
                fill.start()
                fill.wait()


def _moe_experts(hg, route, wgu, wd, layer):
    n = hg.shape[0] // HG_ROWS
    tm = TM_MOE
    tile_group, src, dst, n_used = route
    n_tiles = tile_group.shape[0]
    grid_spec = pltpu.PrefetchScalarGridSpec(
        num_scalar_prefetch=4,
        grid=(n_tiles,),
        in_specs=[
            pl.BlockSpec(memory_space=pl.ANY),
            pl.BlockSpec((None, EXPERTS_PER_GROUP, D_MODEL, 2 * D_EXPERT), lambda t, tg, s, d, nu: (layer, tg[t], 0, 0)),
            pl.BlockSpec((None, EXPERTS_PER_GROUP, D_EXPERT, D_MODEL), lambda t, tg, s, d, nu: (layer, tg[t], 0, 0)),
        ],
        out_specs=pl.BlockSpec(memory_space=pl.ANY),
        scratch_shapes=[
            pltpu.VMEM((2, tm * HG_ROWS, LANES), F32),
            pltpu.VMEM((2, tm * Y_ROWS, LANES), F32),
            pltpu.SemaphoreType.DMA((2,)),
            pltpu.SemaphoreType.DMA((2,)),
        ],
    )
    return pl.pallas_call(
        _moe_kernel,
        grid_spec=grid_spec,
        out_shape=jax.ShapeDtypeStruct(((n + 2 * tm) * Y_ROWS, LANES), F32),
        compiler_params=_cparams(("arbitrary",), 58),
        name="moe_experts",
    )(tile_group, src, dst, n_used, hg, wgu, wd)


def _moe_out_kernel(x1_ref, y_ref, g2_ref, lg_ref, lb_ref, o_ref):
    o_ref[...] = _moe_residual_norm(x1_ref, y_ref, g2_ref, lg_ref, lb_ref)


def _moe_out(x1, y, g2, ln_g, ln_b, *, latent, seq_len):
    t = x1.shape[0]
    tm = TM_POST
    row = lambda i: (i, 0)
    return pl.pallas_call(
        _moe_out_kernel,
        grid=(t // tm,),
        in_specs=[pl.BlockSpec((tm, D_MODEL), row), pl.BlockSpec((tm * Y_ROWS, LANES), row),
                  pl.BlockSpec((None, 1, D_MODEL), _mod_row_map(latent, seq_len // tm)),
                  _resident((1, D_MODEL)), _resident((1, D_MODEL))],
        out_specs=pl.BlockSpec((tm, D_MODEL), row),
        out_shape=jax.ShapeDtypeStruct((t, D_MODEL), F32),
        compiler_params=_cparams(("arbitrary",), 32),
        name="moe_out",
    )(x1, y, g2, ln_g, ln_b)


def _moe(hg, gates, wgu, wd, layer):
    return _moe_experts(hg, _route(gates[:, GROUP_LANE].astype(jnp.int32), TM_MOE), wgu, wd, layer)


def _head_mean_matrix():
    i = jnp.arange(GQ_Q_WIDTH) // HEAD_DIM
    return jnp.where(i[:, None] == i[None, :], 1.0 / HEAD_DIM, 0.0).astype(BF16)


def _rope_tables(n_tokens):
    t = jnp.arange(n_tokens, dtype=jnp.int32)
    inv_freq = ROPE_THETA ** (-jnp.arange(0, ROPE_AXIS, 2, dtype=F32) / ROPE_AXIS)
    ar = (t // GRID_W).astype(F32)[:, None] * inv_freq
    ac = (t % GRID_W).astype(F32)[:, None] * inv_freq
    c = jnp.concatenate([jnp.cos(ar), jnp.cos(ar), jnp.cos(ac), jnp.cos(ac)], axis=1)
    s = jnp.concatenate([-jnp.sin(ar), jnp.sin(ar), -jnp.sin(ac), jnp.sin(ac)], axis=1)
    return jnp.tile(c, (1, LANES // HEAD_DIM)), jnp.tile(s, (1, LANES // HEAD_DIM))


def _na_bias_kernel(rows, t_ref, o_ref):
    masked = jnp.full((GRID_W, GRID_W), NEG_INF, F32)
    for pat, first_qrow in enumerate((0, NA_KH // 2, rows - NA_QROWS)):
        krow0 = min(max(first_qrow - NA_KH // 2, 0), rows - NA_WIN_ROWS)
        for kj in range(NA_WIN_ROWS):
            blocks = []
            for qi in range(NA_QROWS):
                qrow, krow = first_qrow + qi, krow0 + kj
                win0 = min(max(qrow - NA_KH // 2, 0), rows - NA_KH)
                in_window = win0 <= krow < win0 + NA_KH
                blocks.append(t_ref[krow - qrow + NA_KH - 1] if in_window else masked)
            o_ref[pat, kj * GRID_W:(kj + 1) * GRID_W, :] = jnp.concatenate(blocks, axis=1)


def _na_bias_table(rpb, rows):
    col = np.arange(GRID_W)
    col_idx = np.clip(col[:, None] - col[None, :], -(NA_KW - 1), NA_KW - 1) + NA_KW - 1
    onehot = (col_idx[None] == np.arange(2 * NA_KW - 1)[:, None, None]).astype(np.float32)
    col_start = np.clip(col - NA_KW // 2, 0, GRID_W - NA_KW)
    col_ok = (col[:, None] >= col_start[None, :]) & (col[:, None] < col_start[None, :] + NA_KW)
    t = jnp.einsum("lhic,ckq->lhikq", rpb, onehot, precision=lax.Precision.HIGHEST)
    t = jnp.where(col_ok, t, NEG_INF)
    depth, heads, n_rel = t.shape[:3]
    nk, nq = NA_WIN_ROWS * GRID_W, NA_QROWS * GRID_W
    return pl.pallas_call(
        functools.partial(_na_bias_kernel, rows),
        grid=(depth, heads),
        in_specs=[pl.BlockSpec((None, None, n_rel, GRID_W, GRID_W), lambda l, h: (l, h, 0, 0, 0))],
        out_specs=pl.BlockSpec((None, 3, None, nk, nq), lambda l, h: (l, 0, h, 0, 0)),
        out_shape=jax.ShapeDtypeStruct((depth, 3, heads, nk, nq), F32),
        compiler_params=_cparams(("arbitrary", "arbitrary"), 32),
        name="na_bias_table",
    )(t)


def _stream_layer(x, mods, l, wts, *, latent, n_seq, seq_len, cache=None, rope=None):
    sh1, sc1, g1, sh2, sc2, g2 = mods
    pieces = _proj(x, 1.0 + sc1, sh1, wts["w_in"], wts["gmat"], wts["qn"][l], wts["kn"][l], rope,
                   latent=latent, seq_len=seq_len, layer=l)
    if isinstance(x, tuple):
        x, pieces = pieces[0], pieces[1:]
    qa, ka, va, u, qc, kc, vc, gates = pieces[:8]
    new_cache_t = pieces[8:]
    if latent:
        nat_k, nat_v, gqa_k, gqa_v = cache
        swap = lambda a: jnp.swapaxes(a, 1, 2)
        out_a = _na_latent(qa, ka, swap(va.reshape(n_seq, seq_len, NA_WIDTH)), nat_k, swap(nat_v),
                           wts["na_bias"][l], n_seq=n_seq, seq_len=seq_len)
        k_all = jnp.concatenate([gqa_k, kc.reshape(n_seq, seq_len, GQ_KV_WIDTH)], axis=1)
        v_all = jnp.concatenate([gqa_v, vc.reshape(n_seq, seq_len, GQ_KV_WIDTH)], axis=1)
        out_c = _gqa_latent(qc, k_all, v_all, n_seq=n_seq, seq_len=seq_len)
    else:
        out_a, out_c = _ctx_attention(qa, ka, new_cache_t[1], qc, kc, new_cache_t[3], seq_len=seq_len)
    out_b = _conformer_conv(u, wts["conv_w"][l], wts["conv_b"][l], wts["conv_ln_g"][l], wts["conv_ln_b"][l],
                            n_seq=n_seq, seq_len=seq_len)
    x1, hg, moe_gates = _post(out_a, out_b, out_c, gates, x, g1, wts["ln1_g"][l], wts["ln1_b"][l], 1.0 + sc2, sh2,
                         wts["w_br_a"][l], wts["w_br_b"][l], wts["w_br_c"][l], wts["w_out"][l],
                         wts["w_router"], wts["b_router"], latent=latent, seq_len=seq_len)
    y = _moe(hg, moe_gates, wts["w_gate_up"], wts["w_down"], l)
    return (x1, y, g2, wts["ln2_g"][l], wts["ln2_b"][l]), new_cache_t


def kernel(x_prompt, x_sample, cache_nat_k, cache_nat_v, cache_gqa_k, cache_gqa_v, c, c_ctx, w_mod, b_mod, w_in, nat_rpb, conv_w, conv_b, conv_ln_g, conv_ln_b, q_norm_g, k_norm_g, w_br_a, w_br_b, w_br_c, w_out, ln1_g, ln1_b, ln2_g, ln2_b, w_router, b_router, w_gate_up, w_down):
    batch, seq, _ = x_prompt.shape
    dec_batch, dec_seq, _ = x_sample.shape
    past = cache_nat_k.shape[2]
    assert dec_batch == 2 and (batch * seq) % TM_MOE == 0 and dec_seq % TM_MOE == 0 and seq % TL_CONV == 0
    assert TM_MOE % TM_PROJ == 0 and TM_MOE % TM_POST == 0

    cond8 = jnp.zeros((8, D_MODEL), F32).at[0].set(c_ctx).at[1:1 + dec_batch].set(c)
    mods_all = _modulation(cond8, w_mod, b_mod)

    pad = LANES - N_EXPERTS
    vecd = lambda a: a.reshape(DEPTH, 1, -1)
    per_layer_bf16 = lambda w: [w[l].astype(BF16) for l in range(DEPTH)]
    rows = dec_seq // GRID_W
    wts = dict(
        w_in=w_in.astype(BF16),
        gmat=_head_mean_matrix(),
        qn=jnp.tile(q_norm_g, (1, GQ_HEADS)).reshape(DEPTH, 1, GQ_Q_WIDTH),
        kn=jnp.tile(k_norm_g, (1, GQ_KV_HEADS)).reshape(DEPTH, 1, GQ_KV_WIDTH),
        na_bias=[_na_bias_table(nat_rpb[l:l + 1], rows).reshape(3, NA_HEADS, NA_WIN_ROWS * GRID_W, NA_QROWS * GRID_W)
                 for l in range(DEPTH)],
        conv_w=conv_w, conv_b=conv_b, conv_ln_g=conv_ln_g, conv_ln_b=conv_ln_b,
        ln1_g=vecd(ln1_g), ln1_b=vecd(ln1_b), ln2_g=vecd(ln2_g), ln2_b=vecd(ln2_b),
        w_br_a=per_layer_bf16(w_br_a), w_br_b=per_layer_bf16(w_br_b), w_br_c=per_layer_bf16(w_br_c),
        w_out=per_layer_bf16(w_out),
        w_router=jnp.pad(w_router, ((0, 0), (0, pad))),
        b_router=jnp.pad(b_router, (0, pad), constant_values=NEG_INF).reshape(1, LANES),
        w_gate_up=w_gate_up.astype(BF16), w_down=w_down.astype(BF16),
    )
    rope = _rope_tables(dec_seq)

    def layer_mods(l):
        m = mods_all[l, :1 + dec_batch].reshape(1 + dec_batch, 6, 1, D_MODEL)
        return [m[:, i] for i in range(6)]

    xp = x_prompt.reshape(batch * seq, D_MODEL)
    xs = x_sample.reshape(dec_batch * dec_seq, D_MODEL)
    new_cache = []
    for l in range(DEPTH):
        mods = layer_mods(l)
        xp, ctx_kv = _stream_layer(xp, mods, l, wts, latent=False, n_seq=batch, seq_len=seq)
        new_cache.append(ctx_kv)
        cache = (cache_nat_k[:, l].reshape(dec_batch, past, NA_WIDTH).astype(BF16),
                 cache_nat_v[:, l].reshape(dec_batch, past, NA_WIDTH).astype(BF16),
                 cache_gqa_k[:, l].reshape(dec_batch, past, GQ_KV_WIDTH).astype(BF16),
                 cache_gqa_v[:, l].reshape(dec_batch, past, GQ_KV_WIDTH).astype(BF16))
        xs, _ = _stream_layer(xs, mods, l, wts, latent=True, n_seq=dec_batch, seq_len=dec_seq, cache=cache, rope=rope)
    xp = _moe_out(*xp, latent=False, seq_len=seq)
    xs = _moe_out(*xs, latent=True, seq_len=dec_seq)

    def stack(i, heads):
        per_layer = jnp.stack([kv[i].reshape(batch, heads, HEAD_DIM, seq) for kv in new_cache], axis=1)
        return per_layer.transpose(0, 1, 4, 2, 3)

    return (xp.reshape(batch, seq, D_MODEL), xs.reshape(dec_batch, dec_seq, D_MODEL),
            stack(0, NA_HEADS), stack(1, NA_HEADS), stack(2, GQ_KV_HEADS), stack(3, GQ_KV_HEADS))
```

```python
import functools

import numpy as np
import jax
import jax.numpy as jnp
from jax import lax
from jax.experimental import pallas as pl
from jax.experimental.pallas import tpu as pltpu

F32 = jnp.float32
BF16 = jnp.bfloat16

D_MODEL = 1024
DEPTH = 2
GRID_W = 64
HEAD_DIM = 64
NA_HEADS = 8
NA_WIDTH = NA_HEADS * HEAD_DIM
NA_KH = 8
NA_KW = 16
NA_QROWS = 4
NA_WIN_ROWS = NA_QROWS + NA_KH
CONV_CH = 512
CONV_K = 31
GQ_HEADS = 8
GQ_KV_HEADS = 2
GQ_Q_WIDTH = GQ_HEADS * HEAD_DIM
GQ_KV_WIDTH = GQ_KV_HEADS * HEAD_DIM
ROPE_THETA = 10000.0
N_EXPERTS = 16
EXPERTS_PER_GROUP = 4
N_GROUPS = N_EXPERTS // EXPERTS_PER_GROUP
GROUP_LANE = N_EXPERTS
D_EXPERT = 512
ALPHA = (2 * DEPTH) ** 0.25
LN_EPS = 1e-6
RMS_EPS = 1e-6
NEG_INF = -1e30
QK_SCALE = HEAD_DIM ** -0.5

LANES = 128
Y_ROWS = D_MODEL // LANES
HG_ROWS = 2 * Y_ROWS
HG_USED = Y_ROWS + 1

IN_SIZES = (NA_WIDTH, NA_WIDTH, NA_WIDTH, 2 * CONV_CH, GQ_Q_WIDTH, GQ_KV_WIDTH, GQ_KV_WIDTH, 3 * D_MODEL)
_OFF = tuple(int(o) for o in np.cumsum((0,) + IN_SIZES))
IN_COLS = _OFF[-1]
ROPE_AXIS = HEAD_DIM // 2
V7X_VMEM_BYTES = 64 * 2 ** 20
HALO = 16

TM_PROJ = 256
TM_POST = 256
TM_MOE = 512
TL_CONV = 256
TQ_GQA = 256


def _cparams(sem, vmem_mib):
    assert vmem_mib * 2 ** 20 < V7X_VMEM_BYTES
    return pltpu.CompilerParams(dimension_semantics=sem, vmem_limit_bytes=vmem_mib * 2 ** 20)


def _resident(shape):
    nd = len(shape)
    return pl.BlockSpec(shape, lambda *_: (0,) * nd, pipeline_mode=pl.Buffered(1))


def _dot(a, b):
    return jnp.dot(a, b, preferred_element_type=F32)


def _dot_nt(a, b):
    return lax.dot_general(a, b, (((1,), (1,)), ((), ())), preferred_element_type=F32)


def _split_bf16(a):
    hi = a.astype(BF16)
    lo = (a - hi.astype(F32)).astype(BF16)
    return hi, lo


def _dot3(a, b):
    a_hi, a_lo = _split_bf16(a)
    b_hi, b_lo = _split_bf16(b)
    return _dot(a_hi, b_hi) + _dot(a_lo, b_hi) + _dot(a_hi, b_lo)


def _sigmoid(x):
    return 1.0 / (1.0 + jnp.exp(-x))


def _layer_norm(x, g, b):
    mu = jnp.mean(x, axis=-1, keepdims=True)
    xc = x - mu
    var = jnp.mean(xc * xc, axis=-1, keepdims=True)
    return xc * lax.rsqrt(var + LN_EPS) * g + b


def _mod_kernel(cond_ref, w_ref, b_ref, o_ref):
    c = cond_ref[...]
    o_ref[...] = _dot3(c * _sigmoid(c), w_ref[...]) + b_ref[...]


def _modulation(cond8, w_mod, b_mod):
    tn = 1536
    n = w_mod.shape[-1]
    return pl.pallas_call(
        _mod_kernel,
        grid=(DEPTH, n // tn),
        in_specs=[
            pl.BlockSpec((8, D_MODEL), lambda l, j: (0, 0)),
            pl.BlockSpec((None, D_MODEL, tn), lambda l, j: (l, 0, j)),
            pl.BlockSpec((None, 1, tn), lambda l, j: (l, 0, j)),
        ],
        out_specs=pl.BlockSpec((None, 8, tn), lambda l, j: (l, 0, j)),
        out_shape=jax.ShapeDtypeStruct((DEPTH, 8, n), F32),
        compiler_params=_cparams(("arbitrary", "arbitrary"), 40),
        name="modulation",
    )(cond8, w_mod, b_mod.reshape(DEPTH, 1, n))


def _head_rms(x, gmat, gain):
    hi, lo = _split_bf16(x * x)
    ms = _dot(hi, gmat) + _dot(lo, gmat)
    return x * lax.rsqrt(ms + RMS_EPS) * gain


def _rope(x, c, s):
    w = x.shape[1]
    reps = w // LANES
    if reps > 1:
        c = jnp.concatenate([c] * reps, axis=1)
        s = jnp.concatenate([s] * reps, axis=1)
    lane = lax.broadcasted_iota(jnp.int32, x.shape, 1)
    half = ROPE_AXIS // 2
    partner = jnp.where((lane % ROPE_AXIS) < half, pltpu.roll(x, w - half, 1), pltpu.roll(x, half, 1))
    return x * c + partner * s


def _moe_residual_norm(x1_ref, y_ref, g2_ref, lg_ref, lb_ref):
    tm = x1_ref.shape[0]
    y = jnp.concatenate([y_ref[pl.ds(j, tm, stride=Y_ROWS), :] for j in range(Y_ROWS)], axis=1)
    return _layer_norm(ALPHA * x1_ref[...] + g2_ref[...] * y, lg_ref[...], lb_ref[...])


def _proj_kernel(latent, fused, *refs):
    refs = list(refs)
    n_in = (5 if fused else 1) + 6 + (2 if latent else 0)
    ins, outs = refs[:n_in], refs[n_in:]
    x = _moe_residual_norm(*ins[:5]) if fused else ins[0][...]
    sc_ref, sh_ref, w_ref, gm_ref, qn_ref, kn_ref = ins[n_in - 6 - (2 if latent else 0):][:6]
    if latent:
        c_ref, s_ref = ins[-2:]
    if fused:
        outs.pop(0)[...] = x
    qa_ref, ka_ref, va_ref, u_ref, qc_ref, kc_ref, vc_ref, gt_ref = outs[:8]
    h = (x * sc_ref[...] + sh_ref[...]).astype(BF16)

    def mm(i):
        return _dot(h, w_ref[:, _OFF[i]:_OFF[i + 1]])

    qa_ref[...] = (mm(0) * QK_SCALE).astype(qa_ref.dtype)
    ka, va = mm(1), mm(2)
    u_ref[...] = mm(3).astype(u_ref.dtype)
    qc = _head_rms(mm(4), gm_ref[...], qn_ref[...])
    kc = _head_rms(mm(5), gm_ref[:GQ_KV_WIDTH, :GQ_KV_WIDTH], kn_ref[...])
    if latent:
        qc = _rope(qc, c_ref[...], s_ref[...])
        kc = _rope(kc, c_ref[...], s_ref[...])
    qc_ref[...] = (qc * QK_SCALE).astype(qc_ref.dtype)
    vc = mm(6)
    gt_ref[...] = mm(7).astype(gt_ref.dtype)
    for ref, val in ((ka_ref, ka), (va_ref, va), (kc_ref, kc), (vc_ref, vc)):
        ref[...] = val.astype(ref.dtype)
    for ref, val in zip(outs[8:], (ka, va, kc, vc)):
        ref[...] = val.T


def _mod_row_map(latent, tiles_per_seq):
    if latent:
        return lambda i: (1 + i // tiles_per_seq, 0, 0)
    return lambda i: (0, 0, 0)


def _proj(x, scale, shift, w_bf, gmat, qn, kn, rope, *, latent, seq_len, layer):
    fused = isinstance(x, tuple)
    t = (x[0] if fused else x).shape[0]
    tm = TM_PROJ
    row = lambda i: (i, 0)
    mod_map = _mod_row_map(latent, seq_len // tm)
    if fused:
        in_specs = [pl.BlockSpec((tm, D_MODEL), row), pl.BlockSpec((tm * Y_ROWS, LANES), row),
                    pl.BlockSpec((None, 1, D_MODEL), mod_map), _resident((1, D_MODEL)), _resident((1, D_MODEL))]
        args = list(x)
    else:
        in_specs = [pl.BlockSpec((tm, D_MODEL), row)]
        args = [x]
    in_specs += [
        pl.BlockSpec((None, 1, D_MODEL), mod_map),
        pl.BlockSpec((None, 1, D_MODEL), mod_map),
        pl.BlockSpec((None, D_MODEL, IN_COLS), lambda i: (layer, 0, 0), pipeline_mode=pl.Buffered(1)),
        _resident((GQ_Q_WIDTH, GQ_Q_WIDTH)),
        _resident((1, GQ_Q_WIDTH)),
        _resident((1, GQ_KV_WIDTH)),
    ]
    args += [scale, shift, w_bf, gmat, qn, kn]
    if latent:
        per = seq_len // tm
        in_specs += [pl.BlockSpec((tm, LANES), lambda i: (i % per, 0))] * 2
        args += list(rope)
    widths = IN_SIZES
    dtypes = (BF16,) * len(widths)
    if fused:
        widths = (D_MODEL,) + widths
        dtypes = (F32,) + dtypes
    out_specs = [pl.BlockSpec((tm, w), row) for w in widths]
    out_shape = [jax.ShapeDtypeStruct((t, w), dt) for w, dt in zip(widths, dtypes)]
    if not latent:
        per = seq_len // tm
        for w in (NA_WIDTH, NA_WIDTH, GQ_KV_WIDTH, GQ_KV_WIDTH):
            out_specs.append(pl.BlockSpec((None, w, tm), lambda i: (i // per, 0, i % per)))
            out_shape.append(jax.ShapeDtypeStruct((t // seq_len, w, seq_len), F32))
    return pl.pallas_call(
        functools.partial(_proj_kernel, latent, fused),
        grid=(t // tm,),
        in_specs=in_specs,
        out_specs=out_specs,
        out_shape=out_shape,
        compiler_params=_cparams(("arbitrary",), 48),
        name="proj_lat" if latent else "proj_ctx",
    )(*args)


def _pair(ref_or_val, j):
    return ref_or_val[:, j * LANES:(j + 1) * LANES]


def _keep_half(x, half, move_to=None):
    lane = lax.broadcasted_iota(jnp.int32, x.shape, 1)
    keep = (lane >= HEAD_DIM) if half else (lane < HEAD_DIM)
    if move_to is None or move_to == half:
        return jnp.where(keep, x, jnp.zeros_like(x))
    xf = jnp.where(keep, x.astype(F32), 0.0)
    return pltpu.roll(xf, HEAD_DIM, 1).astype(x.dtype)


def _attend_t(qp, pieces):
    scores = []
    for k, _, bias in pieces:
        s = _dot_nt(k, qp)
        scores.append(s if bias is None else s + bias)
    m = None
    for s in scores:
        mi = jnp.max(s, axis=0, keepdims=True)
        m = mi if m is None else jnp.maximum(m, mi)
    l = 0.0
    o = 0.0
    for s, (_, vt, _) in zip(scores, pieces):
        p = jnp.exp(s - m)
        l = l + jnp.sum(p, axis=0, keepdims=True)
        o = o + _dot(vt, p.astype(BF16))
    return o / l


def _select_rows(o_even, o_odd):
    return jnp.concatenate([o_even[:HEAD_DIM], o_odd[HEAD_DIM:]], axis=0)


def _ctx_attn_kernel(qa_ref, ka_ref, vat_ref, qc_ref, kc_ref, vct_ref, oa_ref, oc_ref):
    ka = ka_ref[...]
    vat = vat_ref[...].astype(BF16)
    outs = []
    for j in range(NA_HEADS // 2):
        k = _pair(ka, j)
        vt = vat[j * LANES:(j + 1) * LANES]
        q = _pair(qa_ref, j)
        outs.append(_select_rows(_attend_t(_keep_half(q, 0), [(k, vt, None)]),
                                 _attend_t(_keep_half(q, 1), [(k, vt, None)])))
    oa_ref[...] = jnp.concatenate(outs, axis=0).T.astype(oa_ref.dtype)
    kc = kc_ref[...]
    vct = vct_ref[...].astype(BF16)
    outs = []
    for h in range(GQ_HEADS):
        g = h // (GQ_HEADS // GQ_KV_HEADS)
        o = _attend_t(_keep_half(_pair(qc_ref, h // 2), h % 2, move_to=g), [(kc, vct, None)])
        outs.append(o[g * HEAD_DIM:(g + 1) * HEAD_DIM])
    oc_ref[...] = jnp.concatenate(outs, axis=0).T.astype(oc_ref.dtype)


def _ctx_attention(qa, ka, vat, qc, kc, vct, *, seq_len):
    t = qa.shape[0]
    row = lambda i: (i, 0)
    spec = lambda w: pl.BlockSpec((seq_len, w), row)
    tspec = lambda w: pl.BlockSpec((None, w, seq_len), lambda i: (i, 0, 0))
    return pl.pallas_call(
        _ctx_attn_kernel,
        grid=(t // seq_len,),
        in_specs=[spec(NA_WIDTH), spec(NA_WIDTH), tspec(NA_WIDTH), spec(GQ_Q_WIDTH), spec(GQ_KV_WIDTH),
                  tspec(GQ_KV_WIDTH)],
        out_specs=[spec(NA_WIDTH), spec(GQ_Q_WIDTH)],
        out_shape=[jax.ShapeDtypeStruct((t, NA_WIDTH), BF16), jax.ShapeDtypeStruct((t, GQ_Q_WIDTH), BF16)],
        compiler_params=_cparams(("arbitrary",), 32),
        name="attn_ctx",
    )(qa, ka, vat, qc, kc, vct)


def _na_key_row_start(blk, rows):
    return jnp.clip(blk * NA_QROWS - NA_KH // 2, 0, rows - NA_WIN_ROWS)


def _na_lat_kernel(rows, q_ref, k_ref, vt_ref, kc_ref, vct_ref, bias_ref, o_ref):
    base = pl.multiple_of(_na_key_row_start(pl.program_id(1), rows) * GRID_W, 2 * LANES)
    nk = NA_WIN_ROWS * GRID_W
    kwin = k_ref[pl.ds(base, nk), :]
    outs = []
    for j in range(NA_HEADS // 2):
        k = _pair(kwin, j)
        vt = vt_ref[j * LANES:(j + 1) * LANES, pl.ds(base, nk)]
        kc = _pair(kc_ref, j)
        vct = vct_ref[j * LANES:(j + 1) * LANES, :]
        q = _pair(q_ref, j)
        halves = [_attend_t(_keep_half(q, half), [(k, vt, bias_ref[2 * j + half]), (kc, vct, None)])
                  for half in range(2)]
        outs.append(_select_rows(*halves))
    o_ref[...] = jnp.concatenate(outs, axis=0).T.astype(o_ref.dtype)


def _na_latent(q, k, vt, k_ctx, vt_ctx, bias, *, n_seq, seq_len):
    rows = seq_len // GRID_W
    nblk = rows // NA_QROWS
    tq = NA_QROWS * GRID_W
    pattern = lambda b, i: (jnp.where(i == 0, 0, jnp.where(i == nblk - 1, 2, 1)), 0, 0, 0)
    return pl.pallas_call(
        functools.partial(_na_lat_kernel, rows),
        grid=(n_seq, nblk),
        in_specs=[
            pl.BlockSpec((tq, NA_WIDTH), lambda b, i: (b * nblk + i, 0)),
            pl.BlockSpec((seq_len, NA_WIDTH), lambda b, i: (b, 0)),
            pl.BlockSpec((None, NA_WIDTH, seq_len), lambda b, i: (b, 0, 0)),
            pl.BlockSpec((None,) + k_ctx.shape[1:], lambda b, i: (b, 0, 0)),
            pl.BlockSpec((None,) + vt_ctx.shape[1:], lambda b, i: (b, 0, 0)),
            pl.BlockSpec((None,) + bias.shape[1:], pattern),
        ],
        out_specs=pl.BlockSpec((tq, NA_WIDTH), lambda b, i: (b * nblk + i, 0)),
        out_shape=jax.ShapeDtypeStruct(q.shape, BF16),
        compiler_params=_cparams(("arbitrary", "arbitrary"), 52),
        name="attn_na_lat",
    )(q, k, vt, k_ctx, vt_ctx, bias)


def _gqa_lat_kernel(q_ref, k_ref, v_ref, o_ref):
    k = k_ref[...]
    for h in range(GQ_HEADS):
        g = h // (GQ_HEADS // GQ_KV_HEADS)
        s = _dot_nt(_keep_half(_pair(q_ref, h // 2), h % 2, move_to=g), k)
        p = jnp.exp(s - jnp.max(s, axis=-1, keepdims=True)).astype(BF16)
        o = _dot(p, v_ref[g])
        o_ref[:, h * HEAD_DIM:(h + 1) * HEAD_DIM] = (o[:, :HEAD_DIM] / o[:, HEAD_DIM:HEAD_DIM + 1]).astype(o_ref.dtype)


def _gqa_values_with_ones(v_all):
    b, s, _ = v_all.shape
    tail = jnp.concatenate([jnp.ones((b, s, 1), v_all.dtype), jnp.zeros((b, s, HEAD_DIM - 1), v_all.dtype)], axis=2)
    return jnp.stack([jnp.concatenate([v_all[:, :, g * HEAD_DIM:(g + 1) * HEAD_DIM], tail], axis=2)
                      for g in range(GQ_KV_HEADS)], axis=1)


def _gqa_latent(q, k_all, v_all, *, n_seq, seq_len):
    per = seq_len // TQ_GQA
    s_all = k_all.shape[1]
    return pl.pallas_call(
        _gqa_lat_kernel,
        grid=(n_seq, per),
        in_specs=[
            pl.BlockSpec((TQ_GQA, GQ_Q_WIDTH), lambda b, t: (b * per + t, 0)),
            pl.BlockSpec((None, s_all, GQ_KV_WIDTH), lambda b, t: (b, 0, 0)),
            pl.BlockSpec((None, GQ_KV_HEADS, s_all, 2 * HEAD_DIM), lambda b, t: (b, 0, 0, 0)),
        ],
        out_specs=pl.BlockSpec((TQ_GQA, GQ_Q_WIDTH), lambda b, t: (b * per + t, 0)),
        out_shape=jax.ShapeDtypeStruct(q.shape, BF16),
        compiler_params=_cparams(("arbitrary", "arbitrary"), 48),
        name="attn_gqa_lat",
    )(q, k_all, _gqa_values_with_ones(v_all))


def _conv_kernel(u_ref, up_ref, un_ref, w_ref, b_ref, g_ref, beta_ref, o_ref, shifted_ref):
    t = pl.program_id(1)
    nt = pl.num_programs(1)
    tl = u_ref.shape[0]

    def glu(v):
        return v[:, :CONV_CH].astype(F32) * _sigmoid(v[:, CONV_CH:].astype(F32))

    slab = jnp.concatenate([jnp.where(t > 0, glu(up_ref[...]), 0.0), glu(u_ref[...]),
                            jnp.where(t < nt - 1, glu(un_ref[...]), 0.0)], axis=0)
    first = HALO - CONV_K // 2
    span = shifted_ref.shape[1]
    assert (first + CONV_K - 1) // 8 * 8 + tl == span and span + 7 <= slab.shape[0]
    acc = jnp.zeros((tl, CONV_CH), F32)
    for shift in range(8):
        shifted_ref[shift % 2] = slab[shift:shift + span]
        for k in range(CONV_K):
            if (first + k) % 8 == shift:
                base = first + k - shift
                acc = acc + shifted_ref[shift % 2, base:base + tl, :] * w_ref[k:k + 1, :]
    y = _layer_norm(acc + b_ref[...], g_ref[...], beta_ref[...])
    o_ref[...] = (y * _sigmoid(y)).astype(o_ref.dtype)


def _conformer_conv(u, conv_w, conv_b, ln_g, ln_b, *, n_seq, seq_len):
    tl = TL_CONV
    nt = seq_len // tl
    hb = tl // HALO
    last_hb = seq_len // HALO - 1
    u3 = u.reshape(n_seq, seq_len, 2 * CONV_CH)
    vec = lambda a: a.reshape(1, CONV_CH)
    out = pl.pallas_call(
        _conv_kernel,
        grid=(n_seq, nt),
        in_specs=[
            pl.BlockSpec((None, tl, 2 * CONV_CH), lambda s, t: (s, t, 0)),
            pl.BlockSpec((None, HALO, 2 * CONV_CH), lambda s, t: (s, jnp.maximum(t * hb - 1, 0), 0)),
            pl.BlockSpec((None, HALO, 2 * CONV_CH), lambda s, t: (s, jnp.minimum((t + 1) * hb, last_hb), 0)),
            _resident((CONV_K, CONV_CH)),
            _resident((1, CONV_CH)),
            _resident((1, CONV_CH)),
            _resident((1, CONV_CH)),
        ],
        out_specs=pl.BlockSpec((None, tl, CONV_CH), lambda s, t: (s, t, 0)),
        out_shape=jax.ShapeDtypeStruct((n_seq, seq_len, CONV_CH), BF16),
        scratch_shapes=[pltpu.VMEM((2, (HALO + CONV_K // 2) // 8 * 8 + tl, CONV_CH), F32)],
        compiler_params=_cparams(("arbitrary", "arbitrary"), 32),
        name="conformer_conv",
    )(u3, u3, u3, conv_w, vec(conv_b), vec(ln_g), vec(ln_b))
    return out.reshape(n_seq * seq_len, CONV_CH)


def _router_gates(logits):
    lane = lax.broadcasted_iota(jnp.int32, logits.shape, 1)
    m = jnp.max(logits, axis=-1, keepdims=True)
    e = jnp.exp(logits - m)
    probs = e / jnp.sum(e, axis=-1, keepdims=True)
    p1 = jnp.max(probs, axis=-1, keepdims=True)
    i1 = jnp.min(jnp.where(probs == p1, lane, LANES), axis=-1, keepdims=True)
    in_group = (lane // EXPERTS_PER_GROUP) == (i1 // EXPERTS_PER_GROUP)
    cand = jnp.where(in_group, jnp.where(lane == i1, -1.0, probs), -1.0)
    p2 = jnp.max(cand, axis=-1, keepdims=True)
    i2 = jnp.min(jnp.where(cand == p2, lane, LANES), axis=-1, keepdims=True)
    den = p1 + p2
    gates = jnp.where(lane == i1, p1 / den, 0.0) + jnp.where(lane == i2, p2 / den, 0.0)
    group = (i1 // EXPERTS_PER_GROUP).astype(F32)
    return jnp.where(lane == GROUP_LANE, group, gates)


def _post_kernel(oa_ref, ob_ref, oc_ref, gt_ref, x_ref, g1_ref, lg_ref, lb_ref, sc2_ref, sh2_ref,
                 wa_ref, wb_ref, wc_ref, wo_ref, wr_ref, br_ref, x1_ref, hg_ref, gate_ref):
    sg = _sigmoid(gt_ref[...].astype(F32))
    merged = (sg[:, :D_MODEL] * _dot(oa_ref[...], wa_ref[...])
              + sg[:, D_MODEL:2 * D_MODEL] * _dot(ob_ref[...], wb_ref[...])
              + sg[:, 2 * D_MODEL:] * _dot(oc_ref[...], wc_ref[...]))
    y = _dot(merged.astype(BF16), wo_ref[...])
    x1 = _layer_norm(ALPHA * x_ref[...] + g1_ref[...] * y, lg_ref[...], lb_ref[...])
    x1_ref[...] = x1
    h2 = x1 * sc2_ref[...] + sh2_ref[...]
    tm = x1.shape[0]
    gates = _router_gates(_dot3(h2, wr_ref[...]) + br_ref[...])
    gate_ref[...] = gates
    block = jnp.concatenate([h2, gates, jnp.zeros((tm, (HG_ROWS - HG_USED) * LANES), F32)], axis=1)
    hg_ref[...] = pltpu.einshape("t(jl)->(tj)l", block, j=HG_ROWS)


def _post(oa, ob, oc, gates, x, g1, ln_g, ln_b, scale2, shift2, wa, wb, wc, wo, wr, br, *, latent, seq_len):
    t = x.shape[0]
    tm = TM_POST
    row = lambda i: (i, 0)
    mod_map = _mod_row_map(latent, seq_len // tm)
    mod = pl.BlockSpec((None, 1, D_MODEL), mod_map)
    return pl.pallas_call(
        _post_kernel,
        grid=(t // tm,),
        in_specs=[
            pl.BlockSpec((tm, NA_WIDTH), row), pl.BlockSpec((tm, CONV_CH), row), pl.BlockSpec((tm, GQ_Q_WIDTH), row),
            pl.BlockSpec((tm, 3 * D_MODEL), row), pl.BlockSpec((tm, D_MODEL), row),
            mod, _resident((1, D_MODEL)), _resident((1, D_MODEL)), mod, mod,
            _resident(wa.shape), _resident(wb.shape), _resident(wc.shape), _resident(wo.shape),
            _resident(wr.shape), _resident(br.shape),
        ],
        out_specs=[pl.BlockSpec((tm, D_MODEL), row), pl.BlockSpec((tm * HG_ROWS, LANES), row),
                   pl.BlockSpec((tm, LANES), row)],
        out_shape=[jax.ShapeDtypeStruct((t, D_MODEL), F32), jax.ShapeDtypeStruct((t * HG_ROWS, LANES), F32),
                   jax.ShapeDtypeStruct((t, LANES), F32)],
        compiler_params=_cparams(("arbitrary",), 40),
        name="post_lat" if latent else "post_ctx",
    )(oa, ob, oc, gates, x, g1, ln_g, ln_b, scale2, shift2, wa, wb, wc, wo, wr, br)


def _route(grp, tm):
    n = grp.shape[0]
    n_tiles = n // tm + N_GROUPS
    groups = jnp.arange(N_GROUPS, dtype=jnp.int32)
    onehot = (grp[:, None] == groups[None, :]).astype(jnp.int32)
    csum = jnp.cumsum(onehot, axis=0)
    rank = jnp.sum((csum - onehot) * onehot, axis=1)
    counts = csum[-1]
    tiles = (counts + tm - 1) // tm
    tile_end = jnp.cumsum(tiles)
    tile_start = tile_end - tiles
    slot = jnp.sum(onehot * tile_start[None, :], axis=1) * tm + rank
    tile_ids = jnp.arange(n_tiles, dtype=jnp.int32)
    tile_group = jnp.minimum(jnp.sum((tile_ids[:, None] >= tile_end[None, :]).astype(jnp.int32), axis=1), N_GROUPS - 1)
    src = jnp.zeros((n_tiles * tm,), jnp.int32).at[slot].set(jnp.arange(n, dtype=jnp.int32))
    r = jnp.arange(n_tiles * tm, dtype=jnp.int32)
    slot_tile = r // tm
    in_group = (jnp.repeat(tile_group, tm)[:, None] == groups[None, :]).astype(jnp.int32)
    slot_rank = r - jnp.sum(in_group * tile_start[None, :], axis=1) * tm
    valid = (slot_tile < tile_end[-1]) & (slot_rank < jnp.sum(in_group * counts[None, :], axis=1))
    dst = jnp.where(valid, src, n + (slot_tile % 2) * tm + r % tm)
    return tile_group.astype(jnp.int32), src, dst, tile_end[-1:].astype(jnp.int32)


def _moe_kernel(tg_ref, src_ref, dst_ref, nused_ref, h_hbm, wgu_ref, wd_ref, y_hbm, hbuf, ybuf, gsem, ssem):
    t = pl.program_id(0)
    n_used = nused_ref[0]
    slot = t % 2
    tm = ybuf.shape[1] // Y_ROWS

    def gather_row(tile, s, r):
        tok = src_ref[tile * tm + r]
        return pltpu.make_async_copy(h_hbm.at[pl.ds(pl.multiple_of(tok * HG_ROWS, HG_ROWS), HG_USED), :],
                                     hbuf.at[s, pl.ds(pl.multiple_of(r * HG_ROWS, HG_ROWS), HG_USED), :], gsem.at[s])

    def scatter_row(tile, s, r):
        row = dst_ref[tile * tm + r]
        return pltpu.make_async_copy(ybuf.at[s, pl.ds(pl.multiple_of(r * Y_ROWS, Y_ROWS), Y_ROWS), :],
                                     y_hbm.at[pl.ds(pl.multiple_of(row * Y_ROWS, Y_ROWS), Y_ROWS), :], ssem.at[s])

    def for_rows(fn):
        def body(r, carry):
            fn(r)
            return carry
        lax.fori_loop(0, tm, body, 0, unroll=8)

    def start_rows(copy_of_row):
        group = 8

        def body(i, carry):
            for j in range(group):
                copy_of_row(i * group + j).start(priority=j % 2)
            return carry
        lax.fori_loop(0, tm // group, body, 0)

    @pl.when(t == 0)
    def _():
        start_rows(lambda r: gather_row(0, 0, r))

    @pl.when(t + 1 < n_used)
    def _():
        start_rows(lambda r: gather_row(t + 1, 1 - slot, r))

    @pl.when(t < n_used)
    def _():
        for_rows(lambda r: gather_row(t, slot, r).wait())

        @pl.when(t >= 2)
        def _():
            for_rows(lambda r: scatter_row(t - 2, slot, r).wait())

        first = tg_ref[t] * EXPERTS_PER_GROUP
        token_rows = lambda ref, j, per: ref[slot, pl.ds(j, tm, stride=per), :]
        h = jnp.concatenate([token_rows(hbuf, j, HG_ROWS) for j in range(Y_ROWS)], axis=1).astype(BF16)
        gates = token_rows(hbuf, Y_ROWS, HG_ROWS)
        lane = lax.broadcasted_iota(jnp.int32, gates.shape, 1)
        acc = jnp.zeros((tm, D_MODEL), F32)
        for e in range(EXPERTS_PER_GROUP):
            gu = _dot(h, wgu_ref[e])
            a = gu[:, :D_EXPERT]
            ge = jnp.sum(jnp.where(lane == first + e, gates, 0.0), axis=-1, keepdims=True)
            hid = a * _sigmoid(a) * gu[:, D_EXPERT:] * ge
            acc = acc + _dot(hid.astype(BF16), wd_ref[e])
        for j in range(Y_ROWS):
            ybuf[slot, pl.ds(j, tm, stride=Y_ROWS), :] = acc[:, j * LANES:(j + 1) * LANES]
        start_rows(lambda r: scatter_row(t, slot, r))

        @pl.when(t == n_used - 1)
        def _():
            for_rows(lambda r: scatter_row(t, slot, r).wait())

            @pl.when(t >= 1)
            def _():
                for_rows(lambda r: scatter_row(t - 1, 1 - slot, r).wait())

            n_rows = y_hbm.shape[0] - 2 * tm * Y_ROWS
            ybuf[0] = jnp.zeros(ybuf.shape[1:], F32)
            for half in range(2):
                fill = pltpu.make_async_copy(
                    ybuf.at[0], y_hbm.at[pl.ds(n_rows + half * tm * Y_ROWS, tm * Y_ROWS), :], ssem.at[half])
                fill.start()
                fill.wait()


def _moe_experts(hg, route, wgu, wd, layer):
    n = hg.shape[0] // HG_ROWS
    tm = TM_MOE
    tile_group, src, dst, n_used = route
    n_tiles = tile_group.shape[0]
    grid_spec = pltpu.PrefetchScalarGridSpec(
        num_scalar_prefetch=4,
        grid=(n_tiles,),
        in_specs=[
            pl.BlockSpec(memory_space=pl.ANY),
            pl.BlockSpec((None, EXPERTS_PER_GROUP, D_MODEL, 2 * D_EXPERT), lambda t, tg, s, d, nu: (layer, tg[t], 0, 0)),
            pl.BlockSpec((None, EXPERTS_PER_GROUP, D_EXPERT, D_MODEL), lambda t, tg, s, d, nu: (layer, tg[t], 0, 0)),
        ],
        out_specs=pl.BlockSpec(memory_space=pl.ANY),
        scratch_shapes=[
            pltpu.VMEM((2, tm * HG_ROWS, LANES), F32),
            pltpu.VMEM((2, tm * Y_ROWS, LANES), F32),
            pltpu.SemaphoreType.DMA((2,)),
            pltpu.SemaphoreType.DMA((2,)),
        ],
    )
    return pl.pallas_call(
        _moe_kernel,
        grid_spec=grid_spec,
        out_shape=jax.ShapeDtypeStruct(((n + 2 * tm) * Y_ROWS, LANES), F32),
        compiler_params=_cparams(("arbitrary",), 58),
        name="moe_experts",
    )(tile_group, src, dst, n_used, hg, wgu, wd)


def _moe_out_kernel(x1_ref, y_ref, g2_ref, lg_ref, lb_ref, o_ref):
    o_ref[...] = _moe_residual_norm(x1_ref, y_ref, g2_ref, lg_ref, lb_ref)


def _moe_out(x1, y, g2, ln_g, ln_b, *, latent, seq_len):
    t = x1.shape[0]
    tm = TM_POST
    row = lambda i: (i, 0)
    return pl.pallas_call(
        _moe_out_kernel,
        grid=(t // tm,),
        in_specs=[pl.BlockSpec((tm, D_MODEL), row), pl.BlockSpec((tm * Y_ROWS, LANES), row),
                  pl.BlockSpec((None, 1, D_MODEL), _mod_row_map(latent, seq_len // tm)),
                  _resident((1, D_MODEL)), _resident((1, D_MODEL))],
        out_specs=pl.BlockSpec((tm, D_MODEL), row),
        out_shape=jax.ShapeDtypeStruct((t, D_MODEL), F32),
        compiler_params=_cparams(("arbitrary",), 32),
        name="moe_out",
    )(x1, y, g2, ln_g, ln_b)


def _moe(hg, gates, wgu, wd, layer):
    return _moe_experts(hg, _route(gates[:, GROUP_LANE].astype(jnp.int32), TM_MOE), wgu, wd, layer)


def _head_mean_matrix():
    i = jnp.arange(GQ_Q_WIDTH) // HEAD_DIM
    return jnp.where(i[:, None] == i[None, :], 1.0 / HEAD_DIM, 0.0).astype(BF16)


def _rope_tables(n_tokens):
    t = jnp.arange(n_tokens, dtype=jnp.int32)
    inv_freq = ROPE_THETA ** (-jnp.arange(0, ROPE_AXIS, 2, dtype=F32) / ROPE_AXIS)
    ar = (t // GRID_W).astype(F32)[:, None] * inv_freq
    ac = (t % GRID_W).astype(F32)[:, None] * inv_freq
    c = jnp.concatenate([jnp.cos(ar), jnp.cos(ar), jnp.cos(ac), jnp.cos(ac)], axis=1)
    s = jnp.concatenate([-jnp.sin(ar), jnp.sin(ar), -jnp.sin(ac), jnp.sin(ac)], axis=1)
    return jnp.tile(c, (1, LANES // HEAD_DIM)), jnp.tile(s, (1, LANES // HEAD_DIM))


def _na_bias_kernel(rows, t_ref, o_ref):
    masked = jnp.full((GRID_W, GRID_W), NEG_INF, F32)
    for pat, first_qrow in enumerate((0, NA_KH // 2, rows - NA_QROWS)):
        krow0 = min(max(first_qrow - NA_KH // 2, 0), rows - NA_WIN_ROWS)
        for kj in range(NA_WIN_ROWS):
            blocks = []
            for qi in range(NA_QROWS):
                qrow, krow = first_qrow + qi, krow0 + kj
                win0 = min(max(qrow - NA_KH // 2, 0), rows - NA_KH)
                in_window = win0 <= krow < win0 + NA_KH
                blocks.append(t_ref[krow - qrow + NA_KH - 1] if in_window else masked)
            o_ref[pat, kj * GRID_W:(kj + 1) * GRID_W, :] = jnp.concatenate(blocks, axis=1)


def _na_bias_table(rpb, rows):
    col = np.arange(GRID_W)
    col_idx = np.clip(col[:, None] - col[None, :], -(NA_KW - 1), NA_KW - 1) + NA_KW - 1
    onehot = (col_idx[None] == np.arange(2 * NA_KW - 1)[:, None, None]).astype(np.float32)
    col_start = np.clip(col - NA_KW // 2, 0, GRID_W - NA_KW)
    col_ok = (col[:, None] >= col_start[None, :]) & (col[:, None] < col_start[None, :] + NA_KW)
    t = jnp.einsum("lhic,ckq->lhikq", rpb, onehot, precision=lax.Precision.HIGHEST)
    t = jnp.where(col_ok, t, NEG_INF)
    depth, heads, n_rel = t.shape[:3]
    nk, nq = NA_WIN_ROWS * GRID_W, NA_QROWS * GRID_W
    return pl.pallas_call(
        functools.partial(_na_bias_kernel, rows),
        grid=(depth, heads),
        in_specs=[pl.BlockSpec((None, None, n_rel, GRID_W, GRID_W), lambda l, h: (l, h, 0, 0, 0))],
        out_specs=pl.BlockSpec((None, 3, None, nk, nq), lambda l, h: (l, 0, h, 0, 0)),
        out_shape=jax.ShapeDtypeStruct((depth, 3, heads, nk, nq), F32),
        compiler_params=_cparams(("arbitrary", "arbitrary"), 32),
        name="na_bias_table",
    )(t)


def _stream_layer(x, mods, l, wts, *, latent, n_seq, seq_len, cache=None, rope=None):
    sh1, sc1, g1, sh2, sc2, g2 = mods
    pieces = _proj(x, 1.0 + sc1, sh1, wts["w_in"], wts["gmat"], wts["qn"][l], wts["kn"][l], rope,
                   latent=latent, seq_len=seq_len, layer=l)
    if isinstance(x, tuple):
        x, pieces = pieces[0], pieces[1:]
    qa, ka, va, u, qc, kc, vc, gates = pieces[:8]
    new_cache_t = pieces[8:]
    if latent:
        nat_k, nat_v, gqa_k, gqa_v = cache
        swap = lambda a: jnp.swapaxes(a, 1, 2)
        out_a = _na_latent(qa, ka, swap(va.reshape(n_seq, seq_len, NA_WIDTH)), nat_k, swap(nat_v),
                           wts["na_bias"][l], n_seq=n_seq, seq_len=seq_len)
        k_all = jnp.concatenate([gqa_k, kc.reshape(n_seq, seq_len, GQ_KV_WIDTH)], axis=1)
        v_all = jnp.concatenate([gqa_v, vc.reshape(n_seq, seq_len, GQ_KV_WIDTH)], axis=1)
        out_c = _gqa_latent(qc, k_all, v_all, n_seq=n_seq, seq_len=seq_len)
    else:
        out_a, out_c = _ctx_attention(qa, ka, new_cache_t[1], qc, kc, new_cache_t[3], seq_len=seq_len)
    out_b = _conformer_conv(u, wts["conv_w"][l], wts["conv_b"][l], wts["conv_ln_g"][l], wts["conv_ln_b"][l],
                            n_seq=n_seq, seq_len=seq_len)
    x1, hg, moe_gates = _post(out_a, out_b, out_c, gates, x, g1, wts["ln1_g"][l], wts["ln1_b"][l], 1.0 + sc2, sh2,
                         wts["w_br_a"][l], wts["w_br_b"][l], wts["w_br_c"][l], wts["w_out"][l],
                         wts["w_router"], wts["b_router"], latent=latent, seq_len=seq_len)
    y = _moe(hg, moe_gates, wts["w_gate_up"], wts["w_down"], l)
    return (x1, y, g2, wts["ln2_g"][l], wts["ln2_b"][l]), new_cache_t


def kernel(x_prompt, x_sample, cache_nat_k, cache_nat_v, cache_gqa_k, cache_gqa_v, c, c_ctx, w_mod, b_mod, w_in, nat_rpb, conv_w, conv_b, conv_ln_g, conv_ln_b, q_norm_g, k_norm_g, w_br_a, w_br_b, w_br_c, w_out, ln1_g, ln1_b, ln2_g, ln2_b, w_router, b_router, w_gate_up, w_down):
    batch, seq, _ = x_prompt.shape
    dec_batch, dec_seq, _ = x_sample.shape
    past = cache_nat_k.shape[2]
    assert dec_batch == 2 and (batch * seq) % TM_MOE == 0 and dec_seq % TM_MOE == 0 and seq % TL_CONV == 0
    assert TM_MOE % TM_PROJ == 0 and TM_MOE % TM_POST == 0

    cond8 = jnp.zeros((8, D_MODEL), F32).at[0].set(c_ctx).at[1:1 + dec_batch].set(c)
    mods_all = _modulation(cond8, w_mod, b_mod)

    pad = LANES - N_EXPERTS
    vecd = lambda a: a.reshape(DEPTH, 1, -1)
    per_layer_bf16 = lambda w: [w[l].astype(BF16) for l in range(DEPTH)]
    rows = dec_seq // GRID_W
    wts = dict(
        w_in=w_in.astype(BF16),
        gmat=_head_mean_matrix(),
        qn=jnp.tile(q_norm_g, (1, GQ_HEADS)).reshape(DEPTH, 1, GQ_Q_WIDTH),
        kn=jnp.tile(k_norm_g, (1, GQ_KV_HEADS)).reshape(DEPTH, 1, GQ_KV_WIDTH),
        na_bias=[_na_bias_table(nat_rpb[l:l + 1], rows).reshape(3, NA_HEADS, NA_WIN_ROWS * GRID_W, NA_QROWS * GRID_W)
                 for l in range(DEPTH)],
        conv_w=conv_w, conv_b=conv_b, conv_ln_g=conv_ln_g, conv_ln_b=conv_ln_b,
        ln1_g=vecd(ln1_g), ln1_b=vecd(ln1_b), ln2_g=vecd(ln2_g), ln2_b=vecd(ln2_b),
        w_br_a=per_layer_bf16(w_br_a), w_br_b=per_layer_bf16(w_br_b), w_br_c=per_layer_bf16(w_br_c),
        w_out=per_layer_bf16(w_out),
        w_router=jnp.pad(w_router, ((0, 0), (0, pad))),
        b_router=jnp.pad(b_router, (0, pad), constant_values=NEG_INF).reshape(1, LANES),
        w_gate_up=w_gate_up.astype(BF16), w_down=w_down.astype(BF16),
    )
    rope = _rope_tables(dec_seq)

    def layer_mods(l):
        m = mods_all[l, :1 + dec_batch].reshape(1 + dec_batch, 6, 1, D_MODEL)
        return [m[:, i] for i in range(6)]

    xp = x_prompt.reshape(batch * seq, D_MODEL)
    xs = x_sample.reshape(dec_batch * dec_seq, D_MODEL)
    new_cache = []
    for l in range(DEPTH):
        mods = layer_mods(l)
        xp, ctx_kv = _stream_layer(xp, mods, l, wts, latent=False, n_seq=batch, seq_len=seq)
        new_cache.append(ctx_kv)
        cache = (cache_nat_k[:, l].reshape(dec_batch, past, NA_WIDTH).astype(BF16),
                 cache_nat_v[:, l].reshape(dec_batch, past, NA_WIDTH).astype(BF16),
                 cache_gqa_k[:, l].reshape(dec_batch, past, GQ_KV_WIDTH).astype(BF16),
                 cache_gqa_v[:, l].reshape(dec_batch, past, GQ_KV_WIDTH).astype(BF16))
        xs, _ = _stream_layer(xs, mods, l, wts, latent=True, n_seq=dec_batch, seq_len=dec_seq, cache=cache, rope=rope)
    xp = _moe_out(*xp, latent=False, seq_len=seq)
    xs = _moe_out(*xs, latent=True, seq_len=dec_seq)

    def stack(i, heads):
        per_layer = jnp.stack([kv[i].reshape(batch, heads, HEAD_DIM, seq) for kv in new_cache], axis=1)
        return per_layer.transpose(0, 1, 4, 2, 3)

    return (xp.reshape(batch, seq, D_MODEL), xs.reshape(dec_batch, dec_seq, D_MODEL),
            stack(0, NA_HEADS), stack(1, NA_HEADS), stack(2, GQ_KV_HEADS), stack(3, GQ_KV_HEADS))
```
